```python
import math
import jax, jax.numpy as jnp
from jax import lax
import numpy as np

D_MODEL = 1024
BATCH = 2
SEQ = 16384
DEPTH = 2
DEC_BATCH = 32
DEC_SEQ = 16
PAST_LEN = 1024

CHUNK = 64
N_HEADS = 8
HEAD_DIM = 64
D_ATT = N_HEADS * HEAD_DIM
N_IDX_HEADS = 8
IDX_DIM = 64
TOPK_MAX = 256
D_SSM = 512
SSM_GROUP = 16
N_GROUPS = D_SSM // SSM_GROUP
SSM_STATE = 64
D_FF = 2816
CONV_W = 3
ROPE_THETA = 10000.0
Q_BLOCK = 128
LN_EPS = 1e-5
ALPHA = (2 * DEPTH) ** 0.25
BETA = (8 * DEPTH) ** -0.25
N_IN = 3 * D_ATT + N_IDX_HEADS * IDX_DIM + IDX_DIM + N_IDX_HEADS + D_SSM + 2 * D_MODEL

kernel_name = "hybrid_dsa_s5_convffn_stream_step"


def _split_points():
    widths = (D_ATT, D_ATT, D_ATT, N_IDX_HEADS * IDX_DIM, IDX_DIM, N_IDX_HEADS, D_SSM, 2 * D_MODEL)
    pts, acc = [], 0
    for w in widths[:-1]:
        acc += w
        pts.append(acc)
    return pts


def layer_norm(x, g, b):
    xf = x.astype(jnp.float32)
    mu = jnp.mean(xf, axis=-1, keepdims=True)
    var = jnp.mean(jnp.square(xf - mu), axis=-1, keepdims=True)
    return ((xf - mu) * lax.rsqrt(var + LN_EPS) * g + b).astype(x.dtype)


def rope(x, pos):
    half = x.shape[-1] // 2
    inv = ROPE_THETA ** (-jnp.arange(half, dtype=jnp.float32) / half)
    ang = pos.astype(jnp.float32)[:, None] * inv[None, :]
    cos = jnp.cos(ang)[:, None, :].astype(x.dtype)
    sin = jnp.sin(ang)[:, None, :].astype(x.dtype)
    x1, x2 = x[..., :half], x[..., half:]
    return jnp.concatenate([x1 * cos - x2 * sin, x1 * sin + x2 * cos], axis=-1)


def dsa_attend(q, qi, wi, q_pos, k_all, v_all, ki_all, k_keep):
    L = k_all.shape[1]
    s = jnp.einsum('bqhd,bkd->bqhk', qi, ki_all).astype(jnp.float32) * IDX_DIM ** -0.5
    score = jnp.einsum('bqh,bqhk->bqk', wi.astype(jnp.float32), jax.nn.relu(s))
    visible = (jnp.arange(L) // CHUNK)[None, :] <= (q_pos // CHUNK)[:, None]
    score = jnp.where(visible[None], score, -jnp.inf)
    top_val, sel = lax.top_k(score, k_keep)
    valid = top_val > -jnp.inf
    gather = jax.vmap(lambda rows, idx: rows[idx])
    ks = gather(k_all, sel)
    vs = gather(v_all, sel)
    logits = jnp.einsum('bqhd,bqkhd->bqhk', q, ks).astype(jnp.float32) * HEAD_DIM ** -0.5
    logits = jnp.where(valid[:, :, None, :], logits, -jnp.inf)
    p = jax.nn.softmax(logits, axis=-1).astype(v_all.dtype)
    return jnp.einsum('bqhk,bqkhd->bqhd', p, vs)


def s5_branch(u, h0_re, h0_im, a_re, a_im, log_dt, b_re, b_im, c_re, c_im, d_skip, w_glu, b_glu):
    Bn, T, _ = u.shape
    f32 = jnp.float32
    dt = jnp.exp(log_dt.astype(f32))[:, None]
    ar, ai = a_re.astype(f32), a_im.astype(f32)
    mag = jnp.exp(dt * ar)
    abar_re, abar_im = mag * jnp.cos(dt * ai), mag * jnp.sin(dt * ai)
    den = ar * ar + ai * ai
    nr, ni = abar_re - 1.0, abar_im
    f_re, f_im = (nr * ar + ni * ai) / den, (ni * ar - nr * ai) / den
    br, bi = b_re.astype(f32), b_im.astype(f32)
    bbar_re = f_re[..., None] * br - f_im[..., None] * bi
    bbar_im = f_re[..., None] * bi + f_im[..., None] * br
    ug = u.reshape(Bn, T, N_GROUPS, SSM_GROUP).astype(f32)
    bu_re = jnp.einsum('btgc,gpc->btgp', ug, bbar_re)
    bu_im = jnp.einsum('btgc,gpc->btgp', ug, bbar_im)
    bu_re = bu_re.at[:, 0].add(abar_re * h0_re - abar_im * h0_im)
    bu_im = bu_im.at[:, 0].add(abar_re * h0_im + abar_im * h0_re)
    el_re = jnp.broadcast_to(abar_re, bu_re.shape)
    el_im = jnp.broadcast_to(abar_im, bu_im.shape)

    def combine(e1, e2):
        a1r, a1i, b1r, b1i = e1
        a2r, a2i, b2r, b2i = e2
        return (a2r * a1r - a2i * a1i, a2r * a1i + a2i * a1r,
                a2r * b1r - a2i * b1i + b2r, a2r * b1i + a2i * b1r + b2i)

    _, _, h_re, h_im = lax.associative_scan(combine, (el_re, el_im, bu_re, bu_im), axis=1)
    y = (jnp.einsum('btgp,gcp->btgc', h_re, c_re.astype(f32))
         - jnp.einsum('btgp,gcp->btgc', h_im, c_im.astype(f32)))
    y = y.reshape(Bn, T, D_SSM) + d_skip.astype(f32) * u.astype(f32)
    y = jax.nn.gelu(y).astype(u.dtype)
    y = y * jax.nn.sigmoid(y @ w_glu + b_glu)
    return y, h_re[:, -1], h_im[:, -1]


def trunk_layer(x, past_k, past_v, past_ki, h0_re, h0_im, conv_buf,
                w_in, b_gate, w_pa, w_pb, w_o, a_re, a_im, log_dt, b_re, b_im, c_re, c_im,
                d_skip, w_glu, b_glu, ln1_g, ln1_b, w_up, conv_w, conv_b, w_down, ln2_g, ln2_b):
    Bn, T, _ = x.shape
    P = past_k.shape[1]
    pos = P + jnp.arange(T)
    q, k, v, qi, ki, wi, u, gate_logits = jnp.split(x @ w_in, _split_points(), axis=-1)
    q = rope(q.reshape(Bn, T, N_HEADS, HEAD_DIM), pos)
    k = rope(k.reshape(Bn, T, N_HEADS, HEAD_DIM), pos)
    v = v.reshape(Bn, T, N_HEADS, HEAD_DIM)
    qi = rope(qi.reshape(Bn, T, N_IDX_HEADS, IDX_DIM), pos)
    ki = rope(ki.reshape(Bn, T, 1, IDX_DIM), pos)[:, :, 0]
    wi = wi * N_IDX_HEADS ** -0.5
    k_all = jnp.concatenate([past_k, k], axis=1)
    v_all = jnp.concatenate([past_v, v], axis=1)
    ki_all = jnp.concatenate([past_ki, ki], axis=1)
    L = P + T
    k_keep = min(TOPK_MAX, L // 4)

    if T % Q_BLOCK == 0 and T > Q_BLOCK:
        def block(i):
            q0 = i * Q_BLOCK
            sl = lambda a: lax.dynamic_slice_in_dim(a, q0, Q_BLOCK, axis=1)
            return dsa_attend(sl(q), sl(qi), sl(wi), lax.dynamic_slice_in_dim(pos, q0, Q_BLOCK),
                              k_all, v_all, ki_all, k_keep)
        att = lax.map(block, jnp.arange(T // Q_BLOCK))
        att = jnp.moveaxis(att, 0, 1).reshape(Bn, T, D_ATT)
    else:
        att = dsa_attend(q, qi, wi, pos, k_all, v_all, ki_all, k_keep).reshape(Bn, T, D_ATT)

    ssm, h_re, h_im = s5_branch(u, h0_re, h0_im, a_re, a_im, log_dt, b_re, b_im, c_re, c_im,
                                d_skip, w_glu, b_glu)

    g = jax.nn.sigmoid(gate_logits + b_gate).reshape(Bn, T, 2, D_MODEL)
    mix = g[:, :, 0] * (att @ w_pa) + g[:, :, 1] * (ssm @ w_pb)
    x = layer_norm(ALPHA * x + mix @ w_o, ln1_g, ln1_b)

    h = x @ w_up
    hp = jnp.concatenate([conv_buf.astype(h.dtype), h], axis=1)
    hc = conv_b + sum(conv_w[j] * hp[:, j:j + T] for j in range(CONV_W))
    a_half, g_half = hc[..., :D_FF], hc[..., D_FF:]
    f = (jax.nn.gelu(a_half) * g_half) @ w_down
    x = layer_norm(ALPHA * x + f, ln2_g, ln2_b)
    new_conv = hp[:, -(CONV_W - 1):]
    return x, k, v, ki, h_re, h_im, new_conv


def setup_inputs(seed: int = 0) -> dict:
    key = jax.random.key(seed)
    ks = iter(jax.random.split(key, 48))
    nrm = lambda shape, scale: scale * jax.random.normal(next(ks), shape, jnp.float32)
    D = D_MODEL
    return {
        "x_prompt": nrm((BATCH, SEQ, D), 1.0),
        "x_sample": nrm((DEC_BATCH, DEC_SEQ, D), 1.0),
        "cache_k": nrm((DEPTH, DEC_BATCH, PAST_LEN, N_HEADS, HEAD_DIM), 1.0),
        "cache_v": nrm((DEPTH, DEC_BATCH, PAST_LEN, N_HEADS, HEAD_DIM), 1.0),
        "cache_idx_k": nrm((DEPTH, DEC_BATCH, PAST_LEN, IDX_DIM), 1.0),
        "state_ssm_re": nrm((DEPTH, DEC_BATCH, N_GROUPS, SSM_STATE), 0.1),
        "state_ssm_im": nrm((DEPTH, DEC_BATCH, N_GROUPS, SSM_STATE), 0.1),
        "state_conv": nrm((DEPTH, DEC_BATCH, CONV_W - 1, 2 * D_FF), 1.0),
        "w_in": nrm((DEPTH, D, N_IN), D ** -0.5),
        "b_gate": nrm((DEPTH, 2 * D), 0.02),
        "w_pa": nrm((DEPTH, D_ATT, D), D_ATT ** -0.5),
        "w_pb": nrm((DEPTH, D_SSM, D), D_SSM ** -0.5),
        "w_o": nrm((DEPTH, D, D), BETA * D ** -0.5),
        "a_re": -0.5 + nrm((DEPTH, N_GROUPS, SSM_STATE), 0.01),
        "a_im": math.pi * jnp.arange(SSM_STATE, dtype=jnp.float32) + nrm((DEPTH, N_GROUPS, SSM_STATE), 0.01),
        "log_dt": jax.random.uniform(next(ks), (DEPTH, N_GROUPS), jnp.float32,
                                     minval=math.log(0.001), maxval=math.log(0.1)),
        "b_re": nrm((DEPTH, N_GROUPS, SSM_STATE, SSM_GROUP), (2 * SSM_GROUP) ** -0.5),
        "b_im": nrm((DEPTH, N_GROUPS, SSM_STATE, SSM_GROUP), (2 * SSM_GROUP) ** -0.5),
        "c_re": nrm((DEPTH, N_GROUPS, SSM_GROUP, SSM_STATE), (2 * SSM_STATE) ** -0.5),
        "c_im": nrm((DEPTH, N_GROUPS, SSM_GROUP, SSM_STATE), (2 * SSM_STATE) ** -0.5),
        "d_skip": 1.0 + nrm((DEPTH, D_SSM), 0.1),
        "w_glu": nrm((DEPTH, D_SSM, D_SSM), D_SSM ** -0.5),
        "b_glu": nrm((DEPTH, D_SSM), 0.02),
        "ln1_g": 1.0 + nrm((DEPTH, D), 0.05),
        "ln1_b": nrm((DEPTH, D), 0.02),
        "w_up": nrm((DEPTH, D, 2 * D_FF), D ** -0.5),
        "conv_w": nrm((DEPTH, CONV_W, 2 * D_FF), CONV_W ** -0.5),
        "conv_b": nrm((DEPTH, 2 * D_FF), 0.02),
        "w_down": nrm((DEPTH, D_FF, D), BETA * D_FF ** -0.5),
        "ln2_g": 1.0 + nrm((DEPTH, D), 0.05),
        "ln2_b": nrm((DEPTH, D), 0.02),
    }


def reference(x_prompt, x_sample, cache_k, cache_v, cache_idx_k, state_ssm_re, state_ssm_im, state_conv,
              w_in, b_gate, w_pa, w_pb, w_o, a_re, a_im, log_dt, b_re, b_im, c_re, c_im, d_skip,
              w_glu, b_glu, ln1_g, ln1_b, w_up, conv_w, conv_b, w_down, ln2_g, ln2_b):
    Bp = x_prompt.shape[0]
    dt_ = x_prompt.dtype
    empty_k = jnp.zeros((Bp, 0, N_HEADS, HEAD_DIM), dt_)
    empty_ki = jnp.zeros((Bp, 0, IDX_DIM), dt_)
    zero_h = jnp.zeros((Bp, N_GROUPS, SSM_STATE), jnp.float32)
    zero_conv = jnp.zeros((Bp, CONV_W - 1, 2 * D_FF), dt_)

    xp, xs = x_prompt, x_sample
    kp, vp, kip, hrp, hip, cvp = [], [], [], [], [], []
    kss, vss, kis, hrs, his, cvs = [], [], [], [], [], []
    for l in range(DEPTH):
        lw = (w_in[l], b_gate[l], w_pa[l], w_pb[l], w_o[l], a_re[l], a_im[l], log_dt[l], b_re[l], b_im[l],
              c_re[l], c_im[l], d_skip[l], w_glu[l], b_glu[l], ln1_g[l], ln1_b[l], w_up[l], conv_w[l],
              conv_b[l], w_down[l], ln2_g[l], ln2_b[l])
        xp, k1, v1, ki1, hr1, hi1, cv1 = trunk_layer(xp, empty_k, empty_k, empty_ki, zero_h, zero_h,
                                                     zero_conv, *lw)
        xs, k2, v2, ki2, hr2, hi2, cv2 = trunk_layer(xs, cache_k[l], cache_v[l], cache_idx_k[l],
                                                     state_ssm_re[l].astype(jnp.float32),
                                                     state_ssm_im[l].astype(jnp.float32),
                                                     state_conv[l], *lw)
        kp.append(k1); vp.append(v1); kip.append(ki1); hrp.append(hr1); hip.append(hi1); cvp.append(cv1)
        kss.append(k2); vss.append(v2); kis.append(ki2); hrs.append(hr2); his.append(hi2); cvs.append(cv2)
    st = lambda a: jnp.stack(a, axis=0)
    return (xp, xs,
            st(kp), st(vp), st(kip), st(hrp), st(hip), st(cvp),
            st(kss), st(vss), st(kis), st(hrs), st(his), st(cvs))
```

```python
import functools
import math

import jax
import jax.numpy as jnp
from jax import lax
from jax.experimental import pallas as pl
from jax.experimental.pallas import tpu as pltpu

CHUNK = 64
N_HEADS = 8
HEAD_DIM = 64
D_ATT = N_HEADS * HEAD_DIM
N_IDX_HEADS = 8
IDX_DIM = 64
TOPK_MAX = 256
D_SSM = 512
SSM_GROUP = 16
N_GROUPS = D_SSM // SSM_GROUP
SSM_STATE = 64
N_STATE = N_GROUPS * SSM_STATE
CONV_W = 3
ROPE_THETA = 10000.0
LN_EPS = 1e-5

LANES = 128
SUBLANES = 8
VMEM_LIMIT = 60 * 1024 * 1024

F32 = jnp.float32
BF16 = jnp.bfloat16
I32 = jnp.int32
NEG_INF = float("-inf")
INT_MIN = -(2 ** 31)


def _f32_key_of_neg_inf():
    bits = 0xFF800000
    signed = bits - (1 << 32)
    return signed ^ ((signed >> 31) & 0x7FFFFFFF)


NEG_INF_KEY = _f32_key_of_neg_inf()


def _resident(block_shape, index_map):
    return pl.BlockSpec(block_shape, index_map, pipeline_mode=pl.Buffered(1))


def _params(*sem):
    return pltpu.CompilerParams(dimension_semantics=sem, vmem_limit_bytes=VMEM_LIMIT)


def _inproj_kernel(x_ref, cos_ref, sin_ref, wq_ref, wk_ref, wv_ref, wqi_ref, wkw_ref, wu_ref,
                   q_ref, kf_ref, kb_ref, vf_ref, vb_ref, qi_ref, kw_ref, u_ref, *, wi_scale):
    x = x_ref[...].astype(BF16)
    cos = cos_ref[...]
    sin = sin_ref[...]
    tm = x.shape[0]
    lane = lax.broadcasted_iota(I32, (tm, LANES), 1)
    first_half = (lane % HEAD_DIM) < (HEAD_DIM // 2)

    def proj(w_ref):
        return jnp.dot(x, w_ref[...], preferred_element_type=F32)

    def rope_group(y):
        rot = jnp.where(first_half, pltpu.roll(y, LANES - HEAD_DIM // 2, 1), pltpu.roll(y, HEAD_DIM // 2, 1))
        return y * cos + rot * sin

    def rope(y):
        return jnp.concatenate([rope_group(y[:, c * LANES:(c + 1) * LANES]) for c in range(y.shape[1] // LANES)],
                               axis=1)

    q = rope(proj(wq_ref))
    q_ref[...] = (q * (HEAD_DIM ** -0.5)).astype(BF16)
    k = rope(proj(wk_ref))
    kf_ref[...] = k
    kb_ref[...] = k.astype(BF16)
    v = proj(wv_ref)
    vf_ref[...] = v
    vb_ref[...] = v.astype(BF16)
    qi_ref[...] = rope(proj(wqi_ref)).astype(BF16)
    kw = proj(wkw_ref)
    kw_ref[...] = jnp.where(lane < IDX_DIM, rope_group(kw), kw * wi_scale)
    u_ref[...] = proj(wu_ref)


def _inproj(x2d, cos_tab, sin_tab, ws, tm, n_tab_blocks):
    rows, d = x2d.shape
    wq, wk, wv, wqi, wkw, wu = ws
    row_blk = lambda w: pl.BlockSpec((tm, w), lambda i: (i, 0))
    tab_blk = pl.BlockSpec((tm, LANES), lambda i: (i % n_tab_blocks, 0))
    w_blk = lambda w: _resident(w.shape, lambda i: (0, 0))
    out_shapes = [
        jax.ShapeDtypeStruct((rows, D_ATT), BF16),
        jax.ShapeDtypeStruct((rows, D_ATT), F32),
        jax.ShapeDtypeStruct((rows, D_ATT), BF16),
        jax.ShapeDtypeStruct((rows, D_ATT), F32),
        jax.ShapeDtypeStruct((rows, D_ATT), BF16),
        jax.ShapeDtypeStruct((rows, D_ATT), BF16),
        jax.ShapeDtypeStruct((rows, LANES), F32),
        jax.ShapeDtypeStruct((rows, D_SSM), F32),
    ]
    wi_scale = (N_IDX_HEADS ** -0.5) * (IDX_DIM ** -0.5)
    return pl.pallas_call(
        functools.partial(_inproj_kernel, wi_scale=wi_scale),
        grid=(rows // tm,),
        in_specs=[row_blk(d), tab_blk, tab_blk, w_blk(wq), w_blk(wk), w_blk(wv), w_blk(wqi), w_blk(wkw), w_blk(wu)],
        out_specs=[row_blk(s.shape[1]) for s in out_shapes],
        out_shape=out_shapes,
        compiler_params=_params("parallel"),
    )(x2d, cos_tab, sin_tab, wq, wk, wv, wqi, wkw, wu)


def _attn_kernel(q_ref, qi_ref, kw_ref, kiki_ref, k_ref, v_ref, o_ref, key_s, m_s, l_s, acc_s,
                 *, tq, tk, n_keys, past, k_keep, idx_bits):
    q0 = pl.program_id(1) * tq
    n_pairs = N_HEADS // 2
    row = lax.broadcasted_iota(I32, (tq, 1), 0)
    q_lim = jnp.minimum(((past + q0 + row) // CHUNK + 1) * CHUNK, n_keys)
    blk_lim = jnp.minimum(((past + q0 + tq - 1) // CHUNK + 1) * CHUNK, n_keys)
    nk = (blk_lim + tk - 1) // tk
    lane_q = lax.broadcasted_iota(I32, (tq, LANES), 1)
    lo_half = lane_q < HEAD_DIM
    col = lax.broadcasted_iota(I32, (tq, tk), 1)
    nt = (((1,), (1,)), ((), ()))

    def head_masked(ref, h):
        pair = ref[0, :, (h // 2) * LANES:(h // 2 + 1) * LANES]
        keep = lo_half if h % 2 == 0 else jnp.logical_not(lo_half)
        return jnp.where(keep, pair, jnp.zeros_like(pair))

    kw = kw_ref[0]
    wi = [kw[:, IDX_DIM + h:IDX_DIM + h + 1] for h in range(N_IDX_HEADS)]
    qi_heads = [head_masked(qi_ref, h) for h in range(N_IDX_HEADS)]

    def score_tile(kt, carry):
        k0 = pl.multiple_of(kt * tk, tk)
        kiki = kiki_ref[0, pl.ds(k0, tk), :]
        score = jnp.zeros((tq, tk), F32)
        for h in range(N_IDX_HEADS):
            s = lax.dot_general(qi_heads[h], kiki, nt, preferred_element_type=F32)
            score = score + wi[h] * jnp.maximum(s, 0.0)
        score = jnp.where(k0 + col < q_lim, score, NEG_INF)
        bits = pltpu.bitcast(score, I32)
        key_s[:, pl.ds(k0, tk)] = bits ^ ((bits >> 31) & 0x7FFFFFFF)
        return carry

    lax.fori_loop(0, nk, score_tile, 0)

    def count(pred):
        def body(kt, cnt):
            k0 = pl.multiple_of(kt * tk, tk)
            hit = pred(key_s[:, pl.ds(k0, tk)], k0).astype(I32)
            for c in range(tk // LANES):
                cnt = cnt + hit[:, c * LANES:(c + 1) * LANES]
            return cnt
        cnt = lax.fori_loop(0, nk, body, jnp.zeros((tq, LANES), I32))
        return jnp.sum(cnt, axis=1, keepdims=True)

    def count_ge(c):
        cb = jnp.broadcast_to(c, (tq, tk))
        return count(lambda key, k0: key >= cb)

    def search_cond(st):
        bit, _, _, done = st
        return jnp.logical_and(bit >= 0, jnp.min(done) == 0)

    def search_body(st):
        bit, t, thr, done = st
        cand = t + jnp.left_shift(jnp.int32(1), bit)
        cnt = count_ge(cand)
        t = jnp.where(cnt >= k_keep, cand, t)
        newly = jnp.logical_and(cnt == k_keep, done == 0)
        thr = jnp.where(newly, jnp.maximum(cand - 1, NEG_INF_KEY), thr)
        done = jnp.where(newly, 1, done)
        return bit - 1, t, thr, done

    zero = jnp.zeros((tq, 1), I32)
    _, t_fin, thr_early, done = lax.while_loop(
        search_cond, search_body, (jnp.int32(31), jnp.full((tq, 1), INT_MIN, I32), zero, zero))

    def resolve_ties(_):
        cnt_gt = count_ge(t_fin + 1)
        n_eq = count_ge(t_fin) - cnt_gt
        want = k_keep - cnt_gt
        tb = jnp.broadcast_to(t_fin, (tq, tk))
        need = jnp.logical_and(jnp.logical_and(done == 0, n_eq > want), t_fin > NEG_INF_KEY)

        def cut_search(_):
            def body(j, c):
                cand = c + jnp.left_shift(jnp.int32(1), idx_bits - 1 - j)
                cb = jnp.broadcast_to(cand, (tq, tk))
                below = count(lambda key, k0: jnp.logical_and(key == tb, k0 + col < cb))
                return jnp.where(below < want, cand, c)
            return lax.fori_loop(0, idx_bits, body, zero)

        cut = lax.cond(jnp.max(need.astype(I32)) > 0, cut_search, lambda _: zero, 0)
        all_ties = jnp.where(t_fin > NEG_INF_KEY, jnp.int32(2 ** 30), jnp.int32(-1))
        return jnp.where(need, cut, all_ties)

    cut_full = lax.cond(jnp.min(done) == 0, resolve_ties, lambda _: jnp.full((tq, 1), -1, I32), 0)
    thr = jnp.where(done == 1, thr_early, t_fin)
    cut = jnp.where(done == 1, -1, cut_full)
    thr_b = jnp.broadcast_to(thr, (tq, tk))
    cut_b = jnp.broadcast_to(cut, (tq, tk))

    m_s[...] = jnp.full(m_s.shape, NEG_INF, F32)
    l_s[...] = jnp.zeros(l_s.shape, F32)
    acc_s[...] = jnp.zeros(acc_s.shape, F32)
    q_heads = [head_masked(q_ref, h) for h in range(N_HEADS)]

    def attn_tile(kt, carry):
        k0 = pl.multiple_of(kt * tk, tk)
        key = key_s[:, pl.ds(k0, tk)]
        sel = jnp.logical_or(key > thr_b, jnp.logical_and(key == thr_b, k0 + col <= cut_b))
        for h in range(N_HEADS):
            pair = slice((h // 2) * LANES, (h // 2 + 1) * LANES)
            logits = lax.dot_general(q_heads[h], k_ref[0, pl.ds(k0, tk), pair], nt, preferred_element_type=F32)
            logits = jnp.where(sel, logits, NEG_INF)
            m_old = m_s[h]
            m_new = jnp.maximum(m_old, jnp.max(logits, axis=1, keepdims=True))
            m_safe = jnp.where(m_new == NEG_INF, 0.0, m_new)
            alpha = jnp.exp(m_old - m_safe)
            p = jnp.exp(logits - m_safe)
            l_s[h] = alpha * l_s[h] + jnp.sum(p, axis=1, keepdims=True)
            acc_s[h] = alpha * acc_s[h] + jnp.dot(p.astype(BF16), v_ref[0, pl.ds(k0, tk), pair],
                                                  preferred_element_type=F32)
            m_s[h] = m_new
        return carry

    lax.fori_loop(0, nk, attn_tile, 0)

    for pr in range(n_pairs):
        even = acc_s[2 * pr] / l_s[2 * pr]
        odd = acc_s[2 * pr + 1] / l_s[2 * pr + 1]
        o_ref[0, :, pr * LANES:(pr + 1) * LANES] = jnp.where(lo_half, even, odd).astype(o_ref.dtype)


def _pick_key_tile(l_pad, k_keep):
    for tk in (256, 384, 512, 640, 768, 896, 1024):
        if l_pad % tk == 0 and tk >= k_keep:
            return tk
    return l_pad


def _attention(q, qi, kw, kiki, k_all, v_all, *, n_keys, past, k_keep, tq):
    b, t, _ = q.shape
    l_pad = k_all.shape[1]
    tk = _pick_key_tile(l_pad, k_keep)
    assert l_pad % tk == 0 and tk >= k_keep and t % tq == 0
    idx_bits = max(1, int(math.ceil(math.log2(l_pad))))
    qblk = lambda w: pl.BlockSpec((1, tq, w), lambda bi, i: (bi, i, 0))
    kvblk = lambda w: _resident((1, l_pad, w), lambda bi, i: (bi, 0, 0))
    return pl.pallas_call(
        functools.partial(_attn_kernel, tq=tq, tk=tk, n_keys=n_keys, past=past, k_keep=k_keep, idx_bits=idx_bits),
        grid=(b, t // tq),
        in_specs=[qblk(D_ATT), qblk(D_ATT), qblk(LANES), kvblk(LANES), kvblk(D_ATT), kvblk(D_ATT)],
        out_specs=qblk(D_ATT),
        out_shape=jax.ShapeDtypeStruct((b, t, D_ATT), BF16),
        scratch_shapes=[
            pltpu.VMEM((tq, l_pad), I32),
            pltpu.VMEM((N_HEADS, tq, 1), F32),
            pltpu.VMEM((N_HEADS, tq, 1), F32),
            pltpu.VMEM((N_HEADS, tq, LANES), F32),
        ],
        compiler_params=_params("arbitrary", "arbitrary"),
    )(q, qi, kw, kiki, k_all, v_all)


def _cmul(ar, ai, br, bi):
    return ar * br - ai * bi, ar * bi + ai * br


def _s5_kernel(u_ref, h0re_ref, h0im_ref, are_ref, aim_ref, ldt_ref, bre_ref, bim_ref, cre_ref, cim_ref,
               dskip_ref, wglu_ref, bglu_ref, y_ref, hre_ref, him_ref,
               bbar_s, lvl_s, pw_s, h_s, *, ts, lane_chunk):
    n = N_STATE
    t_idx = pl.program_id(1)
    first = jnp.logical_and(pl.program_id(0) == 0, t_idx == 0)

    @pl.when(first)
    def _discretise():
        dt = jnp.exp(ldt_ref[...])
        ar, ai = are_ref[...], aim_ref[...]
        mag = jnp.exp(dt * ar)
        abr, abi = mag * jnp.cos(dt * ai), mag * jnp.sin(dt * ai)
        den = ar * ar + ai * ai
        nr, ni = abr - 1.0, abi
        f_re, f_im = (nr * ar + ni * ai) / den, (ni * ar - nr * ai) / den
        reps = D_SSM // SSM_GROUP
        bre = jnp.concatenate([bre_ref[...]] * reps, axis=0)
        bim = jnp.concatenate([bim_ref[...]] * reps, axis=0)
        r_grp = lax.broadcasted_iota(I32, (D_SSM, n), 0) // SSM_GROUP
        c_grp = lax.broadcasted_iota(I32, (D_SSM, n), 1) // SSM_STATE
        same = r_grp == c_grp
        bbar_s[:, :n] = jnp.where(same, f_re * bre - f_im * bim, 0.0).astype(BF16)
        bbar_s[:, n:] = jnp.where(same, f_re * bim + f_im * bre, 0.0).astype(BF16)
        a1 = (abr, abi)
        a2 = _cmul(*a1, *a1)
        a3 = _cmul(*a2, *a1)
        a4 = _cmul(*a2, *a2)
        a5 = _cmul(*a4, *a1)
        a6 = _cmul(*a4, *a2)
        a7 = _cmul(*a4, *a3)
        a8 = _cmul(*a4, *a4)
        rows = lax.broadcasted_iota(I32, (SUBLANES, n), 0)
        for part in range(2):
            pw = jnp.zeros((SUBLANES, n), F32)
            for r, a in enumerate((a1, a2, a3, a4, a5, a6, a7, a8)):
                pw = jnp.where(rows == r, a[part], pw)
            pw_s[part] = pw
            for lv, (a, dist) in enumerate(((a1, 1), (a2, 2), (a4, 4))):
                lvl_s[lv, part] = jnp.where(rows >= dist, a[part], 0.0)

    @pl.when(t_idx == 0)
    def _load_state():
        h_s[0:SUBLANES, :] = jnp.zeros((SUBLANES, 2 * n), F32)
        h_s[SUBLANES - 1:SUBLANES, :n] = h0re_ref[0]
        h_s[SUBLANES - 1:SUBLANES, n:] = h0im_ref[0]

    u = u_ref[0]
    u_bf = u.astype(BF16)
    for c in range(2 * n // 512):
        h_s[SUBLANES:, c * 512:(c + 1) * 512] = jnp.dot(u_bf, bbar_s[:, c * 512:(c + 1) * 512],
                                                        preferred_element_type=F32)

    def tile_step(j, carry):
        r0 = pl.multiple_of(j * SUBLANES, SUBLANES)
        for c in range(n // lane_chunk):
            re_sl = slice(c * lane_chunk, (c + 1) * lane_chunk)
            im_sl = slice(n + c * lane_chunk, n + (c + 1) * lane_chunk)
            xr = h_s[pl.ds(r0 + SUBLANES, SUBLANES), re_sl]
            xi = h_s[pl.ds(r0 + SUBLANES, SUBLANES), im_sl]
            for lv, dist in enumerate((1, 2, 4)):
                sr, si = pltpu.roll(xr, dist, 0), pltpu.roll(xi, dist, 0)
                dr, di = _cmul(lvl_s[lv, 0, :, re_sl], lvl_s[lv, 1, :, re_sl], sr, si)
                xr, xi = xr + dr, xi + di
            prev_r = h_s[pl.ds(r0, SUBLANES), re_sl][SUBLANES - 1:SUBLANES, :]
            prev_i = h_s[pl.ds(r0, SUBLANES), im_sl][SUBLANES - 1:SUBLANES, :]
            cr, ci = _cmul(pw_s[0, :, re_sl], pw_s[1, :, re_sl], prev_r, prev_i)
            h_s[pl.ds(r0 + SUBLANES, SUBLANES), re_sl] = xr + cr
            h_s[pl.ds(r0 + SUBLANES, SUBLANES), im_sl] = xi + ci
        return carry

    lax.fori_loop(0, ts // SUBLANES, tile_step, 0)

    nt = (((1,), (1,)), ((), ()))
    h_re = h_s[SUBLANES:, :n].astype(BF16)
    h_im = h_s[SUBLANES:, n:].astype(BF16)
    y = (lax.dot_general(h_re, cre_ref[...], nt, preferred_element_type=F32)
         - lax.dot_general(h_im, cim_ref[...], nt, preferred_element_type=F32))
    y = y + dskip_ref[...] * u
    y = jax.nn.gelu(y)
    gate = jax.nn.sigmoid(jnp.dot(y.astype(BF16), wglu_ref[...], preferred_element_type=F32) + bglu_ref[...])
    y_ref[0] = (y * gate).astype(y_ref.dtype)

    last = h_s[ts:ts + SUBLANES, :]
    h_s[0:SUBLANES, :] = last

    @pl.when(t_idx == pl.num_programs(1) - 1)
    def _emit_state():
        hre_ref[0] = last[SUBLANES - 1:SUBLANES, :n]
        him_ref[0] = last[SUBLANES - 1:SUBLANES, n:]


def _s5(u, h0_re, h0_im, prm, ts):
    b, t, _ = u.shape
    n = N_STATE
    a_re, a_im, ldt, bre_d, bim_d, cre_t, cim_t, d_skip, w_glu, b_glu = prm
    const = lambda a: _resident(a.shape, lambda bi, i: (0,) * a.ndim)
    st_blk = pl.BlockSpec((1, 1, n), lambda bi, i: (bi, 0, 0))
    y, hre, him = pl.pallas_call(
        functools.partial(_s5_kernel, ts=ts, lane_chunk=512),
        grid=(b, t // ts),
        in_specs=[pl.BlockSpec((1, ts, D_SSM), lambda bi, i: (bi, i, 0)), st_blk, st_blk,
                  const(a_re), const(a_im), const(ldt), const(bre_d), const(bim_d), const(cre_t), const(cim_t),
                  const(d_skip), const(w_glu), const(b_glu)],
        out_specs=[pl.BlockSpec((1, ts, D_SSM), lambda bi, i: (bi, i, 0)), st_blk, st_blk],
        out_shape=[jax.ShapeDtypeStruct((b, t, D_SSM), BF16),
                   jax.ShapeDtypeStruct((b, 1, n), F32), jax.ShapeDtypeStruct((b, 1, n), F32)],
        scratch_shapes=[
            pltpu.VMEM((D_SSM, 2 * n), BF16),
            pltpu.VMEM((3, 2, SUBLANES, n), F32),
            pltpu.VMEM((2, SUBLANES, n), F32),
            pltpu.VMEM((SUBLANES + ts, 2 * n), F32),
        ],
        compiler_params=_params("arbitrary", "arbitrary"),
    )(u, h0_re, h0_im, a_re, a_im, ldt, bre_d, bim_d, cre_t, cim_t, d_skip, w_glu, b_glu)
    return y, hre, him


def _layer_norm(z, g, b):
    mu = jnp.mean(z, axis=-1, keepdims=True)
    zc = z - mu
    var = jnp.mean(zc * zc, axis=-1, keepdims=True)
    return zc * lax.rsqrt(var + LN_EPS) * g + b


def _merge_kernel(x_ref, att_ref, ssm_ref, wg_ref, bg_ref, wpa_ref, wpb_ref, wo_ref, g_ref, b_ref, o_ref, *, alpha):
    x = x_ref[...]
    d = x.shape[1]
    gates = jax.nn.sigmoid(jnp.dot(x.astype(BF16), wg_ref[...], preferred_element_type=F32) + bg_ref[...])
    mix = (gates[:, :d] * jnp.dot(att_ref[...], wpa_ref[...], preferred_element_type=F32)
           + gates[:, d:] * jnp.dot(ssm_ref[...], wpb_ref[...], preferred_element_type=F32))
    z = alpha * x + jnp.dot(mix.astype(BF16), wo_ref[...], preferred_element_type=F32)
    o_ref[...] = _layer_norm(z, g_ref[...], b_ref[...])


def _merge(x2d, att2d, ssm2d, prm, alpha, tm):
    rows, d = x2d.shape
    w_g, b_g, w_pa, w_pb, w_o, ln_g, ln_b = prm
    row_blk = lambda w: pl.BlockSpec((tm, w), lambda i: (i, 0))
    const = lambda a: _resident(a.shape, lambda i: (0, 0))
    return pl.pallas_call(
        functools.partial(_merge_kernel, alpha=alpha),
        grid=(rows // tm,),
        in_specs=[row_blk(d), row_blk(D_ATT), row_blk(D_SSM)] + [const(a) for a in prm],
        out_specs=row_blk(d),
        out_shape=jax.ShapeDtypeStruct((rows, d), F32),
        compiler_params=_params("parallel"),
    )(x2d, att2d, ssm2d, *prm)


def _ffn_kernel(x_ref, cbuf_ref, wup_ref, cw_ref, cb_ref, wdn_ref, g_ref, b_ref, o_ref, nconv_ref,
                h_s, acc_s, *, tm, d_ff, fc, alpha):
    t_idx = pl.program_id(1)
    hist = CONV_W - 1

    @pl.when(t_idx == 0)
    def _load_history():
        h_s[0:SUBLANES, :] = jnp.zeros((SUBLANES, h_s.shape[1]), F32)
        h_s[SUBLANES - hist:SUBLANES, :] = cbuf_ref[0]

    x = x_ref[0]
    x_bf = x.astype(BF16)
    for c in range(d_ff // fc):
        halves = []
        for off in (c * fc, d_ff + c * fc):
            cols = slice(off, off + fc)
            h_s[SUBLANES:, cols] = jnp.dot(x_bf, wup_ref[:, cols], preferred_element_type=F32)
            conv = cb_ref[:, cols] + sum(cw_ref[j:j + 1, cols] * h_s[SUBLANES - hist + j:SUBLANES - hist + j + tm, cols]
                                         for j in range(CONV_W))
            halves.append(conv)
        act = (jax.nn.gelu(halves[0]) * halves[1]).astype(BF16)
        part = jnp.dot(act, wdn_ref[c * fc:(c + 1) * fc, :], preferred_element_type=F32)
        if c == 0:
            acc_s[...] = part
        else:
            acc_s[...] += part
    o_ref[0] = _layer_norm(alpha * x + acc_s[...], g_ref[...], b_ref[...])

    tail = h_s[tm:tm + SUBLANES, :]
    h_s[0:SUBLANES, :] = tail

    @pl.when(t_idx == pl.num_programs(1) - 1)
    def _emit_history():
        nconv_ref[0] = tail[SUBLANES - hist:, :]


def _ffn(x, conv_buf, prm, alpha, tm):
    b, t, d = x.shape
    w_up, conv_w, conv_b, w_dn, ln_g, ln_b = prm
    d_ff = w_dn.shape[0]
    fc = 256
    assert d_ff % fc == 0 and t % tm == 0 and tm % SUBLANES == 0
    const = lambda a: _resident(a.shape, lambda bi, i: (0, 0))
    hist_blk = pl.BlockSpec((1, CONV_W - 1, 2 * d_ff), lambda bi, i: (bi, 0, 0))
    return pl.pallas_call(
        functools.partial(_ffn_kernel, tm=tm, d_ff=d_ff, fc=fc, alpha=alpha),
        grid=(b, t // tm),
        in_specs=[pl.BlockSpec((1, tm, d), lambda bi, i: (bi, i, 0)), hist_blk] + [const(a) for a in prm],
        out_specs=[pl.BlockSpec((1, tm, d), lambda bi, i: (bi, i, 0)), hist_blk],
        out_shape=[jax.ShapeDtypeStruct((b, t, d), F32), jax.ShapeDtypeStruct((b, CONV_W - 1, 2 * d_ff), F32)],
        scratch_shapes=[pltpu.VMEM((SUBLANES + tm, 2 * d_ff), F32),
                        pltpu.VMEM((tm, d), F32)],
        compiler_params=_params("arbitrary", "arbitrary"),
    )(x, conv_buf, *prm)


def _rope_tables(pos):
    half = HEAD_DIM // 2
    inv = ROPE_THETA ** (-jnp.arange(half, dtype=F32) / half)
    ang = pos.astype(F32)[:, None] * inv[None, :]
    cos, sin = jnp.cos(ang), jnp.sin(ang)
    cos_tab = jnp.tile(cos, (1, LANES // half))
    sin_tab = jnp.tile(jnp.concatenate([-sin, sin], axis=1), (1, LANES // HEAD_DIM))
    return cos_tab, sin_tab


def _layer_weights(w_in, b_gate, w_pa, w_pb, w_o, a_re, a_im, log_dt, b_re, b_im, c_re, c_im, d_skip, w_glu,
                   b_glu, ln1_g, ln1_b, w_up, conv_w, conv_b, w_down, ln2_g, ln2_b):
    d = w_in.shape[0]
    o = 0
    pieces = []
    for w in (D_ATT, D_ATT, D_ATT, N_IDX_HEADS * IDX_DIM, IDX_DIM + N_IDX_HEADS, D_SSM, 2 * d):
        pieces.append(w_in[:, o:o + w])
        o += w
    wq, wk, wv, wqi, wkw, wu, wg = pieces
    wkw = jnp.pad(wkw, ((0, 0), (0, LANES - wkw.shape[1])))
    inproj = tuple(w.astype(BF16) for w in (wq, wk, wv, wqi, wkw, wu))
    n = N_STATE
    row = lambda a: a.reshape(1, -1).astype(F32)
    same_group = jnp.eye(N_GROUPS, dtype=bool)[:, None, :, None]
    blockdiag_t = lambda c: jnp.where(same_group, c[:, :, None, :], 0.0).reshape(D_SSM, n).astype(BF16)
    dense = lambda bm: jnp.transpose(bm, (2, 0, 1)).reshape(SSM_GROUP, n).astype(F32)
    s5 = (row(a_re), row(a_im), row(jnp.repeat(log_dt, SSM_STATE)), dense(b_re), dense(b_im),
          blockdiag_t(c_re), blockdiag_t(c_im), row(d_skip), w_glu.astype(BF16), row(b_glu))
    merge = (wg.astype(BF16), row(b_gate), w_pa.astype(BF16), w_pb.astype(BF16), w_o.astype(BF16),
             row(ln1_g), row(ln1_b))
    ffn = (w_up.astype(BF16), conv_w.astype(F32), row(conv_b), w_down.astype(BF16), row(ln2_g), row(ln2_b))
    return inproj, s5, merge, ffn


def _largest_tile(n, cap):
    t = min(n, cap)
    while n % t:
        t -= SUBLANES
    return t


def _trunk_layer(x, past_k, past_v, past_ki, h0_re, h0_im, conv_buf, weights, alpha):
    bn, t, d = x.shape
    p = 0 if past_k is None else past_k.shape[1]
    n_keys = p + t
    k_keep = min(TOPK_MAX, n_keys // 4)
    inproj_w, s5_w, merge_w, ffn_w = weights
    rows = bn * t
    tm = _largest_tile(rows, 512)

    cos_tab, sin_tab = _rope_tables(p + jnp.arange(t))
    if t % tm == 0:
        n_tab_blocks = t // tm
    else:
        cos_tab, sin_tab = jnp.tile(cos_tab, (bn, 1)), jnp.tile(sin_tab, (bn, 1))
        n_tab_blocks = rows // tm
    q, k_f, k_b, v_f, v_b, qi, kw, u = _inproj(x.reshape(rows, d), cos_tab, sin_tab, inproj_w, tm, n_tab_blocks)

    r3 = lambda a: a.reshape(bn, t, a.shape[-1])
    ki_b = r3(kw)[:, :, :IDX_DIM].astype(BF16)
    k_all, v_all = r3(k_b), r3(v_b)
    if p:
        k_all = jnp.concatenate([past_k.reshape(bn, p, D_ATT).astype(BF16), k_all], axis=1)
        v_all = jnp.concatenate([past_v.reshape(bn, p, D_ATT).astype(BF16), v_all], axis=1)
        ki_b = jnp.concatenate([past_ki.astype(BF16), ki_b], axis=1)
    l_pad = -(-n_keys // LANES) * LANES
    l_pad = max(l_pad, 2 * LANES)
    pad = lambda a: jnp.pad(a, ((0, 0), (0, l_pad - n_keys), (0, 0)))
    kiki = jnp.concatenate([ki_b, ki_b], axis=-1)
    tq = _largest_tile(t, 128)
    att = _attention(r3(q), r3(qi), r3(kw), pad(kiki), pad(k_all), pad(v_all),
                     n_keys=n_keys, past=p, k_keep=k_keep, tq=tq)

    ts = _largest_tile(t, 256)
    ssm, h_re, h_im = _s5(r3(u), h0_re.reshape(bn, 1, N_STATE), h0_im.reshape(bn, 1, N_STATE), s5_w, ts)

    x1 = _merge(x.reshape(rows, d), att.reshape(rows, D_ATT), ssm.reshape(rows, D_SSM), merge_w, alpha, tm)
    x2, new_conv = _ffn(x1.reshape(bn, t, d), conv_buf, ffn_w, alpha, _largest_tile(t, 256))

    k_out = r3(k_f).reshape(bn, t, N_HEADS, HEAD_DIM)
    v_out = r3(v_f).reshape(bn, t, N_HEADS, HEAD_DIM)
    ki_out = r3(kw)[:, :, :IDX_DIM]
    return (x2, k_out, v_out, ki_out, h_re.reshape(bn, N_GROUPS, SSM_STATE), h_im.reshape(bn, N_GROUPS, SSM_STATE),
            new_conv)


def kernel(x_prompt, x_sample, cache_k, cache_v, cache_idx_k, state_ssm_re, state_ssm_im, state_conv, w_in, b_gate, w_pa, w_pb, w_o, a_re, a_im, log_dt, b_re, b_im, c_re, c_im, d_skip, w_glu, b_glu, ln1_g, ln1_b, w_up, conv_w, conv_b, w_down, ln2_g, ln2_b):
    depth = w_in.shape[0]
    alpha = (2 * depth) ** 0.25
    bp = x_prompt.shape[0]
    d_ff2 = w_up.shape[2]
    zero_h = jnp.zeros((bp, N_GROUPS, SSM_STATE), F32)
    zero_conv = jnp.zeros((bp, CONV_W - 1, d_ff2), F32)
    layer_params = (w_in, b_gate, w_pa, w_pb, w_o, a_re, a_im, log_dt, b_re, b_im, c_re, c_im, d_skip, w_glu, b_glu,
                    ln1_g, ln1_b, w_up, conv_w, conv_b, w_down, ln2_g, ln2_b)
    xp, xs = x_prompt, x_sample
    outs_p, outs_s = [], []
    for l in range(depth):
        weights = _layer_weights(*(a[l] for a in layer_params))
        rp = _trunk_layer(xp, None, None, None, zero_h, zero_h, zero_conv, weights, alpha)
        rs = _trunk_layer(xs, cache_k[l], cache_v[l], cache_idx_k[l], state_ssm_re[l].astype(F32),
                          state_ssm_im[l].astype(F32), state_conv[l], weights, alpha)
        xp, xs = rp[0], rs[0]
        outs_p.append(rp[1:])
        outs_s.append(rs[1:])
    stack = lambda outs: tuple(jnp.stack([o[j] for o in outs], axis=0) for j in range(6))
    return (xp, xs) + stack(outs_p) + stack(outs_s)
```

```python
import functools
import math

import jax
import jax.numpy as jnp
from jax import lax
from jax.experimental import pallas as pl
from jax.experimental.pallas import tpu as pltpu

CHUNK = 64
N_HEADS = 8
HEAD_DIM = 64
D_ATT = N_HEADS * HEAD_DIM
N_IDX_HEADS = 8
IDX_DIM = 64
TOPK_MAX = 256
D_SSM = 512
SSM_GROUP = 16
N_GROUPS = D_SSM // SSM_GROUP
SSM_STATE = 64
N_STATE = N_GROUPS * SSM_STATE
CONV_W = 3
ROPE_THETA = 10000.0
LN_EPS = 1e-5

LANES = 128
SUBLANES = 8
VMEM_LIMIT = 60 * 1024 * 1024

F32 = jnp.float32
BF16 = jnp.bfloat16
I32 = jnp.int32
NEG_INF = float("-inf")
INT_MIN = -(2 ** 31)
HIGH16 = -(2 ** 16)


def _f32_key_of_neg_inf():
    bits = 0xFF800000
    signed = bits - (1 << 32)
    return signed ^ ((signed >> 31) & 0x7FFFFFFF)


NEG_INF_KEY = _f32_key_of_neg_inf()
SOFTMAX_SUM_FLOOR = 1e-30


def _resident(block_shape, index_map):
    return pl.BlockSpec(block_shape, index_map, pipeline_mode=pl.Buffered(1))


def _params(*sem):
    return pltpu.CompilerParams(dimension_semantics=sem, vmem_limit_bytes=VMEM_LIMIT)


def _inproj_kernel(x_ref, cos_ref, sin_ref, wq_ref, wk_ref, wv_ref, wqi_ref, wkw_ref, wu_ref,
                   q_ref, kf_ref, kb_ref, vf_ref, vb_ref, qi_ref, kw_ref, u_ref, *, wi_scale):
    x = x_ref[...].astype(BF16)
    cos = cos_ref[...]
    sin = sin_ref[...]
    tm = x.shape[0]
    lane = lax.broadcasted_iota(I32, (tm, LANES), 1)
    first_half = (lane % HEAD_DIM) < (HEAD_DIM // 2)

    def proj(w_ref):
        return jnp.dot(x, w_ref[...], preferred_element_type=F32)

    def rope_group(y):
        rot = jnp.where(first_half, pltpu.roll(y, LANES - HEAD_DIM // 2, 1), pltpu.roll(y, HEAD_DIM // 2, 1))
        return y * cos + rot * sin

    def rope(y):
        return jnp.concatenate([rope_group(y[:, c * LANES:(c + 1) * LANES]) for c in range(y.shape[1] // LANES)],
                               axis=1)

    q = rope(proj(wq_ref))
    q_ref[...] = (q * (HEAD_DIM ** -0.5)).astype(BF16)
    k = rope(proj(wk_ref))
    kf_ref[...] = k
    kb_ref[...] = k.astype(BF16)
    v = proj(wv_ref)
    vf_ref[...] = v
    vb_ref[...] = v.astype(BF16)
    qi_ref[...] = rope(proj(wqi_ref)).astype(BF16)
    kw = proj(wkw_ref)
    kw_ref[...] = jnp.where(lane < IDX_DIM, rope_group(kw), kw * wi_scale)
    u_ref[...] = proj(wu_ref)


def _inproj(x2d, cos_tab, sin_tab, ws, tm, n_tab_blocks):
    rows, d = x2d.shape
    wq, wk, wv, wqi, wkw, wu = ws
    row_blk = lambda w: pl.BlockSpec((tm, w), lambda i: (i, 0))
    tab_blk = pl.BlockSpec((tm, LANES), lambda i: (i % n_tab_blocks, 0))
    w_blk = lambda w: _resident(w.shape, lambda i: (0, 0))
    out_shapes = [
        jax.ShapeDtypeStruct((rows, D_ATT), BF16),
        jax.ShapeDtypeStruct((rows, D_ATT), F32),
        jax.ShapeDtypeStruct((rows, D_ATT), BF16),
        jax.ShapeDtypeStruct((rows, D_ATT), F32),
        jax.ShapeDtypeStruct((rows, D_ATT), BF16),
        jax.ShapeDtypeStruct((rows, D_ATT), BF16),
        jax.ShapeDtypeStruct((rows, LANES), F32),
        jax.ShapeDtypeStruct((rows, D_SSM), F32),
    ]
    wi_scale = (N_IDX_HEADS ** -0.5) * (IDX_DIM ** -0.5)
    return pl.pallas_call(
        functools.partial(_inproj_kernel, wi_scale=wi_scale),
        grid=(rows // tm,),
        in_specs=[row_blk(d), tab_blk, tab_blk, w_blk(wq), w_blk(wk), w_blk(wv), w_blk(wqi), w_blk(wkw), w_blk(wu)],
        out_specs=[row_blk(s.shape[1]) for s in out_shapes],
        out_shape=out_shapes,
        compiler_params=_params("parallel"),
    )(x2d, cos_tab, sin_tab, wq, wk, wv, wqi, wkw, wu)


def _attn_kernel(q_ref, qi_ref, kw_ref, kmax_ref, kiki_ref, k_ref, v_ref, o_ref,
                 key_s, keyhi_s, qm_s, qim_s, wi_s, mb_s, bias_s, l_s, acc_s, m2_s, l2_s,
                 *, tq, tk, n_keys, past, k_keep, idx_bits):
    q0 = pl.program_id(1) * tq
    n_pairs = N_HEADS // 2
    n_chunks = tk // LANES
    row = lax.broadcasted_iota(I32, (tq, 1), 0)
    q_lim = jnp.minimum(((past + q0 + row) // CHUNK + 1) * CHUNK, n_keys)
    q_lim_b = jnp.broadcast_to(q_lim, (tq, LANES))
    blk_lim = jnp.minimum(((past + q0 + tq - 1) // CHUNK + 1) * CHUNK, n_keys)
    nk = (blk_lim + tk - 1) // tk
    lane_q = lax.broadcasted_iota(I32, (tq, LANES), 1)
    lo_half = lane_q < HEAD_DIM
    nt = (((1,), (1,)), ((), ()))
    pair_of = lambda h: slice((h // 2) * LANES, (h // 2 + 1) * LANES)
    chunk_of = lambda c: slice(c * LANES, (c + 1) * LANES)

    kw = kw_ref[0]
    kmax = kmax_ref[0]
    halves = (slice(0, tq), slice(tq, 2 * tq))
    for pr in range(n_pairs):
        qi_pair = qi_ref[0, :, pair_of(2 * pr)]
        q_pair = q_ref[0, :, pair_of(2 * pr)]
        for half, keep in enumerate((lo_half, jnp.logical_not(lo_half))):
            h = 2 * pr + half
            qim_s[pr, halves[half]] = jnp.where(keep, qi_pair, jnp.zeros_like(qi_pair))
            qm = jnp.where(keep, q_pair, jnp.zeros_like(q_pair))
            qm_s[pr, halves[half]] = qm
            wi_s[h] = jnp.broadcast_to(kw[:, IDX_DIM + h:IDX_DIM + h + 1], (tq, LANES))
            qf = qm.astype(F32)
            q_norm = jnp.sqrt(jnp.sum(qf * qf, axis=1, keepdims=True))
            mb_s[h] = jnp.broadcast_to(q_norm * kmax[:, h:h + 1], (tq, LANES))

    def key_base(kt, c):
        return pl.multiple_of(kt * tk + c * LANES, LANES)

    def score_tile(kt, carry):
        k0 = pl.multiple_of(kt * tk, tk)
        kiki = kiki_ref[0, pl.ds(k0, tk), :]
        score = [jnp.zeros((tq, LANES), F32) for _ in range(n_chunks)]
        for pr in range(n_pairs):
            s = lax.dot_general(qim_s[pr], kiki, nt, preferred_element_type=F32)
            for half in range(2):
                w = wi_s[2 * pr + half]
                for c in range(n_chunks):
                    score[c] = score[c] + w * jnp.maximum(s[halves[half], chunk_of(c)], 0.0)
        for c in range(n_chunks):
            kb = key_base(kt, c)
            bits = pltpu.bitcast(jnp.where(kb + lane_q < q_lim_b, score[c], NEG_INF), I32)
            key_s[:, pl.ds(kb, LANES)] = bits ^ ((bits >> 31) & 0x7FFFFFFF)
            keyhi_s[:, pl.ds(kb, LANES)] = pltpu.bitcast(bits & HIGH16, F32).astype(BF16)
        return carry

    lax.fori_loop(0, nk, score_tile, 0)

    def count(pred):
        def body(kt, cnt):
            for c in range(n_chunks):
                kb = key_base(kt, c)
                cnt = cnt + pred(key_s[:, pl.ds(kb, LANES)], kb).astype(I32)
            return cnt
        cnt = lax.fori_loop(0, nk, body, jnp.zeros((tq, LANES), I32))
        return jnp.sum(cnt, axis=1, keepdims=True)

    def count_ge(c):
        cb = jnp.broadcast_to(c, (tq, LANES))
        return count(lambda key, kb: key >= cb)

    def count_ge_high(c):
        c_bits = (c ^ ((c >> 31) & 0x7FFFFFFF)) & HIGH16
        cb = jnp.broadcast_to(pltpu.bitcast(c_bits, F32).astype(BF16), (tq, LANES))
        one, nil = jnp.ones((tq, LANES), BF16), jnp.zeros((tq, LANES), BF16)

        def body(kt, cnt):
            for c_i in range(n_chunks):
                cnt = cnt + jnp.where(keyhi_s[:, pl.ds(key_base(kt, c_i), LANES)] >= cb, one, nil)
            return cnt
        cnt = lax.fori_loop(0, nk, body, nil)
        return jnp.sum(cnt.astype(F32), axis=1, keepdims=True).astype(I32)

    def search(counter, first_bit, last_bit, state):
        def cond(st):
            bit, _, _, done = st
            return jnp.logical_and(bit >= last_bit, jnp.min(done) == 0)

        def body(st):
            bit, t, thr, done = st
            cand = t + jnp.left_shift(jnp.int32(1), bit)
            cnt = counter(cand)
            t = jnp.where(cnt >= k_keep, cand, t)
            newly = jnp.logical_and(cnt == k_keep, done == 0)
            thr = jnp.where(newly, jnp.maximum(cand - 1, NEG_INF_KEY), thr)
            done = jnp.where(newly, 1, done)
            return bit - 1, t, thr, done

        return lax.while_loop(cond, body, (jnp.int32(first_bit),) + state)[1:]

    zero = jnp.zeros((tq, 1), I32)
    state = search(count_ge_high, 31, 16, (jnp.full((tq, 1), INT_MIN, I32), zero, zero))
    t_fin, thr_early, done = search(count_ge, 15, 0, state)
    t_fin = jnp.maximum(t_fin, NEG_INF_KEY)

    def resolve_ties(_):
        cnt_gt = count_ge(t_fin + 1)
        n_eq = count_ge(t_fin) - cnt_gt
        want = k_keep - cnt_gt
        tb = jnp.broadcast_to(t_fin, (tq, LANES))
        need = jnp.logical_and(jnp.logical_and(done == 0, n_eq > want), t_fin > NEG_INF_KEY)

        def cut_search(_):
            def body(j, c):
                cand = c + jnp.left_shift(jnp.int32(1), idx_bits - 1 - j)
                cb = jnp.broadcast_to(cand, (tq, LANES))
                below = count(lambda key, kb: jnp.logical_and(key == tb, kb + lane_q < cb))
                return jnp.where(below < want, cand, c)
            return lax.fori_loop(0, idx_bits, body, zero)

        cut = lax.cond(jnp.max(need.astype(I32)) > 0, cut_search, lambda _: zero, 0)
        all_ties = jnp.where(t_fin > NEG_INF_KEY, jnp.int32(2 ** 30), jnp.int32(-1))
        return jnp.where(need, cut, all_ties)

    cut_full = lax.cond(jnp.min(done) == 0, resolve_ties, lambda _: jnp.full((tq, 1), -1, I32), 0)
    thr = jnp.where(done == 1, thr_early, t_fin)
    cut = jnp.where(done == 1, -1, cut_full)
    thr_b = jnp.broadcast_to(thr, (tq, LANES))
    cut_b = jnp.broadcast_to(cut, (tq, LANES))

    def tile_bias(kt):
        for c in range(n_chunks):
            kb = key_base(kt, c)
            key = key_s[:, pl.ds(kb, LANES)]
            sel = jnp.logical_or(key > thr_b, jnp.logical_and(key == thr_b, kb + lane_q <= cut_b))
            bias_s[:, chunk_of(c)] = jnp.where(sel, 0.0, NEG_INF)

    def emit(l_of_head):
        for pr in range(n_pairs):
            even = acc_s[pr, halves[0]] / l_of_head(2 * pr)
            odd = acc_s[pr, halves[1]] / l_of_head(2 * pr + 1)
            o_ref[0, :, pair_of(2 * pr)] = jnp.where(lo_half, even, odd).astype(o_ref.dtype)

    l_s[...] = jnp.zeros(l_s.shape, F32)
    acc_s[...] = jnp.zeros(acc_s.shape, F32)

    def bound_tile(kt, carry):
        k0 = pl.multiple_of(kt * tk, tk)
        tile_bias(kt)
        for pr in range(n_pairs):
            logits = lax.dot_general(qm_s[pr], k_ref[0, pl.ds(k0, tk), pair_of(2 * pr)], nt,
                                     preferred_element_type=F32)
            p = []
            for half in range(2):
                h = 2 * pr + half
                shift = mb_s[h]
                l_part = l_s[h]
                p_row = []
                for c in range(n_chunks):
                    e = jnp.exp(logits[halves[half], chunk_of(c)] + bias_s[:, chunk_of(c)] - shift)
                    l_part = l_part + e
                    p_row.append(e.astype(BF16))
                l_s[h] = l_part
                p.append(jnp.concatenate(p_row, axis=1))
            acc_s[pr] += jnp.dot(jnp.concatenate(p, axis=0), v_ref[0, pl.ds(k0, tk), pair_of(2 * pr)],
                                 preferred_element_type=F32)
        return carry

    lax.fori_loop(0, nk, bound_tile, 0)
    l_rows = [jnp.sum(l_s[h], axis=1, keepdims=True) for h in range(N_HEADS)]
    l_min = functools.reduce(jnp.minimum, [jnp.min(l) for l in l_rows])
    well_scaled = l_min >= SOFTMAX_SUM_FLOOR

    @pl.when(well_scaled)
    def _emit_bound():
        emit(lambda h: l_rows[h])

    @pl.when(jnp.logical_not(well_scaled))
    def _running_max():
        m2_s[...] = jnp.full(m2_s.shape, NEG_INF, F32)
        l2_s[...] = jnp.zeros(l2_s.shape, F32)
        acc_s[...] = jnp.zeros(acc_s.shape, F32)

        def online_tile(kt, carry):
            k0 = pl.multiple_of(kt * tk, tk)
            tile_bias(kt)
            for h in range(N_HEADS):
                rows = halves[h % 2]
                logits = lax.dot_general(qm_s[h // 2, rows], k_ref[0, pl.ds(k0, tk), pair_of(h)], nt,
                                         preferred_element_type=F32) + bias_s[...]
                m_old = m2_s[h]
                m_new = jnp.maximum(m_old, jnp.max(logits, axis=1, keepdims=True))
                m_safe = jnp.where(m_new == NEG_INF, 0.0, m_new)
                alpha = jnp.exp(m_old - m_safe)
                p = jnp.exp(logits - m_safe)
                l2_s[h] = alpha * l2_s[h] + jnp.sum(p, axis=1, keepdims=True)
                acc_s[h // 2, rows] = alpha * acc_s[h // 2, rows] + jnp.dot(
                    p.astype(BF16), v_ref[0, pl.ds(k0, tk), pair_of(h)], preferred_element_type=F32)
                m2_s[h] = m_new
            return carry

        lax.fori_loop(0, nk, online_tile, 0)
        emit(lambda h: l2_s[h])


def _pick_key_tile(l_pad, k_keep):
    for tk in (256, 384, 512, 640, 768, 896, 1024):
        if l_pad % tk == 0 and tk >= k_keep:
            return tk
    return l_pad


def _attention(q, qi, kw, kmax, kiki, k_all, v_all, *, n_keys, past, k_keep, tq):
    b, t, _ = q.shape
    l_pad = k_all.shape[1]
    tk = _pick_key_tile(l_pad, k_keep)
    assert l_pad % tk == 0 and tk >= k_keep and t % tq == 0
    assert l_pad // LANES <= 256
    idx_bits = max(1, int(math.ceil(math.log2(l_pad))))
    qblk = lambda w: pl.BlockSpec((1, tq, w), lambda bi, i: (bi, i, 0))
    kvblk = lambda w: _resident((1, l_pad, w), lambda bi, i: (bi, 0, 0))
    per_head = lambda w, dt: pltpu.VMEM((N_HEADS, tq, w), dt)
    per_pair = lambda dt: pltpu.VMEM((N_HEADS // 2, 2 * tq, LANES), dt)
    return pl.pallas_call(
        functools.partial(_attn_kernel, tq=tq, tk=tk, n_keys=n_keys, past=past, k_keep=k_keep, idx_bits=idx_bits),
        grid=(b, t // tq),
        in_specs=[qblk(D_ATT), qblk(D_ATT), qblk(LANES), pl.BlockSpec((1, 1, LANES), lambda bi, i: (bi, 0, 0)),
                  kvblk(LANES), kvblk(D_ATT), kvblk(D_ATT)],
        out_specs=qblk(D_ATT),
        out_shape=jax.ShapeDtypeStruct((b, t, D_ATT), BF16),
        scratch_shapes=[
            pltpu.VMEM((tq, l_pad), I32),
            pltpu.VMEM((tq, l_pad), BF16),
            per_pair(BF16),
            per_pair(BF16),
            per_head(LANES, F32),
            per_head(LANES, F32),
            pltpu.VMEM((tq, tk), F32),
            per_head(LANES, F32),
            per_pair(F32),
            per_head(1, F32),
            per_head(1, F32),
        ],
        compiler_params=_params("arbitrary", "arbitrary"),
    )(q, qi, kw, kmax, kiki, k_all, v_all)


def _cmul(ar, ai, br, bi):
    return ar * br - ai * bi, ar * bi + ai * br


def _s5_kernel(u_ref, h0re_ref, h0im_ref, are_ref, aim_ref, ldt_ref, bre_ref, bim_ref, cre_ref, cim_ref,
               dskip_ref, wglu_ref, bglu_ref, y_ref, hre_ref, him_ref,
               bbar_s, lvl_s, pw_s, h_s, *, ts, lane_chunk):
    n = N_STATE
    t_idx = pl.program_id(1)
    first = jnp.logical_and(pl.program_id(0) == 0, t_idx == 0)

    @pl.when(first)
    def _discretise():
        dt = jnp.exp(ldt_ref[...])
        ar, ai = are_ref[...], aim_ref[...]
        mag = jnp.exp(dt * ar)
        abr, abi = mag * jnp.cos(dt * ai), mag * jnp.sin(dt * ai)
        den = ar * ar + ai * ai
        nr, ni = abr - 1.0, abi
        f_re, f_im = (nr * ar + ni * ai) / den, (ni * ar - nr * ai) / den
        reps = D_SSM // SSM_GROUP
        bre = jnp.concatenate([bre_ref[...]] * reps, axis=0)
        bim = jnp.concatenate([bim_ref[...]] * reps, axis=0)
        r_grp = lax.broadcasted_iota(I32, (D_SSM, n), 0) // SSM_GROUP
        c_grp = lax.broadcasted_iota(I32, (D_SSM, n), 1) // SSM_STATE
        same = r_grp == c_grp
        bbar_s[:, :n] = jnp.where(same, f_re * bre - f_im * bim, 0.0).astype(BF16)
        bbar_s[:, n:] = jnp.where(same, f_re * bim + f_im * bre, 0.0).astype(BF16)
        a1 = (abr, abi)
        a2 = _cmul(*a1, *a1)
        a3 = _cmul(*a2, *a1)
        a4 = _cmul(*a2, *a2)
        a5 = _cmul(*a4, *a1)
        a6 = _cmul(*a4, *a2)
        a7 = _cmul(*a4, *a3)
        a8 = _cmul(*a4, *a4)
        rows = lax.broadcasted_iota(I32, (SUBLANES, n), 0)
        for part in range(2):
            pw = jnp.zeros((SUBLANES, n), F32)
            for r, a in enumerate((a1, a2, a3, a4, a5, a6, a7, a8)):
                pw = jnp.where(rows == r, a[part], pw)
            pw_s[part] = pw
            for lv, (a, dist) in enumerate(((a1, 1), (a2, 2), (a4, 4))):
                lvl_s[lv, part] = jnp.where(rows >= dist, a[part], 0.0)

    @pl.when(t_idx == 0)
    def _load_state():
        h_s[0:SUBLANES, :] = jnp.zeros((SUBLANES, 2 * n), F32)
        h_s[SUBLANES - 1:SUBLANES, :n] = h0re_ref[0]
        h_s[SUBLANES - 1:SUBLANES, n:] = h0im_ref[0]

    u = u_ref[0]
    u_bf = u.astype(BF16)
    for c in range(2 * n // 512):
        h_s[SUBLANES:, c * 512:(c + 1) * 512] = jnp.dot(u_bf, bbar_s[:, c * 512:(c + 1) * 512],
                                                        preferred_element_type=F32)

    def tile_step(j, carry):
        r0 = pl.multiple_of(j * SUBLANES, SUBLANES)
        for c in range(n // lane_chunk):
            re_sl = slice(c * lane_chunk, (c + 1) * lane_chunk)
            im_sl = slice(n + c * lane_chunk, n + (c + 1) * lane_chunk)
            xr = h_s[pl.ds(r0 + SUBLANES, SUBLANES), re_sl]
            xi = h_s[pl.ds(r0 + SUBLANES, SUBLANES), im_sl]
            for lv, dist in enumerate((1, 2, 4)):
                sr, si = pltpu.roll(xr, dist, 0), pltpu.roll(xi, dist, 0)
                dr, di = _cmul(lvl_s[lv, 0, :, re_sl], lvl_s[lv, 1, :, re_sl], sr, si)
                xr, xi = xr + dr, xi + di
            prev_r = h_s[pl.ds(r0, SUBLANES), re_sl][SUBLANES - 1:SUBLANES, :]
            prev_i = h_s[pl.ds(r0, SUBLANES), im_sl][SUBLANES - 1:SUBLANES, :]
            cr, ci = _cmul(pw_s[0, :, re_sl], pw_s[1, :, re_sl], prev_r, prev_i)
            h_s[pl.ds(r0 + SUBLANES, SUBLANES), re_sl] = xr + cr
            h_s[pl.ds(r0 + SUBLANES, SUBLANES), im_sl] = xi + ci
        return carry

    lax.fori_loop(0, ts // SUBLANES, tile_step, 0)

    nt = (((1,), (1,)), ((), ()))
    h_re = h_s[SUBLANES:, :n].astype(BF16)
    h_im = h_s[SUBLANES:, n:].astype(BF16)
    y = (lax.dot_general(h_re, cre_ref[...], nt, preferred_element_type=F32)
         - lax.dot_general(h_im, cim_ref[...], nt, preferred_element_type=F32))
    y = y + dskip_ref[...] * u
    y = jax.nn.gelu(y)
    gate = jax.nn.sigmoid(jnp.dot(y.astype(BF16), wglu_ref[...], preferred_element_type=F32) + bglu_ref[...])
    y_ref[0] = (y * gate).astype(y_ref.dtype)

    last = h_s[ts:ts + SUBLANES, :]
    h_s[0:SUBLANES, :] = last

    @pl.when(t_idx == pl.num_programs(1) - 1)
    def _emit_state():
        hre_ref[0] = last[SUBLANES - 1:SUBLANES, :n]
        him_ref[0] = last[SUBLANES - 1:SUBLANES, n:]


def _s5(u, h0_re, h0_im, prm, ts):
    b, t, _ = u.shape
    n = N_STATE
    a_re, a_im, ldt, bre_d, bim_d, cre_t, cim_t, d_skip, w_glu, b_glu = prm
    const = lambda a: _resident(a.shape, lambda bi, i: (0,) * a.ndim)
    st_blk = pl.BlockSpec((1, 1, n), lambda bi, i: (bi, 0, 0))
    y, hre, him = pl.pallas_call(
        functools.partial(_s5_kernel, ts=ts, lane_chunk=512),
        grid=(b, t // ts),
        in_specs=[pl.BlockSpec((1, ts, D_SSM), lambda bi, i: (bi, i, 0)), st_blk, st_blk,
                  const(a_re), const(a_im), const(ldt), const(bre_d), const(bim_d), const(cre_t), const(cim_t),
                  const(d_skip), const(w_glu), const(b_glu)],
        out_specs=[pl.BlockSpec((1, ts, D_SSM), lambda bi, i: (bi, i, 0)), st_blk, st_blk],
        out_shape=[jax.ShapeDtypeStruct((b, t, D_SSM), BF16),
                   jax.ShapeDtypeStruct((b, 1, n), F32), jax.ShapeDtypeStruct((b, 1, n), F32)],
        scratch_shapes=[
            pltpu.VMEM((D_SSM, 2 * n), BF16),
            pltpu.VMEM((3, 2, SUBLANES, n), F32),
            pltpu.VMEM((2, SUBLANES, n), F32),
            pltpu.VMEM((SUBLANES + ts, 2 * n), F32),
        ],
        compiler_params=_params("arbitrary", "arbitrary"),
    )(u, h0_re, h0_im, a_re, a_im, ldt, bre_d, bim_d, cre_t, cim_t, d_skip, w_glu, b_glu)
    return y, hre, him


def _layer_norm(z, g, b):
    mu = jnp.mean(z, axis=-1, keepdims=True)
    zc = z - mu
    var = jnp.mean(zc * zc, axis=-1, keepdims=True)
    return zc * lax.rsqrt(var + LN_EPS) * g + b


def _merge_kernel(x_ref, att_ref, ssm_ref, wg_ref, bg_ref, wpa_ref, wpb_ref, wo_ref, g_ref, b_ref, o_ref, *, alpha):
    x = x_ref[...]
    d = x.shape[1]
    gates = jax.nn.sigmoid(jnp.dot(x.astype(BF16), wg_ref[...], preferred_element_type=F32) + bg_ref[...])
    mix = (gates[:, :d] * jnp.dot(att_ref[...], wpa_ref[...], preferred_element_type=F32)
           + gates[:, d:] * jnp.dot(ssm_ref[...], wpb_ref[...], preferred_element_type=F32))
    z = alpha * x + jnp.dot(mix.astype(BF16), wo_ref[...], preferred_element_type=F32)
    o_ref[...] = _layer_norm(z, g_ref[...], b_ref[...])


def _merge(x2d, att2d, ssm2d, prm, alpha, tm):
    rows, d = x2d.shape
    w_g, b_g, w_pa, w_pb, w_o, ln_g, ln_b = prm
    row_blk = lambda w: pl.BlockSpec((tm, w), lambda i: (i, 0))
    const = lambda a: _resident(a.shape, lambda i: (0, 0))
    return pl.pallas_call(
        functools.partial(_merge_kernel, alpha=alpha),
        grid=(rows // tm,),
        in_specs=[row_blk(d), row_blk(D_ATT), row_blk(D_SSM)] + [const(a) for a in prm],
        out_specs=row_blk(d),
        out_shape=jax.ShapeDtypeStruct((rows, d), F32),
        compiler_params=_params("parallel"),
    )(x2d, att2d, ssm2d, *prm)


def _ffn_kernel(x_ref, cbuf_ref, wup_ref, cw_ref, cb_ref, wdn_ref, g_ref, b_ref, o_ref, nconv_ref,
                h_s, acc_s, *, tm, d_ff, fc, alpha):
    t_idx = pl.program_id(1)
    hist = CONV_W - 1

    @pl.when(t_idx == 0)
    def _load_history():
        h_s[0:SUBLANES, :] = jnp.zeros((SUBLANES, h_s.shape[1]), F32)
        h_s[SUBLANES - hist:SUBLANES, :] = cbuf_ref[0]

    x = x_ref[0]
    x_bf = x.astype(BF16)
    for c in range(d_ff // fc):
        halves = []
        for off in (c * fc, d_ff + c * fc):
            cols = slice(off, off + fc)
            h_s[SUBLANES:, cols] = jnp.dot(x_bf, wup_ref[:, cols], preferred_element_type=F32)
            conv = cb_ref[:, cols] + sum(cw_ref[j:j + 1, cols] * h_s[SUBLANES - hist + j:SUBLANES - hist + j + tm, cols]
                                         for j in range(CONV_W))
            halves.append(conv)
        act = (jax.nn.gelu(halves[0]) * halves[1]).astype(BF16)
        part = jnp.dot(act, wdn_ref[c * fc:(c + 1) * fc, :], preferred_element_type=F32)
        if c == 0:
            acc_s[...] = part
        else:
            acc_s[...] += part
    o_ref[0] = _layer_norm(alpha * x + acc_s[...], g_ref[...], b_ref[...])

    tail = h_s[tm:tm + SUBLANES, :]
    h_s[0:SUBLANES, :] = tail

    @pl.when(t_idx == pl.num_programs(1) - 1)
    def _emit_history():
        nconv_ref[0] = tail[SUBLANES - hist:, :]


def _ffn(x, conv_buf, prm, alpha, tm):
    b, t, d = x.shape
    w_up, conv_w, conv_b, w_dn, ln_g, ln_b = prm
    d_ff = w_dn.shape[0]
    fc = 256
    assert d_ff % fc == 0 and t % tm == 0 and tm % SUBLANES == 0
    const = lambda a: _resident(a.shape, lambda bi, i: (0, 0))
    hist_blk = pl.BlockSpec((1, CONV_W - 1, 2 * d_ff), lambda bi, i: (bi, 0, 0))
    return pl.pallas_call(
        functools.partial(_ffn_kernel, tm=tm, d_ff=d_ff, fc=fc, alpha=alpha),
        grid=(b, t // tm),
        in_specs=[pl.BlockSpec((1, tm, d), lambda bi, i: (bi, i, 0)), hist_blk] + [const(a) for a in prm],
        out_specs=[pl.BlockSpec((1, tm, d), lambda bi, i: (bi, i, 0)), hist_blk],
        out_shape=[jax.ShapeDtypeStruct((b, t, d), F32), jax.ShapeDtypeStruct((b, CONV_W - 1, 2 * d_ff), F32)],
        scratch_shapes=[pltpu.VMEM((SUBLANES + tm, 2 * d_ff), F32),
                        pltpu.VMEM((tm, d), F32)],
        compiler_params=_params("arbitrary", "arbitrary"),
    )(x, conv_buf, *prm)


def _rope_tables(pos):
    half = HEAD_DIM // 2
    inv = ROPE_THETA ** (-jnp.arange(half, dtype=F32) / half)
    ang = pos.astype(F32)[:, None] * inv[None, :]
    cos, sin = jnp.cos(ang), jnp.sin(ang)
    cos_tab = jnp.tile(cos, (1, LANES // half))
    sin_tab = jnp.tile(jnp.concatenate([-sin, sin], axis=1), (1, LANES // HEAD_DIM))
    return cos_tab, sin_tab


def _layer_weights(w_in, b_gate, w_pa, w_pb, w_o, a_re, a_im, log_dt, b_re, b_im, c_re, c_im, d_skip, w_glu,
                   b_glu, ln1_g, ln1_b, w_up, conv_w, conv_b, w_down, ln2_g, ln2_b):
    d = w_in.shape[0]
    o = 0
    pieces = []
    for w in (D_ATT, D_ATT, D_ATT, N_IDX_HEADS * IDX_DIM, IDX_DIM + N_IDX_HEADS, D_SSM, 2 * d):
        pieces.append(w_in[:, o:o + w])
        o += w
    wq, wk, wv, wqi, wkw, wu, wg = pieces
    wkw = jnp.pad(wkw, ((0, 0), (0, LANES - wkw.shape[1])))
    inproj = tuple(w.astype(BF16) for w in (wq, wk, wv, wqi, wkw, wu))
    n = N_STATE
    row = lambda a: a.reshape(1, -1).astype(F32)
    same_group = jnp.eye(N_GROUPS, dtype=bool)[:, None, :, None]
    blockdiag_t = lambda c: jnp.where(same_group, c[:, :, None, :], 0.0).reshape(D_SSM, n).astype(BF16)
    dense = lambda bm: jnp.transpose(bm, (2, 0, 1)).reshape(SSM_GROUP, n).astype(F32)
    s5 = (row(a_re), row(a_im), row(jnp.repeat(log_dt, SSM_STATE)), dense(b_re), dense(b_im),
          blockdiag_t(c_re), blockdiag_t(c_im), row(d_skip), w_glu.astype(BF16), row(b_glu))
    merge = (wg.astype(BF16), row(b_gate), w_pa.astype(BF16), w_pb.astype(BF16), w_o.astype(BF16),
             row(ln1_g), row(ln1_b))
    ffn = (w_up.astype(BF16), conv_w.astype(F32), row(conv_b), w_down.astype(BF16), row(ln2_g), row(ln2_b))
    return inproj, s5, merge, ffn


def _largest_tile(n, cap):
    t = min(n, cap)
    while n % t:
        t -= SUBLANES
    return t


def _trunk_layer(x, past_k, past_v, past_ki, h0_re, h0_im, conv_buf, weights, alpha):
    bn, t, d = x.shape
    p = 0 if past_k is None else past_k.shape[1]
    n_keys = p + t
    k_keep = min(TOPK_MAX, n_keys // 4)
    inproj_w, s5_w, merge_w, ffn_w = weights
    rows = bn * t
    tm = _largest_tile(rows, 512)

    cos_tab, sin_tab = _rope_tables(p + jnp.arange(t))
    if t % tm == 0:
        n_tab_blocks = t // tm
    else:
        cos_tab, sin_tab = jnp.tile(cos_tab, (bn, 1)), jnp.tile(sin_tab, (bn, 1))
        n_tab_blocks = rows // tm
    q, k_f, k_b, v_f, v_b, qi, kw, u = _inproj(x.reshape(rows, d), cos_tab, sin_tab, inproj_w, tm, n_tab_blocks)

    r3 = lambda a: a.reshape(bn, t, a.shape[-1])
    ki_b = r3(kw)[:, :, :IDX_DIM].astype(BF16)
    k_all, v_all = r3(k_b), r3(v_b)
    if p:
        k_all = jnp.concatenate([past_k.reshape(bn, p, D_ATT).astype(BF16), k_all], axis=1)
        v_all = jnp.concatenate([past_v.reshape(bn, p, D_ATT).astype(BF16), v_all], axis=1)
        ki_b = jnp.concatenate([past_ki.astype(BF16), ki_b], axis=1)
    l_pad = -(-n_keys // LANES) * LANES
    l_pad = max(l_pad, 2 * LANES)
    pad = lambda a: jnp.pad(a, ((0, 0), (0, l_pad - n_keys), (0, 0)))
    kiki = jnp.concatenate([ki_b, ki_b], axis=-1)
    k_heads = k_all.astype(F32).reshape(bn, n_keys, N_HEADS, HEAD_DIM)
    kmax = jnp.sqrt(jnp.max(jnp.sum(k_heads * k_heads, axis=-1), axis=1))
    kmax = jnp.pad(kmax, ((0, 0), (0, LANES - N_HEADS))).reshape(bn, 1, LANES)
    tq = _largest_tile(t, 128)
    att = _attention(r3(q), r3(qi), r3(kw), kmax, pad(kiki), pad(k_all), pad(v_all),
                     n_keys=n_keys, past=p, k_keep=k_keep, tq=tq)

    ts = _largest_tile(t, 256)
    ssm, h_re, h_im = _s5(r3(u), h0_re.reshape(bn, 1, N_STATE), h0_im.reshape(bn, 1, N_STATE), s5_w, ts)

    x1 = _merge(x.reshape(rows, d), att.reshape(rows, D_ATT), ssm.reshape(rows, D_SSM), merge_w, alpha, tm)
    x2, new_conv = _ffn(x1.reshape(bn, t, d), conv_buf, ffn_w, alpha, _largest_tile(t, 256))

    k_out = r3(k_f).reshape(bn, t, N_HEADS, HEAD_DIM)
    v_out = r3(v_f).reshape(bn, t, N_HEADS, HEAD_DIM)
    ki_out = r3(kw)[:, :, :IDX_DIM]
    return (x2, k_out, v_out, ki_out, h_re.reshape(bn, N_GROUPS, SSM_STATE), h_im.reshape(bn, N_GROUPS, SSM_STATE),
            new_conv)


def kernel(x_prompt, x_sample, cache_k, cache_v, cache_idx_k, state_ssm_re, state_ssm_im, state_conv, w_in, b_gate, w_pa, w_pb, w_o, a_re, a_im, log_dt, b_re, b_im, c_re, c_im, d_skip, w_glu, b_glu, ln1_g, ln1_b, w_up, conv_w, conv_b, w_down, ln2_g, ln2_b):
    depth = w_in.shape[0]
    alpha = (2 * depth) ** 0.25
    bp = x_prompt.shape[0]
    d_ff2 = w_up.shape[2]
    zero_h = jnp.zeros((bp, N_GROUPS, SSM_STATE), F32)
    zero_conv = jnp.zeros((bp, CONV_W - 1, d_ff2), F32)
    layer_params = (w_in, b_gate, w_pa, w_pb, w_o, a_re, a_im, log_dt, b_re, b_im, c_re, c_im, d_skip, w_glu, b_glu,
                    ln1_g, ln1_b, w_up, conv_w, conv_b, w_down, ln2_g, ln2_b)
    xp, xs = x_prompt, x_sample
    outs_p, outs_s = [], []
    for l in range(depth):
        weights = _layer_weights(*(a[l] for a in layer_params))
        rp = _trunk_layer(xp, None, None, None, zero_h, zero_h, zero_conv, weights, alpha)
        rs = _trunk_layer(xs, cache_k[l], cache_v[l], cache_idx_k[l], state_ssm_re[l].astype(F32),
                          state_ssm_im[l].astype(F32), state_conv[l], weights, alpha)
        xp, xs = rp[0], rs[0]
        outs_p.append(rp[1:])
        outs_s.append(rs[1:])
    stack = lambda outs: tuple(jnp.stack([o[j] for o in outs], axis=0) for j in range(6))
    return (xp, xs) + stack(outs_p) + stack(outs_s)
```

```python
import functools
import math

import jax
import jax.numpy as jnp
from jax import lax
from jax.experimental import pallas as pl
from jax.experimental.pallas import tpu as pltpu

CHUNK = 64
N_HEADS = 8
HEAD_DIM = 64
D_ATT = N_HEADS * HEAD_DIM
N_IDX_HEADS = 8
IDX_DIM = 64
TOPK_MAX = 256
D_SSM = 512
SSM_GROUP = 16
N_GROUPS = D_SSM // SSM_GROUP
SSM_STATE = 64
N_STATE = N_GROUPS * SSM_STATE
CONV_W = 3
ROPE_THETA = 10000.0
LN_EPS = 1e-5

LANES = 128
SUBLANES = 8
VMEM_LIMIT = 60 * 1024 * 1024

F32 = jnp.float32
BF16 = jnp.bfloat16
I32 = jnp.int32
NEG_INF = float("-inf")
INT_MIN = -(2 ** 31)
HIGH16 = -(2 ** 16)


def _f32_key_of_neg_inf():
    bits = 0xFF800000
    signed = bits - (1 << 32)
    return signed ^ ((signed >> 31) & 0x7FFFFFFF)


NEG_INF_KEY = _f32_key_of_neg_inf()
SEARCH_UNROLL = 4
SOFTMAX_SUM_FLOOR = 1e-30


def _resident(block_shape, index_map):
    return pl.BlockSpec(block_shape, index_map, pipeline_mode=pl.Buffered(1))


def _params(*sem):
    return pltpu.CompilerParams(dimension_semantics=sem, vmem_limit_bytes=VMEM_LIMIT)


def _inproj_kernel(x_ref, cos_ref, sin_ref, wq_ref, wk_ref, wv_ref, wqi_ref, wkw_ref, wu_ref,
                   q_ref, kf_ref, kb_ref, vf_ref, vb_ref, qi_ref, kw_ref, u_ref, *, wi_scale):
    x = x_ref[...].astype(BF16)
    cos = cos_ref[...]
    sin = sin_ref[...]
    tm = x.shape[0]
    lane = lax.broadcasted_iota(I32, (tm, LANES), 1)
    first_half = (lane % HEAD_DIM) < (HEAD_DIM // 2)

    def proj(w_ref):
        return jnp.dot(x, w_ref[...], preferred_element_type=F32)

    def rope_group(y):
        rot = jnp.where(first_half, pltpu.roll(y, LANES - HEAD_DIM // 2, 1), pltpu.roll(y, HEAD_DIM // 2, 1))
        return y * cos + rot * sin

    def rope(y):
        return jnp.concatenate([rope_group(y[:, c * LANES:(c + 1) * LANES]) for c in range(y.shape[1] // LANES)],
                               axis=1)

    q = rope(proj(wq_ref))
    q_ref[...] = (q * (HEAD_DIM ** -0.5)).astype(BF16)
    k = rope(proj(wk_ref))
    kf_ref[...] = k
    kb_ref[...] = k.astype(BF16)
    v = proj(wv_ref)
    vf_ref[...] = v
    vb_ref[...] = v.astype(BF16)
    qi_ref[...] = rope(proj(wqi_ref)).astype(BF16)
    kw = proj(wkw_ref)
    kw_ref[...] = jnp.where(lane < IDX_DIM, rope_group(kw), kw * wi_scale)
    u_ref[...] = proj(wu_ref)


def _inproj(x2d, cos_tab, sin_tab, ws, tm, n_tab_blocks):
    rows, d = x2d.shape
    wq, wk, wv, wqi, wkw, wu = ws
    row_blk = lambda w: pl.BlockSpec((tm, w), lambda i: (i, 0))
    tab_blk = pl.BlockSpec((tm, LANES), lambda i: (i % n_tab_blocks, 0))
    w_blk = lambda w: _resident(w.shape, lambda i: (0, 0))
    out_shapes = [
        jax.ShapeDtypeStruct((rows, D_ATT), BF16),
        jax.ShapeDtypeStruct((rows, D_ATT), F32),
        jax.ShapeDtypeStruct((rows, D_ATT), BF16),
        jax.ShapeDtypeStruct((rows, D_ATT), F32),
        jax.ShapeDtypeStruct((rows, D_ATT), BF16),
        jax.ShapeDtypeStruct((rows, D_ATT), BF16),
        jax.ShapeDtypeStruct((rows, LANES), F32),
        jax.ShapeDtypeStruct((rows, D_SSM), F32),
    ]
    wi_scale = (N_IDX_HEADS ** -0.5) * (IDX_DIM ** -0.5)
    return pl.pallas_call(
        functools.partial(_inproj_kernel, wi_scale=wi_scale),
        grid=(rows // tm,),
        in_specs=[row_blk(d), tab_blk, tab_blk, w_blk(wq), w_blk(wk), w_blk(wv), w_blk(wqi), w_blk(wkw), w_blk(wu)],
        out_specs=[row_blk(s.shape[1]) for s in out_shapes],
        out_shape=out_shapes,
        compiler_params=_params("parallel"),
    )(x2d, cos_tab, sin_tab, wq, wk, wv, wqi, wkw, wu)


def _attn_kernel(q_ref, qi_ref, kw_ref, kmax_ref, kiki_ref, k_ref, v_ref, o_ref,
                 key_s, keyhi_s, qm_s, qim_s, wi_s, mb_s, bias_s, l_s, acc_s, m2_s, l2_s, lga_s, lgb_s,
                 *, tq, tk, group, n_keys, past, k_keep, idx_bits):
    q0 = pl.program_id(1) * tq
    n_pairs = N_HEADS // 2
    n_chunks = tk // LANES
    row = lax.broadcasted_iota(I32, (tq, 1), 0)
    q_lim = jnp.minimum(((past + q0 + row) // CHUNK + 1) * CHUNK, n_keys)
    q_lim_b = jnp.broadcast_to(q_lim, (tq, LANES))
    blk_lim = jnp.minimum(((past + q0 + tq - 1) // CHUNK + 1) * CHUNK, n_keys)
    nk = (blk_lim + tk - 1) // tk
    lane_q = lax.broadcasted_iota(I32, (tq, LANES), 1)
    lo_half = lane_q < HEAD_DIM
    nt = (((1,), (1,)), ((), ()))
    pair_of = lambda h: slice((h // 2) * LANES, (h // 2 + 1) * LANES)
    chunk_of = lambda c: slice(c * LANES, (c + 1) * LANES)

    kw = kw_ref[0]
    kmax = kmax_ref[0]
    halves = (slice(0, tq), slice(tq, 2 * tq))
    for pr in range(n_pairs):
        qi_pair = qi_ref[0, :, pair_of(2 * pr)]
        q_pair = q_ref[0, :, pair_of(2 * pr)]
        for half, keep in enumerate((lo_half, jnp.logical_not(lo_half))):
            h = 2 * pr + half
            qim_s[pr, halves[half]] = jnp.where(keep, qi_pair, jnp.zeros_like(qi_pair))
            qm = jnp.where(keep, q_pair, jnp.zeros_like(q_pair))
            qm_s[pr, halves[half]] = qm
            wi_s[h] = jnp.broadcast_to(kw[:, IDX_DIM + h:IDX_DIM + h + 1], (tq, LANES))
            qf = qm.astype(F32)
            q_norm = jnp.sqrt(jnp.sum(qf * qf, axis=1, keepdims=True))
            mb_s[h] = jnp.broadcast_to(q_norm * kmax[:, h:h + 1], (tq, LANES))

    def key_base(kt, c):
        return pl.multiple_of(kt * tk + c * LANES, LANES)

    def tile_start(kt):
        return pl.multiple_of(jnp.minimum(kt, nk - 1) * tk, tk)

    def two_stage_loop(produce, consume):
        produce(0, lga_s)

        def body(j, carry):
            produce(2 * j + 1, lgb_s)
            consume(2 * j, lga_s)
            produce(2 * j + 2, lga_s)
            consume(2 * j + 1, lgb_s)
            return carry

        lax.fori_loop(0, (nk + 1) // 2, body, 0)

    def produce_scores(kt, buf):
        kiki = kiki_ref[0, pl.ds(tile_start(kt), tk), :]
        for pr in range(n_pairs):
            buf[pr] = lax.dot_general(qim_s[pr], kiki, nt, preferred_element_type=F32)

    def consume_scores(kt, buf):
        ktc = jnp.minimum(kt, nk - 1)
        score = [jnp.zeros((tq, LANES), F32) for _ in range(n_chunks)]
        for pr in range(n_pairs):
            for half in range(2):
                w = wi_s[2 * pr + half]
                for c in range(n_chunks):
                    score[c] = score[c] + w * jnp.maximum(buf[pr, halves[half], chunk_of(c)], 0.0)
        for c in range(n_chunks):
            kb = key_base(ktc, c)
            bits = pltpu.bitcast(jnp.where(kb + lane_q < q_lim_b, score[c], NEG_INF), I32)
            key_s[:, pl.ds(kb, LANES)] = bits ^ ((bits >> 31) & 0x7FFFFFFF)
            keyhi_s[:, pl.ds(kb, LANES)] = pltpu.bitcast(bits & HIGH16, F32).astype(BF16)

    two_stage_loop(produce_scores, consume_scores)

    n_groups = (nk + group - 1) // group

    def fill_tile(kt, carry):
        for c in range(n_chunks):
            kb = key_base(kt, c)
            key_s[:, pl.ds(kb, LANES)] = jnp.full((tq, LANES), NEG_INF_KEY, I32)
            keyhi_s[:, pl.ds(kb, LANES)] = jnp.full((tq, LANES), NEG_INF, BF16)
        return carry

    lax.fori_loop(nk, n_groups * group, fill_tile, 0)

    def group_chunks(g_idx):
        return [pl.multiple_of(g_idx * (group * tk) + c * LANES, LANES) for c in range(group * n_chunks)]

    def count(pred):
        def body(g_idx, cnt):
            for kb in group_chunks(g_idx):
                cnt = cnt + pred(key_s[:, pl.ds(kb, LANES)], kb).astype(I32)
            return cnt
        cnt = lax.fori_loop(0, n_groups, body, jnp.zeros((tq, LANES), I32))
        return jnp.sum(cnt, axis=1, keepdims=True)

    def count_ge(c):
        cb = jnp.broadcast_to(c, (tq, LANES))
        return count(lambda key, kb: key >= cb)

    def count_ge_high(c):
        c_bits = (c ^ ((c >> 31) & 0x7FFFFFFF)) & HIGH16
        cb = jnp.broadcast_to(pltpu.bitcast(c_bits, F32).astype(BF16), (tq, LANES))
        one, nil = jnp.ones((tq, LANES), BF16), jnp.zeros((tq, LANES), BF16)

        def body(g_idx, cnt):
            for kb in group_chunks(g_idx):
                cnt = cnt + jnp.where(keyhi_s[:, pl.ds(kb, LANES)] >= cb, one, nil)
            return cnt
        cnt = lax.fori_loop(0, n_groups, body, nil)
        return jnp.sum(cnt.astype(F32), axis=1, keepdims=True).astype(I32)

    def search(counter, first_bit, n_bits, state):
        def cond(st):
            step, _, _, done = st
            return jnp.logical_and(step < n_bits // SEARCH_UNROLL, jnp.min(done) == 0)

        def body(st):
            step, t, thr, done = st
            for i in range(SEARCH_UNROLL):
                cand = t + jnp.left_shift(jnp.int32(1), first_bit - i - step * SEARCH_UNROLL)
                cnt = counter(cand)
                t = jnp.where(cnt >= k_keep, cand, t)
                newly = jnp.logical_and(cnt == k_keep, done == 0)
                thr = jnp.where(newly, jnp.maximum(cand - 1, NEG_INF_KEY), thr)
                done = jnp.where(newly, 1, done)
            return step + 1, t, thr, done

        return lax.while_loop(cond, body, (jnp.int32(0),) + state)[1:]

    zero = jnp.zeros((tq, 1), I32)
    state = search(count_ge_high, 31, 16, (jnp.full((tq, 1), INT_MIN, I32), zero, zero))
    t_fin, thr_early, done = search(count_ge, 15, 16, state)
    t_fin = jnp.maximum(t_fin, NEG_INF_KEY)

    def resolve_ties(_):
        cnt_gt = count_ge(t_fin + 1)
        n_eq = count_ge(t_fin) - cnt_gt
        want = k_keep - cnt_gt
        tb = jnp.broadcast_to(t_fin, (tq, LANES))
        need = jnp.logical_and(jnp.logical_and(done == 0, n_eq > want), t_fin > NEG_INF_KEY)

        def cut_search(_):
            def body(j, c):
                cand = c + jnp.left_shift(jnp.int32(1), idx_bits - 1 - j)
                cb = jnp.broadcast_to(cand, (tq, LANES))
                below = count(lambda key, kb: jnp.logical_and(key == tb, kb + lane_q < cb))
                return jnp.where(below < want, cand, c)
            return lax.fori_loop(0, idx_bits, body, zero)

        cut = lax.cond(jnp.max(need.astype(I32)) > 0, cut_search, lambda _: zero, 0)
        all_ties = jnp.where(t_fin > NEG_INF_KEY, jnp.int32(2 ** 30), jnp.int32(-1))
        return jnp.where(need, cut, all_ties)

    cut_full = lax.cond(jnp.min(done) == 0, resolve_ties, lambda _: jnp.full((tq, 1), -1, I32), 0)
    thr = jnp.where(done == 1, thr_early, t_fin)
    cut = jnp.where(done == 1, -1, cut_full)
    thr_b = jnp.broadcast_to(thr, (tq, LANES))
    cut_b = jnp.broadcast_to(cut, (tq, LANES))

    def tile_bias(kt, valid):
        thr_t = jnp.where(valid, thr_b, jnp.int32(2 ** 31 - 1))
        cut_t = jnp.where(valid, cut_b, jnp.int32(-1))
        for c in range(n_chunks):
            kb = key_base(kt, c)
            key = key_s[:, pl.ds(kb, LANES)]
            sel = jnp.logical_or(key > thr_t, jnp.logical_and(key == thr_t, kb + lane_q <= cut_t))
            bias_s[:, chunk_of(c)] = jnp.where(sel, 0.0, NEG_INF)

    def emit(l_of_head):
        for pr in range(n_pairs):
            even = acc_s[pr, halves[0]] / l_of_head(2 * pr)
            odd = acc_s[pr, halves[1]] / l_of_head(2 * pr + 1)
            o_ref[0, :, pair_of(2 * pr)] = jnp.where(lo_half, even, odd).astype(o_ref.dtype)

    l_s[...] = jnp.zeros(l_s.shape, F32)
    acc_s[...] = jnp.zeros(acc_s.shape, F32)

    def produce_logits(kt, buf):
        k0 = tile_start(kt)
        for pr in range(n_pairs):
            buf[pr] = lax.dot_general(qm_s[pr], k_ref[0, pl.ds(k0, tk), pair_of(2 * pr)], nt,
                                      preferred_element_type=F32)

    def consume_logits(kt, buf):
        k0 = tile_start(kt)
        tile_bias(jnp.minimum(kt, nk - 1), kt < nk)
        for pr in range(n_pairs):
            p = []
            for half in range(2):
                h = 2 * pr + half
                shift = mb_s[h]
                l_part = l_s[h]
                p_row = []
                for c in range(n_chunks):
                    e = jnp.exp(buf[pr, halves[half], chunk_of(c)] + bias_s[:, chunk_of(c)] - shift)
                    l_part = l_part + e
                    p_row.append(e.astype(BF16))
                l_s[h] = l_part
                p.append(jnp.concatenate(p_row, axis=1))
            acc_s[pr] += jnp.dot(jnp.concatenate(p, axis=0), v_ref[0, pl.ds(k0, tk), pair_of(2 * pr)],
                                 preferred_element_type=F32)

    two_stage_loop(produce_logits, consume_logits)
    l_rows = [jnp.sum(l_s[h], axis=1, keepdims=True) for h in range(N_HEADS)]
    l_min = functools.reduce(jnp.minimum, [jnp.min(l) for l in l_rows])
    well_scaled = l_min >= SOFTMAX_SUM_FLOOR

    @pl.when(well_scaled)
    def _emit_bound():
        emit(lambda h: l_rows[h])

    @pl.when(jnp.logical_not(well_scaled))
    def _running_max():
        m2_s[...] = jnp.full(m2_s.shape, NEG_INF, F32)
        l2_s[...] = jnp.zeros(l2_s.shape, F32)
        acc_s[...] = jnp.zeros(acc_s.shape, F32)

        def online_tile(kt, carry):
            k0 = pl.multiple_of(kt * tk, tk)
            tile_bias(kt, True)
            for h in range(N_HEADS):
                rows = halves[h % 2]
                logits = lax.dot_general(qm_s[h // 2, rows], k_ref[0, pl.ds(k0, tk), pair_of(h)], nt,
                                         preferred_element_type=F32) + bias_s[...]
                m_old = m2_s[h]
                m_new = jnp.maximum(m_old, jnp.max(logits, axis=1, keepdims=True))
                m_safe = jnp.where(m_new == NEG_INF, 0.0, m_new)
                alpha = jnp.exp(m_old - m_safe)
                p = jnp.exp(logits - m_safe)
                l2_s[h] = alpha * l2_s[h] + jnp.sum(p, axis=1, keepdims=True)
                acc_s[h // 2, rows] = alpha * acc_s[h // 2, rows] + jnp.dot(
                    p.astype(BF16), v_ref[0, pl.ds(k0, tk), pair_of(h)], preferred_element_type=F32)
                m2_s[h] = m_new
            return carry

        lax.fori_loop(0, nk, online_tile, 0)
        emit(lambda h: l2_s[h])


def _pick_key_tile(l_pad, k_keep):
    for tk in (256, 384, 512, 640, 768, 896, 1024):
        if l_pad % tk == 0 and tk >= k_keep:
            return tk
    return l_pad


def _attention(q, qi, kw, kmax, kiki, k_all, v_all, *, n_keys, past, k_keep, tq):
    b, t, _ = q.shape
    l_pad = k_all.shape[1]
    tk = _pick_key_tile(l_pad, k_keep)
    assert l_pad % tk == 0 and tk >= k_keep and t % tq == 0
    assert l_pad // LANES <= 256
    idx_bits = max(1, int(math.ceil(math.log2(l_pad))))
    group = max(g for g in (4, 2, 1) if l_pad % (g * tk) == 0)
    qblk = lambda w: pl.BlockSpec((1, tq, w), lambda bi, i: (bi, i, 0))
    kvblk = lambda w: _resident((1, l_pad, w), lambda bi, i: (bi, 0, 0))
    per_head = lambda w, dt: pltpu.VMEM((N_HEADS, tq, w), dt)
    per_pair = lambda dt: pltpu.VMEM((N_HEADS // 2, 2 * tq, LANES), dt)
    return pl.pallas_call(
        functools.partial(_attn_kernel, tq=tq, tk=tk, group=group, n_keys=n_keys, past=past, k_keep=k_keep,
                          idx_bits=idx_bits),
        grid=(b, t // tq),
        in_specs=[qblk(D_ATT), qblk(D_ATT), qblk(LANES), pl.BlockSpec((1, 1, LANES), lambda bi, i: (bi, 0, 0)),
                  kvblk(LANES), kvblk(D_ATT), kvblk(D_ATT)],
        out_specs=qblk(D_ATT),
        out_shape=jax.ShapeDtypeStruct((b, t, D_ATT), BF16),
        scratch_shapes=[
            pltpu.VMEM((tq, l_pad), I32),
            pltpu.VMEM((tq, l_pad), BF16),
            per_pair(BF16),
            per_pair(BF16),
            per_head(LANES, F32),
            per_head(LANES, F32),
            pltpu.VMEM((tq, tk), F32),
            per_head(LANES, F32),
            per_pair(F32),
            per_head(1, F32),
            per_head(1, F32),
            pltpu.VMEM((N_HEADS // 2, 2 * tq, tk), F32),
            pltpu.VMEM((N_HEADS // 2, 2 * tq, tk), F32),
        ],
        compiler_params=_params("arbitrary", "arbitrary"),
    )(q, qi, kw, kmax, kiki, k_all, v_all)


def _cmul(ar, ai, br, bi):
    return ar * br - ai * bi, ar * bi + ai * br


def _s5_kernel(u_ref, h0re_ref, h0im_ref, are_ref, aim_ref, ldt_ref, bre_ref, bim_ref, cre_ref, cim_ref,
               dskip_ref, wglu_ref, bglu_ref, y_ref, hre_ref, him_ref,
               bbar_s, lvl_s, pw_s, h_s, *, ts, lane_chunk):
    n = N_STATE
    t_idx = pl.program_id(1)
    first = jnp.logical_and(pl.program_id(0) == 0, t_idx == 0)

    @pl.when(first)
    def _discretise():
        dt = jnp.exp(ldt_ref[...])
        ar, ai = are_ref[...], aim_ref[...]
        mag = jnp.exp(dt * ar)
        abr, abi = mag * jnp.cos(dt * ai), mag * jnp.sin(dt * ai)
        den = ar * ar + ai * ai
        nr, ni = abr - 1.0, abi
        f_re, f_im = (nr * ar + ni * ai) / den, (ni * ar - nr * ai) / den
        reps = D_SSM // SSM_GROUP
        bre = jnp.concatenate([bre_ref[...]] * reps, axis=0)
        bim = jnp.concatenate([bim_ref[...]] * reps, axis=0)
        r_grp = lax.broadcasted_iota(I32, (D_SSM, n), 0) // SSM_GROUP
        c_grp = lax.broadcasted_iota(I32, (D_SSM, n), 1) // SSM_STATE
        same = r_grp == c_grp
        bbar_s[:, :n] = jnp.where(same, f_re * bre - f_im * bim, 0.0).astype(BF16)
        bbar_s[:, n:] = jnp.where(same, f_re * bim + f_im * bre, 0.0).astype(BF16)
        a1 = (abr, abi)
        a2 = _cmul(*a1, *a1)
        a3 = _cmul(*a2, *a1)
        a4 = _cmul(*a2, *a2)
        a5 = _cmul(*a4, *a1)
        a6 = _cmul(*a4, *a2)
        a7 = _cmul(*a4, *a3)
        a8 = _cmul(*a4, *a4)
        rows = lax.broadcasted_iota(I32, (SUBLANES, n), 0)
        for part in range(2):
            pw = jnp.zeros((SUBLANES, n), F32)
            for r, a in enumerate((a1, a2, a3, a4, a5, a6, a7, a8)):
                pw = jnp.where(rows == r, a[part], pw)
            pw_s[part] = pw
            for lv, (a, dist) in enumerate(((a1, 1), (a2, 2), (a4, 4))):
                lvl_s[lv, part] = jnp.where(rows >= dist, a[part], 0.0)

    @pl.when(t_idx == 0)
    def _load_state():
        h_s[0:SUBLANES, :] = jnp.zeros((SUBLANES, 2 * n), F32)
        h_s[SUBLANES - 1:SUBLANES, :n] = h0re_ref[0]
        h_s[SUBLANES - 1:SUBLANES, n:] = h0im_ref[0]

    u = u_ref[0]
    u_bf = u.astype(BF16)
    for c in range(2 * n // 512):
        h_s[SUBLANES:, c * 512:(c + 1) * 512] = jnp.dot(u_bf, bbar_s[:, c * 512:(c + 1) * 512],
                                                        preferred_element_type=F32)

    def tile_step(j, carry):
        r0 = pl.multiple_of(j * SUBLANES, SUBLANES)
        for c in range(n // lane_chunk):
            re_sl = slice(c * lane_chunk, (c + 1) * lane_chunk)
            im_sl = slice(n + c * lane_chunk, n + (c + 1) * lane_chunk)
            xr = h_s[pl.ds(r0 + SUBLANES, SUBLANES), re_sl]
            xi = h_s[pl.ds(r0 + SUBLANES, SUBLANES), im_sl]
            for lv, dist in enumerate((1, 2, 4)):
                sr, si = pltpu.roll(xr, dist, 0), pltpu.roll(xi, dist, 0)
                dr, di = _cmul(lvl_s[lv, 0, :, re_sl], lvl_s[lv, 1, :, re_sl], sr, si)
                xr, xi = xr + dr, xi + di
            prev_r = h_s[pl.ds(r0, SUBLANES), re_sl][SUBLANES - 1:SUBLANES, :]
            prev_i = h_s[pl.ds(r0, SUBLANES), im_sl][SUBLANES - 1:SUBLANES, :]
            cr, ci = _cmul(pw_s[0, :, re_sl], pw_s[1, :, re_sl], prev_r, prev_i)
            h_s[pl.ds(r0 + SUBLANES, SUBLANES), re_sl] = xr + cr
            h_s[pl.ds(r0 + SUBLANES, SUBLANES), im_sl] = xi + ci
        return carry

    lax.fori_loop(0, ts // SUBLANES, tile_step, 0)

    nt = (((1,), (1,)), ((), ()))
    h_re = h_s[SUBLANES:, :n].astype(BF16)
    h_im = h_s[SUBLANES:, n:].astype(BF16)
    y = (lax.dot_general(h_re, cre_ref[...], nt, preferred_element_type=F32)
         - lax.dot_general(h_im, cim_ref[...], nt, preferred_element_type=F32))
    y = y + dskip_ref[...] * u
    y = jax.nn.gelu(y)
    gate = jax.nn.sigmoid(jnp.dot(y.astype(BF16), wglu_ref[...], preferred_element_type=F32) + bglu_ref[...])
    y_ref[0] = (y * gate).astype(y_ref.dtype)

    last = h_s[ts:ts + SUBLANES, :]
    h_s[0:SUBLANES, :] = last

    @pl.when(t_idx == pl.num_programs(1) - 1)
    def _emit_state():
        hre_ref[0] = last[SUBLANES - 1:SUBLANES, :n]
        him_ref[0] = last[SUBLANES - 1:SUBLANES, n:]


def _s5(u, h0_re, h0_im, prm, ts):
    b, t, _ = u.shape
    n = N_STATE
    a_re, a_im, ldt, bre_d, bim_d, cre_t, cim_t, d_skip, w_glu, b_glu = prm
    const = lambda a: _resident(a.shape, lambda bi, i: (0,) * a.ndim)
    st_blk = pl.BlockSpec((1, 1, n), lambda bi, i: (bi, 0, 0))
    y, hre, him = pl.pallas_call(
        functools.partial(_s5_kernel, ts=ts, lane_chunk=512),
        grid=(b, t // ts),
        in_specs=[pl.BlockSpec((1, ts, D_SSM), lambda bi, i: (bi, i, 0)), st_blk, st_blk,
                  const(a_re), const(a_im), const(ldt), const(bre_d), const(bim_d), const(cre_t), const(cim_t),
                  const(d_skip), const(w_glu), const(b_glu)],
        out_specs=[pl.BlockSpec((1, ts, D_SSM), lambda bi, i: (bi, i, 0)), st_blk, st_blk],
        out_shape=[jax.ShapeDtypeStruct((b, t, D_SSM), BF16),
                   jax.ShapeDtypeStruct((b, 1, n), F32), jax.ShapeDtypeStruct((b, 1, n), F32)],
        scratch_shapes=[
            pltpu.VMEM((D_SSM, 2 * n), BF16),
            pltpu.VMEM((3, 2, SUBLANES, n), F32),
            pltpu.VMEM((2, SUBLANES, n), F32),
            pltpu.VMEM((SUBLANES + ts, 2 * n), F32),
        ],
        compiler_params=_params("arbitrary", "arbitrary"),
    )(u, h0_re, h0_im, a_re, a_im, ldt, bre_d, bim_d, cre_t, cim_t, d_skip, w_glu, b_glu)
    return y, hre, him


def _layer_norm(z, g, b):
    mu = jnp.mean(z, axis=-1, keepdims=True)
    zc = z - mu
    var = jnp.mean(zc * zc, axis=-1, keepdims=True)
    return zc * lax.rsqrt(var + LN_EPS) * g + b


def _merge_kernel(x_ref, att_ref, ssm_ref, wg_ref, bg_ref, wpa_ref, wpb_ref, wo_ref, g_ref, b_ref, o_ref, *, alpha):
    x = x_ref[...]
    d = x.shape[1]
    gates = jax.nn.sigmoid(jnp.dot(x.astype(BF16), wg_ref[...], preferred_element_type=F32) + bg_ref[...])
    mix = (gates[:, :d] * jnp.dot(att_ref[...], wpa_ref[...], preferred_element_type=F32)
           + gates[:, d:] * jnp.dot(ssm_ref[...], wpb_ref[...], preferred_element_type=F32))
    z = alpha * x + jnp.dot(mix.astype(BF16), wo_ref[...], preferred_element_type=F32)
    o_ref[...] = _layer_norm(z, g_ref[...], b_ref[...])


def _merge(x2d, att2d, ssm2d, prm, alpha, tm):
    rows, d = x2d.shape
    w_g, b_g, w_pa, w_pb, w_o, ln_g, ln_b = prm
    row_blk = lambda w: pl.BlockSpec((tm, w), lambda i: (i, 0))
    const = lambda a: _resident(a.shape, lambda i: (0, 0))
    return pl.pallas_call(
        functools.partial(_merge_kernel, alpha=alpha),
        grid=(rows // tm,),
        in_specs=[row_blk(d), row_blk(D_ATT), row_blk(D_SSM)] + [const(a) for a in prm],
        out_specs=row_blk(d),
        out_shape=jax.ShapeDtypeStruct((rows, d), F32),
        compiler_params=_params("parallel"),
    )(x2d, att2d, ssm2d, *prm)


def _ffn_kernel(x_ref, cbuf_ref, wup_ref, cw_ref, cb_ref, wdn_ref, g_ref, b_ref, o_ref, nconv_ref,
                h_s, acc_s, *, tm, d_ff, fc, alpha):
    t_idx = pl.program_id(1)
    hist = CONV_W - 1

    @pl.when(t_idx == 0)
    def _load_history():
        h_s[0:SUBLANES, :] = jnp.zeros((SUBLANES, h_s.shape[1]), F32)
        h_s[SUBLANES - hist:SUBLANES, :] = cbuf_ref[0]

    x = x_ref[0]
    x_bf = x.astype(BF16)
    for c in range(d_ff // fc):
        halves = []
        for off in (c * fc, d_ff + c * fc):
            cols = slice(off, off + fc)
            h_s[SUBLANES:, cols] = jnp.dot(x_bf, wup_ref[:, cols], preferred_element_type=F32)
            conv = cb_ref[:, cols] + sum(cw_ref[j:j + 1, cols] * h_s[SUBLANES - hist + j:SUBLANES - hist + j + tm, cols]
                                         for j in range(CONV_W))
            halves.append(conv)
        act = (jax.nn.gelu(halves[0]) * halves[1]).astype(BF16)
        part = jnp.dot(act, wdn_ref[c * fc:(c + 1) * fc, :], preferred_element_type=F32)
        if c == 0:
            acc_s[...] = part
        else:
            acc_s[...] += part
    o_ref[0] = _layer_norm(alpha * x + acc_s[...], g_ref[...], b_ref[...])

    tail = h_s[tm:tm + SUBLANES, :]
    h_s[0:SUBLANES, :] = tail

    @pl.when(t_idx == pl.num_programs(1) - 1)
    def _emit_history():
        nconv_ref[0] = tail[SUBLANES - hist:, :]


def _ffn(x, conv_buf, prm, alpha, tm):
    b, t, d = x.shape
    w_up, conv_w, conv_b, w_dn, ln_g, ln_b = prm
    d_ff = w_dn.shape[0]
    fc = 256
    assert d_ff % fc == 0 and t % tm == 0 and tm % SUBLANES == 0
    const = lambda a: _resident(a.shape, lambda bi, i: (0, 0))
    hist_blk = pl.BlockSpec((1, CONV_W - 1, 2 * d_ff), lambda bi, i: (bi, 0, 0))
    return pl.pallas_call(
        functools.partial(_ffn_kernel, tm=tm, d_ff=d_ff, fc=fc, alpha=alpha),
        grid=(b, t // tm),
        in_specs=[pl.BlockSpec((1, tm, d), lambda bi, i: (bi, i, 0)), hist_blk] + [const(a) for a in prm],
        out_specs=[pl.BlockSpec((1, tm, d), lambda bi, i: (bi, i, 0)), hist_blk],
        out_shape=[jax.ShapeDtypeStruct((b, t, d), F32), jax.ShapeDtypeStruct((b, CONV_W - 1, 2 * d_ff), F32)],
        scratch_shapes=[pltpu.VMEM((SUBLANES + tm, 2 * d_ff), F32),
                        pltpu.VMEM((tm, d), F32)],
        compiler_params=_params("arbitrary", "arbitrary"),
    )(x, conv_buf, *prm)


def _rope_tables(pos):
    half = HEAD_DIM // 2
    inv = ROPE_THETA ** (-jnp.arange(half, dtype=F32) / half)
    ang = pos.astype(F32)[:, None] * inv[None, :]
    cos, sin = jnp.cos(ang), jnp.sin(ang)
    cos_tab = jnp.tile(cos, (1, LANES // half))
    sin_tab = jnp.tile(jnp.concatenate([-sin, sin], axis=1), (1, LANES // HEAD_DIM))
    return cos_tab, sin_tab


def _layer_weights(w_in, b_gate, w_pa, w_pb, w_o, a_re, a_im, log_dt, b_re, b_im, c_re, c_im, d_skip, w_glu,
                   b_glu, ln1_g, ln1_b, w_up, conv_w, conv_b, w_down, ln2_g, ln2_b):
    d = w_in.shape[0]
    o = 0
    pieces = []
    for w in (D_ATT, D_ATT, D_ATT, N_IDX_HEADS * IDX_DIM, IDX_DIM + N_IDX_HEADS, D_SSM, 2 * d):
        pieces.append(w_in[:, o:o + w])
        o += w
    wq, wk, wv, wqi, wkw, wu, wg = pieces
    wkw = jnp.pad(wkw, ((0, 0), (0, LANES - wkw.shape[1])))
    inproj = tuple(w.astype(BF16) for w in (wq, wk, wv, wqi, wkw, wu))
    n = N_STATE
    row = lambda a: a.reshape(1, -1).astype(F32)
    same_group = jnp.eye(N_GROUPS, dtype=bool)[:, None, :, None]
    blockdiag_t = lambda c: jnp.where(same_group, c[:, :, None, :], 0.0).reshape(D_SSM, n).astype(BF16)
    dense = lambda bm: jnp.transpose(bm, (2, 0, 1)).reshape(SSM_GROUP, n).astype(F32)
    s5 = (row(a_re), row(a_im), row(jnp.repeat(log_dt, SSM_STATE)), dense(b_re), dense(b_im),
          blockdiag_t(c_re), blockdiag_t(c_im), row(d_skip), w_glu.astype(BF16), row(b_glu))
    merge = (wg.astype(BF16), row(b_gate), w_pa.astype(BF16), w_pb.astype(BF16), w_o.astype(BF16),
             row(ln1_g), row(ln1_b))
    ffn = (w_up.astype(BF16), conv_w.astype(F32), row(conv_b), w_down.astype(BF16), row(ln2_g), row(ln2_b))
    return inproj, s5, merge, ffn


def _largest_tile(n, cap):
    t = min(n, cap)
    while n % t:
        t -= SUBLANES
    return t


def _trunk_layer(x, past_k, past_v, past_ki, h0_re, h0_im, conv_buf, weights, alpha):
    bn, t, d = x.shape
    p = 0 if past_k is None else past_k.shape[1]
    n_keys = p + t
    k_keep = min(TOPK_MAX, n_keys // 4)
    inproj_w, s5_w, merge_w, ffn_w = weights
    rows = bn * t
    tm = _largest_tile(rows, 512)

    cos_tab, sin_tab = _rope_tables(p + jnp.arange(t))
    if t % tm == 0:
        n_tab_blocks = t // tm
    else:
        cos_tab, sin_tab = jnp.tile(cos_tab, (bn, 1)), jnp.tile(sin_tab, (bn, 1))
        n_tab_blocks = rows // tm
    q, k_f, k_b, v_f, v_b, qi, kw, u = _inproj(x.reshape(rows, d), cos_tab, sin_tab, inproj_w, tm, n_tab_blocks)

    r3 = lambda a: a.reshape(bn, t, a.shape[-1])
    ki_b = r3(kw)[:, :, :IDX_DIM].astype(BF16)
    k_all, v_all = r3(k_b), r3(v_b)
    if p:
        k_all = jnp.concatenate([past_k.reshape(bn, p, D_ATT).astype(BF16), k_all], axis=1)
        v_all = jnp.concatenate([past_v.reshape(bn, p, D_ATT).astype(BF16), v_all], axis=1)
        ki_b = jnp.concatenate([past_ki.astype(BF16), ki_b], axis=1)
    l_pad = -(-n_keys // LANES) * LANES
    l_pad = max(l_pad, 2 * LANES)
    pad = lambda a: jnp.pad(a, ((0, 0), (0, l_pad - n_keys), (0, 0)))
    kiki = jnp.concatenate([ki_b, ki_b], axis=-1)
    k_heads = k_all.astype(F32).reshape(bn, n_keys, N_HEADS, HEAD_DIM)
    kmax = jnp.sqrt(jnp.max(jnp.sum(k_heads * k_heads, axis=-1), axis=1))
    kmax = jnp.pad(kmax, ((0, 0), (0, LANES - N_HEADS))).reshape(bn, 1, LANES)
    tq = _largest_tile(t, 128)
    att = _attention(r3(q), r3(qi), r3(kw), kmax, pad(kiki), pad(k_all), pad(v_all),
                     n_keys=n_keys, past=p, k_keep=k_keep, tq=tq)

    ts = _largest_tile(t, 256)
    ssm, h_re, h_im = _s5(r3(u), h0_re.reshape(bn, 1, N_STATE), h0_im.reshape(bn, 1, N_STATE), s5_w, ts)

    x1 = _merge(x.reshape(rows, d), att.reshape(rows, D_ATT), ssm.reshape(rows, D_SSM), merge_w, alpha, tm)
    x2, new_conv = _ffn(x1.reshape(bn, t, d), conv_buf, ffn_w, alpha, _largest_tile(t, 256))

    k_out = r3(k_f).reshape(bn, t, N_HEADS, HEAD_DIM)
    v_out = r3(v_f).reshape(bn, t, N_HEADS, HEAD_DIM)
    ki_out = r3(kw)[:, :, :IDX_DIM]
    return (x2, k_out, v_out, ki_out, h_re.reshape(bn, N_GROUPS, SSM_STATE), h_im.reshape(bn, N_GROUPS, SSM_STATE),
            new_conv)


def kernel(x_prompt, x_sample, cache_k, cache_v, cache_idx_k, state_ssm_re, state_ssm_im, state_conv, w_in, b_gate, w_pa, w_pb, w_o, a_re, a_im, log_dt, b_re, b_im, c_re, c_im, d_skip, w_glu, b_glu, ln1_g, ln1_b, w_up, conv_w, conv_b, w_down, ln2_g, ln2_b):
    depth = w_in.shape[0]
    alpha = (2 * depth) ** 0.25
    bp = x_prompt.shape[0]
    d_ff2 = w_up.shape[2]
    zero_h = jnp.zeros((bp, N_GROUPS, SSM_STATE), F32)
    zero_conv = jnp.zeros((bp, CONV_W - 1, d_ff2), F32)
    layer_params = (w_in, b_gate, w_pa, w_pb, w_o, a_re, a_im, log_dt, b_re, b_im, c_re, c_im, d_skip, w_glu, b_glu,
                    ln1_g, ln1_b, w_up, conv_w, conv_b, w_down, ln2_g, ln2_b)
    xp, xs = x_prompt, x_sample
    outs_p, outs_s = [], []
    for l in range(depth):
        weights = _layer_weights(*(a[l] for a in layer_params))
        rp = _trunk_layer(xp, None, None, None, zero_h, zero_h, zero_conv, weights, alpha)
        rs = _trunk_layer(xs, cache_k[l], cache_v[l], cache_idx_k[l], state_ssm_re[l].astype(F32),
                          state_ssm_im[l].astype(F32), state_conv[l], weights, alpha)
        xp, xs = rp[0], rs[0]
        outs_p.append(rp[1:])
        outs_s.append(rs[1:])
    stack = lambda outs: tuple(jnp.stack([o[j] for o in outs], axis=0) for j in range(6))
    return (xp, xs) + stack(outs_p) + stack(outs_s)
```

```python
import functools
import math

import jax
import jax.numpy as jnp
from jax import lax
from jax.experimental import pallas as pl
from jax.experimental.pallas import tpu as pltpu

CHUNK = 64
N_HEADS = 8
HEAD_DIM = 64
D_ATT = N_HEADS * HEAD_DIM
N_IDX_HEADS = 8
IDX_DIM = 64
TOPK_MAX = 256
D_SSM = 512
SSM_GROUP = 16
N_GROUPS = D_SSM // SSM_GROUP
SSM_STATE = 64
N_STATE = N_GROUPS * SSM_STATE
CONV_W = 3
ROPE_THETA = 10000.0
LN_EPS = 1e-5

LANES = 128
SUBLANES = 8
VMEM_LIMIT = 60 * 1024 * 1024

F32 = jnp.float32
BF16 = jnp.bfloat16
I32 = jnp.int32
NEG_INF = float("-inf")
INT_MIN = -(2 ** 31)
HIGH16 = -(2 ** 16)
MIN_NORMAL_BITS = 0x00800000


def _f32_key_of_neg_inf():
    bits = 0xFF800000
    signed = bits - (1 << 32)
    return signed ^ ((signed >> 31) & 0x7FFFFFFF)


NEG_INF_KEY = _f32_key_of_neg_inf()
SEARCH_UNROLL = 4
LOW_VALUE_BITS = 18
LOW_STEP_BITS = 3
SOFTMAX_SUM_FLOOR = 1e-30


def _resident(block_shape, index_map):
    return pl.BlockSpec(block_shape, index_map, pipeline_mode=pl.Buffered(1))


def _params(*sem):
    return pltpu.CompilerParams(dimension_semantics=sem, vmem_limit_bytes=VMEM_LIMIT)


def _inproj_kernel(x_ref, cos_ref, sin_ref, wq_ref, wk_ref, wv_ref, wqi_ref, wkw_ref, wu_ref,
                   q_ref, kf_ref, kb_ref, vf_ref, vb_ref, qi_ref, kw_ref, u_ref, *, wi_scale):
    x = x_ref[...].astype(BF16)
    cos = cos_ref[...]
    sin = sin_ref[...]
    tm = x.shape[0]
    lane = lax.broadcasted_iota(I32, (tm, LANES), 1)
    first_half = (lane % HEAD_DIM) < (HEAD_DIM // 2)

    def proj(w_ref):
        return jnp.dot(x, w_ref[...], preferred_element_type=F32)

    def rope_group(y):
        rot = jnp.where(first_half, pltpu.roll(y, LANES - HEAD_DIM // 2, 1), pltpu.roll(y, HEAD_DIM // 2, 1))
        return y * cos + rot * sin

    def rope(y):
        return jnp.concatenate([rope_group(y[:, c * LANES:(c + 1) * LANES]) for c in range(y.shape[1] // LANES)],
                               axis=1)

    q = rope(proj(wq_ref))
    q_ref[...] = (q * (HEAD_DIM ** -0.5)).astype(BF16)
    k = rope(proj(wk_ref))
    kf_ref[...] = k
    kb_ref[...] = k.astype(BF16)
    v = proj(wv_ref)
    vf_ref[...] = v
    vb_ref[...] = v.astype(BF16)
    qi_ref[...] = rope(proj(wqi_ref)).astype(BF16)
    kw = proj(wkw_ref)
    kw_ref[...] = jnp.where(lane < IDX_DIM, rope_group(kw), kw * wi_scale)
    u_ref[...] = proj(wu_ref)


def _inproj(x2d, cos_tab, sin_tab, ws, tm, n_tab_blocks):
    rows, d = x2d.shape
    wq, wk, wv, wqi, wkw, wu = ws
    row_blk = lambda w: pl.BlockSpec((tm, w), lambda i: (i, 0))
    tab_blk = pl.BlockSpec((tm, LANES), lambda i: (i % n_tab_blocks, 0))
    w_blk = lambda w: _resident(w.shape, lambda i: (0, 0))
    out_shapes = [
        jax.ShapeDtypeStruct((rows, D_ATT), BF16),
        jax.ShapeDtypeStruct((rows, D_ATT), F32),
        jax.ShapeDtypeStruct((rows, D_ATT), BF16),
        jax.ShapeDtypeStruct((rows, D_ATT), F32),
        jax.ShapeDtypeStruct((rows, D_ATT), BF16),
        jax.ShapeDtypeStruct((rows, D_ATT), BF16),
        jax.ShapeDtypeStruct((rows, LANES), F32),
        jax.ShapeDtypeStruct((rows, D_SSM), F32),
    ]
    wi_scale = (N_IDX_HEADS ** -0.5) * (IDX_DIM ** -0.5)
    return pl.pallas_call(
        functools.partial(_inproj_kernel, wi_scale=wi_scale),
        grid=(rows // tm,),
        in_specs=[row_blk(d), tab_blk, tab_blk, w_blk(wq), w_blk(wk), w_blk(wv), w_blk(wqi), w_blk(wkw), w_blk(wu)],
        out_specs=[row_blk(s.shape[1]) for s in out_shapes],
        out_shape=out_shapes,
        compiler_params=_params("parallel"),
    )(x2d, cos_tab, sin_tab, wq, wk, wv, wqi, wkw, wu)


def _attn_kernel(q_ref, qi_ref, kw_ref, kmax_ref, kiki_ref, k_ref, v_ref, o_ref,
                 key_s, keyhi_s, qm_s, qim_s, wi_s, mb_s, bias_s, l_s, acc_s, m2_s, l2_s, lga_s, lgb_s,
                 *, tq, tk, group, n_keys, past, k_keep, idx_bits):
    q0 = pl.program_id(1) * tq
    n_pairs = N_HEADS // 2
    n_chunks = tk // LANES
    row = lax.broadcasted_iota(I32, (tq, 1), 0)
    q_lim = jnp.minimum(((past + q0 + row) // CHUNK + 1) * CHUNK, n_keys)
    q_lim_b = jnp.broadcast_to(q_lim, (tq, LANES))
    blk_lim = jnp.minimum(((past + q0 + tq - 1) // CHUNK + 1) * CHUNK, n_keys)
    nk = (blk_lim + tk - 1) // tk
    lane_q = lax.broadcasted_iota(I32, (tq, LANES), 1)
    lo_half = lane_q < HEAD_DIM
    nt = (((1,), (1,)), ((), ()))
    pair_of = lambda h: slice((h // 2) * LANES, (h // 2 + 1) * LANES)
    chunk_of = lambda c: slice(c * LANES, (c + 1) * LANES)

    kw = kw_ref[0]
    kmax = kmax_ref[0]
    halves = (slice(0, tq), slice(tq, 2 * tq))
    for pr in range(n_pairs):
        qi_pair = qi_ref[0, :, pair_of(2 * pr)]
        q_pair = q_ref[0, :, pair_of(2 * pr)]
        for half, keep in enumerate((lo_half, jnp.logical_not(lo_half))):
            h = 2 * pr + half
            qim_s[pr, halves[half]] = jnp.where(keep, qi_pair, jnp.zeros_like(qi_pair))
            qm = jnp.where(keep, q_pair, jnp.zeros_like(q_pair))
            qm_s[pr, halves[half]] = qm
            wi_s[h] = jnp.broadcast_to(kw[:, IDX_DIM + h:IDX_DIM + h + 1], (tq, LANES))
            qf = qm.astype(F32)
            q_norm = jnp.sqrt(jnp.sum(qf * qf, axis=1, keepdims=True))
            mb_s[h] = jnp.broadcast_to(q_norm * kmax[:, h:h + 1], (tq, LANES))

    def key_base(kt, c):
        return pl.multiple_of(kt * tk + c * LANES, LANES)

    def tile_start(kt):
        return pl.multiple_of(jnp.minimum(kt, nk - 1) * tk, tk)

    def two_stage_loop(produce, consume):
        produce(0, lga_s)

        def body(j, carry):
            produce(2 * j + 1, lgb_s)
            consume(2 * j, lga_s)
            produce(2 * j + 2, lga_s)
            consume(2 * j + 1, lgb_s)
            return carry

        lax.fori_loop(0, (nk + 1) // 2, body, 0)

    def produce_scores(kt, buf):
        kiki = kiki_ref[0, pl.ds(tile_start(kt), tk), :]
        for pr in range(n_pairs):
            buf[pr] = lax.dot_general(qim_s[pr], kiki, nt, preferred_element_type=F32)

    def consume_scores(kt, buf):
        ktc = jnp.minimum(kt, nk - 1)
        score = [jnp.zeros((tq, LANES), F32) for _ in range(n_chunks)]
        for pr in range(n_pairs):
            for half in range(2):
                w = wi_s[2 * pr + half]
                for c in range(n_chunks):
                    score[c] = score[c] + w * jnp.maximum(buf[pr, halves[half], chunk_of(c)], 0.0)
        for c in range(n_chunks):
            kb = key_base(ktc, c)
            bits = pltpu.bitcast(jnp.where(kb + lane_q < q_lim_b, score[c], NEG_INF), I32)
            key_s[:, pl.ds(kb, LANES)] = bits ^ ((bits >> 31) & 0x7FFFFFFF)
            keyhi_s[:, pl.ds(kb, LANES)] = pltpu.bitcast(bits & HIGH16, F32).astype(BF16)

    two_stage_loop(produce_scores, consume_scores)

    n_groups = (nk + group - 1) // group

    def fill_tile(kt, carry):
        for c in range(n_chunks):
            kb = key_base(kt, c)
            key_s[:, pl.ds(kb, LANES)] = jnp.full((tq, LANES), NEG_INF_KEY, I32)
            keyhi_s[:, pl.ds(kb, LANES)] = jnp.full((tq, LANES), NEG_INF, BF16)
        return carry

    lax.fori_loop(nk, n_groups * group, fill_tile, 0)

    def group_chunks(g_idx):
        return [pl.multiple_of(g_idx * (group * tk) + c * LANES, LANES) for c in range(group * n_chunks)]

    def count(pred):
        def body(g_idx, cnt):
            for kb in group_chunks(g_idx):
                cnt = cnt + pred(key_s[:, pl.ds(kb, LANES)], kb).astype(I32)
            return cnt
        cnt = lax.fori_loop(0, n_groups, body, jnp.zeros((tq, LANES), I32))
        return jnp.sum(cnt, axis=1, keepdims=True)

    def count_ge(c):
        cb = jnp.broadcast_to(c, (tq, LANES))
        return count(lambda key, kb: key >= cb)

    def count_ge_high(c):
        c_bits = (c ^ ((c >> 31) & 0x7FFFFFFF)) & HIGH16
        c_bits = jnp.where(jnp.logical_and(c_bits > 0, c_bits < MIN_NORMAL_BITS), MIN_NORMAL_BITS, c_bits)
        cb = jnp.broadcast_to(pltpu.bitcast(c_bits, F32).astype(BF16), (tq, LANES))
        one, nil = jnp.ones((tq, LANES), BF16), jnp.zeros((tq, LANES), BF16)

        def body(g_idx, cnt):
            for kb in group_chunks(g_idx):
                cnt = cnt + jnp.where(keyhi_s[:, pl.ds(kb, LANES)] >= cb, one, nil)
            return cnt
        cnt = lax.fori_loop(0, n_groups, body, nil)
        return jnp.sum(cnt.astype(F32), axis=1, keepdims=True).astype(I32)

    def search(counter, first_bit, n_bits, state):
        def cond(st):
            step, _, _, done = st
            return jnp.logical_and(step < n_bits // SEARCH_UNROLL, jnp.min(done) == 0)

        def body(st):
            step, t, thr, done = st
            for i in range(SEARCH_UNROLL):
                cand = t + jnp.left_shift(jnp.int32(1), first_bit - i - step * SEARCH_UNROLL)
                cnt = counter(cand)
                t = jnp.where(cnt >= k_keep, cand, t)
                newly = jnp.logical_and(cnt == k_keep, done == 0)
                thr = jnp.where(newly, jnp.maximum(cand - 1, NEG_INF_KEY), thr)
                done = jnp.where(newly, 1, done)
            return step + 1, t, thr, done

        return lax.while_loop(cond, body, (jnp.int32(0),) + state)[1:]

    zero = jnp.zeros((tq, 1), I32)
    lane_sum = lambda a: jnp.sum(a, axis=1, keepdims=True)
    high_state = search(count_ge_high, 31, 16, (jnp.full((tq, 1), INT_MIN, I32), zero, zero))
    t_high = high_state[0]

    def low_half(_):
        above = count_ge_high(t_high + 2 ** 16)
        members = count_ge_high(t_high) - above
        bucket = jnp.broadcast_to(t_high, (tq, LANES))

        def top2(g_idx, m):
            m1, m2 = m
            for kb in group_chunks(g_idx):
                key = key_s[:, pl.ds(kb, LANES)]
                v = jnp.where((key & HIGH16) == bucket, (key & 0xFFFF) + 1, 0)
                m1, m2 = jnp.maximum(m1, v), jnp.maximum(m2, jnp.minimum(m1, v))
            return m1, m2

        nil = jnp.zeros((tq, LANES), I32)
        m1, m2 = lax.fori_loop(0, n_groups, top2, (nil, nil))
        count_low = lambda pred: lane_sum(pred(m1).astype(I32) + pred(m2).astype(I32))
        caught_all = jnp.min((count_low(lambda m: m > 0) == members).astype(I32)) == 1

        def from_lanes(_):
            want_low = k_keep - above
            low = zero
            for step in range(LOW_VALUE_BITS // LOW_STEP_BITS):
                unit = 1 << (LOW_VALUE_BITS - LOW_STEP_BITS * (step + 1))
                taken = zero
                low_lanes = jnp.broadcast_to(low, (tq, LANES))
                for mult in range(1, 2 ** LOW_STEP_BITS):
                    cand = low_lanes + mult * unit
                    taken = taken + (count_low(lambda m: m >= cand) >= want_low).astype(I32)
                low = low + taken * unit
            low = jnp.maximum(low, 1)
            low_b = jnp.broadcast_to(low, (tq, LANES))
            return (t_high + low - 1, above + count_low(lambda m: m > low_b), count_low(lambda m: m == low_b),
                    high_state[1], high_state[2])

        def from_all_keys(_):
            t, thr_e, dn = search(count_ge, 15, 16, high_state)
            t = jnp.maximum(t, NEG_INF_KEY)
            greater = count_ge(t + 1)
            return t, greater, count_ge(t) - greater, thr_e, dn

        return lax.cond(caught_all, from_lanes, from_all_keys, 0)

    t_fin, cnt_gt, n_eq, thr_early, done = lax.cond(
        jnp.min(high_state[2]) == 0, low_half, lambda _: (t_high, zero, zero, high_state[1], high_state[2]), 0)
    t_fin = jnp.maximum(t_fin, NEG_INF_KEY)

    def resolve_ties(_):
        want = k_keep - cnt_gt
        tb = jnp.broadcast_to(t_fin, (tq, LANES))
        need = jnp.logical_and(jnp.logical_and(done == 0, n_eq > want), t_fin > NEG_INF_KEY)

        def cut_search(_):
            def body(j, c):
                cand = c + jnp.left_shift(jnp.int32(1), idx_bits - 1 - j)
                cb = jnp.broadcast_to(cand, (tq, LANES))
                below = count(lambda key, kb: jnp.logical_and(key == tb, kb + lane_q < cb))
                return jnp.where(below < want, cand, c)
            return lax.fori_loop(0, idx_bits, body, zero)

        cut = lax.cond(jnp.max(need.astype(I32)) > 0, cut_search, lambda _: zero, 0)
        all_ties = jnp.where(t_fin > NEG_INF_KEY, jnp.int32(2 ** 30), jnp.int32(-1))
        return jnp.where(need, cut, all_ties)

    cut_full = lax.cond(jnp.min(done) == 0, resolve_ties, lambda _: jnp.full((tq, 1), -1, I32), 0)
    thr = jnp.where(done == 1, thr_early, t_fin)
    cut = jnp.where(done == 1, -1, cut_full)
    thr_b = jnp.broadcast_to(thr, (tq, LANES))
    cut_b = jnp.broadcast_to(cut, (tq, LANES))

    def tile_bias(kt, valid):
        thr_t = jnp.where(valid, thr_b, jnp.int32(2 ** 31 - 1))
        cut_t = jnp.where(valid, cut_b, jnp.int32(-1))
        for c in range(n_chunks):
            kb = key_base(kt, c)
            key = key_s[:, pl.ds(kb, LANES)]
            sel = jnp.logical_or(key > thr_t, jnp.logical_and(key == thr_t, kb + lane_q <= cut_t))
            bias_s[:, chunk_of(c)] = jnp.where(sel, 0.0, NEG_INF)

    def emit(l_of_head):
        for pr in range(n_pairs):
            even = acc_s[pr, halves[0]] / l_of_head(2 * pr)
            odd = acc_s[pr, halves[1]] / l_of_head(2 * pr + 1)
            o_ref[0, :, pair_of(2 * pr)] = jnp.where(lo_half, even, odd).astype(o_ref.dtype)

    l_s[...] = jnp.zeros(l_s.shape, F32)
    acc_s[...] = jnp.zeros(acc_s.shape, F32)

    def produce_logits(kt, buf):
        k0 = tile_start(kt)
        for pr in range(n_pairs):
            buf[pr] = lax.dot_general(qm_s[pr], k_ref[0, pl.ds(k0, tk), pair_of(2 * pr)], nt,
                                      preferred_element_type=F32)

    def consume_logits(kt, buf):
        k0 = tile_start(kt)
        tile_bias(jnp.minimum(kt, nk - 1), kt < nk)
        for pr in range(n_pairs):
            p = []
            for half in range(2):
                h = 2 * pr + half
                shift = mb_s[h]
                l_part = l_s[h]
                p_row = []
                for c in range(n_chunks):
                    e = jnp.exp(buf[pr, halves[half], chunk_of(c)] + bias_s[:, chunk_of(c)] - shift)
                    l_part = l_part + e
                    p_row.append(e.astype(BF16))
                l_s[h] = l_part
                p.append(jnp.concatenate(p_row, axis=1))
            acc_s[pr] += jnp.dot(jnp.concatenate(p, axis=0), v_ref[0, pl.ds(k0, tk), pair_of(2 * pr)],
                                 preferred_element_type=F32)

    two_stage_loop(produce_logits, consume_logits)
    l_rows = [jnp.sum(l_s[h], axis=1, keepdims=True) for h in range(N_HEADS)]
    l_min = functools.reduce(jnp.minimum, [jnp.min(l) for l in l_rows])
    well_scaled = l_min >= SOFTMAX_SUM_FLOOR

    @pl.when(well_scaled)
    def _emit_bound():
        emit(lambda h: l_rows[h])

    @pl.when(jnp.logical_not(well_scaled))
    def _running_max():
        m2_s[...] = jnp.full(m2_s.shape, NEG_INF, F32)
        l2_s[...] = jnp.zeros(l2_s.shape, F32)
        acc_s[...] = jnp.zeros(acc_s.shape, F32)

        def online_tile(kt, carry):
            k0 = pl.multiple_of(kt * tk, tk)
            tile_bias(kt, True)
            for h in range(N_HEADS):
                rows = halves[h % 2]
                logits = lax.dot_general(qm_s[h // 2, rows], k_ref[0, pl.ds(k0, tk), pair_of(h)], nt,
                                         preferred_element_type=F32) + bias_s[...]
                m_old = m2_s[h]
                m_new = jnp.maximum(m_old, jnp.max(logits, axis=1, keepdims=True))
                m_safe = jnp.where(m_new == NEG_INF, 0.0, m_new)
                alpha = jnp.exp(m_old - m_safe)
                p = jnp.exp(logits - m_safe)
                l2_s[h] = alpha * l2_s[h] + jnp.sum(p, axis=1, keepdims=True)
                acc_s[h // 2, rows] = alpha * acc_s[h // 2, rows] + jnp.dot(
                    p.astype(BF16), v_ref[0, pl.ds(k0, tk), pair_of(h)], preferred_element_type=F32)
                m2_s[h] = m_new
            return carry

        lax.fori_loop(0, nk, online_tile, 0)
        emit(lambda h: l2_s[h])


def _pick_key_tile(l_pad, k_keep):
    for tk in (512, 384, 256, 640, 768, 896, 1024):
        if l_pad % tk == 0 and tk >= k_keep:
            return tk
    return l_pad


def _attention(q, qi, kw, kmax, kiki, k_all, v_all, *, n_keys, past, k_keep, tq):
    b, t, _ = q.shape
    l_pad = k_all.shape[1]
    tk = _pick_key_tile(l_pad, k_keep)
    assert l_pad % tk == 0 and tk >= k_keep and t % tq == 0
    assert l_pad // LANES <= 256
    idx_bits = max(1, int(math.ceil(math.log2(l_pad))))
    group = max(g for g in (4, 2, 1) if l_pad % (g * tk) == 0)
    qblk = lambda w: pl.BlockSpec((1, tq, w), lambda bi, i: (bi, i, 0))
    kvblk = lambda w: _resident((1, l_pad, w), lambda bi, i: (bi, 0, 0))
    per_head = lambda w, dt: pltpu.VMEM((N_HEADS, tq, w), dt)
    per_pair = lambda dt: pltpu.VMEM((N_HEADS // 2, 2 * tq, LANES), dt)
    return pl.pallas_call(
        functools.partial(_attn_kernel, tq=tq, tk=tk, group=group, n_keys=n_keys, past=past, k_keep=k_keep,
                          idx_bits=idx_bits),
        grid=(b, t // tq),
        in_specs=[qblk(D_ATT), qblk(D_ATT), qblk(LANES), pl.BlockSpec((1, 1, LANES), lambda bi, i: (bi, 0, 0)),
                  kvblk(LANES), kvblk(D_ATT), kvblk(D_ATT)],
        out_specs=qblk(D_ATT),
        out_shape=jax.ShapeDtypeStruct((b, t, D_ATT), BF16),
        scratch_shapes=[
            pltpu.VMEM((tq, l_pad), I32),
            pltpu.VMEM((tq, l_pad), BF16),
            per_pair(BF16),
            per_pair(BF16),
            per_head(LANES, F32),
            per_head(LANES, F32),
            pltpu.VMEM((tq, tk), F32),
            per_head(LANES, F32),
            per_pair(F32),
            per_head(1, F32),
            per_head(1, F32),
            pltpu.VMEM((N_HEADS // 2, 2 * tq, tk), F32),
            pltpu.VMEM((N_HEADS // 2, 2 * tq, tk), F32),
        ],
        compiler_params=_params("arbitrary", "arbitrary"),
    )(q, qi, kw, kmax, kiki, k_all, v_all)


def _cmul(ar, ai, br, bi):
    return ar * br - ai * bi, ar * bi + ai * br


def _s5_kernel(u_ref, h0re_ref, h0im_ref, are_ref, aim_ref, ldt_ref, bre_ref, bim_ref, cre_ref, cim_ref,
               dskip_ref, wglu_ref, bglu_ref, y_ref, hre_ref, him_ref,
               bbar_s, lvl_s, pw_s, h_s, *, ts, lane_chunk):
    n = N_STATE
    t_idx = pl.program_id(1)
    first = jnp.logical_and(pl.program_id(0) == 0, t_idx == 0)

    @pl.when(first)
    def _discretise():
        dt = jnp.exp(ldt_ref[...])
        ar, ai = are_ref[...], aim_ref[...]
        mag = jnp.exp(dt * ar)
        abr, abi = mag * jnp.cos(dt * ai), mag * jnp.sin(dt * ai)
        den = ar * ar + ai * ai
        nr, ni = abr - 1.0, abi
        f_re, f_im = (nr * ar + ni * ai) / den, (ni * ar - nr * ai) / den
        reps = D_SSM // SSM_GROUP
        bre = jnp.concatenate([bre_ref[...]] * reps, axis=0)
        bim = jnp.concatenate([bim_ref[...]] * reps, axis=0)
        r_grp = lax.broadcasted_iota(I32, (D_SSM, n), 0) // SSM_GROUP
        c_grp = lax.broadcasted_iota(I32, (D_SSM, n), 1) // SSM_STATE
        same = r_grp == c_grp
        bbar_s[:, :n] = jnp.where(same, f_re * bre - f_im * bim, 0.0).astype(BF16)
        bbar_s[:, n:] = jnp.where(same, f_re * bim + f_im * bre, 0.0).astype(BF16)
        a1 = (abr, abi)
        a2 = _cmul(*a1, *a1)
        a3 = _cmul(*a2, *a1)
        a4 = _cmul(*a2, *a2)
        a5 = _cmul(*a4, *a1)
        a6 = _cmul(*a4, *a2)
        a7 = _cmul(*a4, *a3)
        a8 = _cmul(*a4, *a4)
        rows = lax.broadcasted_iota(I32, (SUBLANES, n), 0)
        for part in range(2):
            pw = jnp.zeros((SUBLANES, n), F32)
            for r, a in enumerate((a1, a2, a3, a4, a5, a6, a7, a8)):
                pw = jnp.where(rows == r, a[part], pw)
            pw_s[part] = pw
            for lv, (a, dist) in enumerate(((a1, 1), (a2, 2), (a4, 4))):
                lvl_s[lv, part] = jnp.where(rows >= dist, a[part], 0.0)

    @pl.when(t_idx == 0)
    def _load_state():
        h_s[0:SUBLANES, :] = jnp.zeros((SUBLANES, 2 * n), F32)
        h_s[SUBLANES - 1:SUBLANES, :n] = h0re_ref[0]
        h_s[SUBLANES - 1:SUBLANES, n:] = h0im_ref[0]

    u = u_ref[0]
    u_bf = u.astype(BF16)
    for c in range(2 * n // 512):
        h_s[SUBLANES:, c * 512:(c + 1) * 512] = jnp.dot(u_bf, bbar_s[:, c * 512:(c + 1) * 512],
                                                        preferred_element_type=F32)

    def tile_step(j, carry):
        r0 = pl.multiple_of(j * SUBLANES, SUBLANES)
        for c in range(n // lane_chunk):
            re_sl = slice(c * lane_chunk, (c + 1) * lane_chunk)
            im_sl = slice(n + c * lane_chunk, n + (c + 1) * lane_chunk)
            xr = h_s[pl.ds(r0 + SUBLANES, SUBLANES), re_sl]
            xi = h_s[pl.ds(r0 + SUBLANES, SUBLANES), im_sl]
            for lv, dist in enumerate((1, 2, 4)):
                sr, si = pltpu.roll(xr, dist, 0), pltpu.roll(xi, dist, 0)
                dr, di = _cmul(lvl_s[lv, 0, :, re_sl], lvl_s[lv, 1, :, re_sl], sr, si)
                xr, xi = xr + dr, xi + di
            prev_r = h_s[pl.ds(r0, SUBLANES), re_sl][SUBLANES - 1:SUBLANES, :]
            prev_i = h_s[pl.ds(r0, SUBLANES), im_sl][SUBLANES - 1:SUBLANES, :]
            cr, ci = _cmul(pw_s[0, :, re_sl], pw_s[1, :, re_sl], prev_r, prev_i)
            h_s[pl.ds(r0 + SUBLANES, SUBLANES), re_sl] = xr + cr
            h_s[pl.ds(r0 + SUBLANES, SUBLANES), im_sl] = xi + ci
        return carry

    lax.fori_loop(0, ts // SUBLANES, tile_step, 0)

    nt = (((1,), (1,)), ((), ()))
    h_re = h_s[SUBLANES:, :n].astype(BF16)
    h_im = h_s[SUBLANES:, n:].astype(BF16)
    y = (lax.dot_general(h_re, cre_ref[...], nt, preferred_element_type=F32)
         - lax.dot_general(h_im, cim_ref[...], nt, preferred_element_type=F32))
    y = y + dskip_ref[...] * u
    y = jax.nn.gelu(y)
    gate = jax.nn.sigmoid(jnp.dot(y.astype(BF16), wglu_ref[...], preferred_element_type=F32) + bglu_ref[...])
    y_ref[0] = (y * gate).astype(y_ref.dtype)

    last = h_s[ts:ts + SUBLANES, :]
    h_s[0:SUBLANES, :] = last

    @pl.when(t_idx == pl.num_programs(1) - 1)
    def _emit_state():
        hre_ref[0] = last[SUBLANES - 1:SUBLANES, :n]
        him_ref[0] = last[SUBLANES - 1:SUBLANES, n:]


def _s5(u, h0_re, h0_im, prm, ts):
    b, t, _ = u.shape
    n = N_STATE
    a_re, a_im, ldt, bre_d, bim_d, cre_t, cim_t, d_skip, w_glu, b_glu = prm
    const = lambda a: _resident(a.shape, lambda bi, i: (0,) * a.ndim)
    st_blk = pl.BlockSpec((1, 1, n), lambda bi, i: (bi, 0, 0))
    y, hre, him = pl.pallas_call(
        functools.partial(_s5_kernel, ts=ts, lane_chunk=512),
        grid=(b, t // ts),
        in_specs=[pl.BlockSpec((1, ts, D_SSM), lambda bi, i: (bi, i, 0)), st_blk, st_blk,
                  const(a_re), const(a_im), const(ldt), const(bre_d), const(bim_d), const(cre_t), const(cim_t),
                  const(d_skip), const(w_glu), const(b_glu)],
        out_specs=[pl.BlockSpec((1, ts, D_SSM), lambda bi, i: (bi, i, 0)), st_blk, st_blk],
        out_shape=[jax.ShapeDtypeStruct((b, t, D_SSM), BF16),
                   jax.ShapeDtypeStruct((b, 1, n), F32), jax.ShapeDtypeStruct((b, 1, n), F32)],
        scratch_shapes=[
            pltpu.VMEM((D_SSM, 2 * n), BF16),
            pltpu.VMEM((3, 2, SUBLANES, n), F32),
            pltpu.VMEM((2, SUBLANES, n), F32),
            pltpu.VMEM((SUBLANES + ts, 2 * n), F32),
        ],
        compiler_params=_params("arbitrary", "arbitrary"),
    )(u, h0_re, h0_im, a_re, a_im, ldt, bre_d, bim_d, cre_t, cim_t, d_skip, w_glu, b_glu)
    return y, hre, him


def _layer_norm(z, g, b):
    mu = jnp.mean(z, axis=-1, keepdims=True)
    zc = z - mu
    var = jnp.mean(zc * zc, axis=-1, keepdims=True)
    return zc * lax.rsqrt(var + LN_EPS) * g + b


def _merge_kernel(x_ref, att_ref, ssm_ref, wg_ref, bg_ref, wpa_ref, wpb_ref, wo_ref, g_ref, b_ref, o_ref, *, alpha):
    x = x_ref[...]
    d = x.shape[1]
    gates = jax.nn.sigmoid(jnp.dot(x.astype(BF16), wg_ref[...], preferred_element_type=F32) + bg_ref[...])
    mix = (gates[:, :d] * jnp.dot(att_ref[...], wpa_ref[...], preferred_element_type=F32)
           + gates[:, d:] * jnp.dot(ssm_ref[...], wpb_ref[...], preferred_element_type=F32))
    z = alpha * x + jnp.dot(mix.astype(BF16), wo_ref[...], preferred_element_type=F32)
    o_ref[...] = _layer_norm(z, g_ref[...], b_ref[...])


def _merge(x2d, att2d, ssm2d, prm, alpha, tm):
    rows, d = x2d.shape
    w_g, b_g, w_pa, w_pb, w_o, ln_g, ln_b = prm
    row_blk = lambda w: pl.BlockSpec((tm, w), lambda i: (i, 0))
    const = lambda a: _resident(a.shape, lambda i: (0, 0))
    return pl.pallas_call(
        functools.partial(_merge_kernel, alpha=alpha),
        grid=(rows // tm,),
        in_specs=[row_blk(d), row_blk(D_ATT), row_blk(D_SSM)] + [const(a) for a in prm],
        out_specs=row_blk(d),
        out_shape=jax.ShapeDtypeStruct((rows, d), F32),
        compiler_params=_params("parallel"),
    )(x2d, att2d, ssm2d, *prm)


def _ffn_kernel(x_ref, cbuf_ref, wup_ref, cw_ref, cb_ref, wdn_ref, g_ref, b_ref, o_ref, nconv_ref,
                h_s, acc_s, *, tm, d_ff, fc, alpha):
    t_idx = pl.program_id(1)
    hist = CONV_W - 1

    @pl.when(t_idx == 0)
    def _load_history():
        h_s[0:SUBLANES, :] = jnp.zeros((SUBLANES, h_s.shape[1]), F32)
        h_s[SUBLANES - hist:SUBLANES, :] = cbuf_ref[0]

    x = x_ref[0]
    x_bf = x.astype(BF16)
    n_chunks = d_ff // fc
    column_pair = lambda c: (slice(c * fc, (c + 1) * fc), slice(d_ff + c * fc, d_ff + (c + 1) * fc))

    def up_project(c):
        for cols in column_pair(c):
            h_s[SUBLANES:, cols] = jnp.dot(x_bf, wup_ref[:, cols], preferred_element_type=F32)

    def conv_act_down(c):
        conv = [cb_ref[:, cols] + sum(cw_ref[j:j + 1, cols] * h_s[SUBLANES - hist + j:SUBLANES - hist + j + tm, cols]
                                      for j in range(CONV_W)) for cols in column_pair(c)]
        act = (jax.nn.gelu(conv[0]) * conv[1]).astype(BF16)
        part = jnp.dot(act, wdn_ref[c * fc:(c + 1) * fc, :], preferred_element_type=F32)
        if c == 0:
            acc_s[...] = part
        else:
            acc_s[...] += part

    up_project(0)
    for c in range(n_chunks):
        if c + 1 < n_chunks:
            up_project(c + 1)
        conv_act_down(c)
    o_ref[0] = _layer_norm(alpha * x + acc_s[...], g_ref[...], b_ref[...])

    tail = h_s[tm:tm + SUBLANES, :]
    h_s[0:SUBLANES, :] = tail

    @pl.when(t_idx == pl.num_programs(1) - 1)
    def _emit_history():
        nconv_ref[0] = tail[SUBLANES - hist:, :]


def _ffn(x, conv_buf, prm, alpha, tm):
    b, t, d = x.shape
    w_up, conv_w, conv_b, w_dn, ln_g, ln_b = prm
    d_ff = w_dn.shape[0]
    fc = 256
    assert d_ff % fc == 0 and t % tm == 0 and tm % SUBLANES == 0
    const = lambda a: _resident(a.shape, lambda bi, i: (0, 0))
    hist_blk = pl.BlockSpec((1, CONV_W - 1, 2 * d_ff), lambda bi, i: (bi, 0, 0))
    return pl.pallas_call(
        functools.partial(_ffn_kernel, tm=tm, d_ff=d_ff, fc=fc, alpha=alpha),
        grid=(b, t // tm),
        in_specs=[pl.BlockSpec((1, tm, d), lambda bi, i: (bi, i, 0)), hist_blk] + [const(a) for a in prm],
        out_specs=[pl.BlockSpec((1, tm, d), lambda bi, i: (bi, i, 0)), hist_blk],
        out_shape=[jax.ShapeDtypeStruct((b, t, d), F32), jax.ShapeDtypeStruct((b, CONV_W - 1, 2 * d_ff), F32)],
        scratch_shapes=[pltpu.VMEM((SUBLANES + tm, 2 * d_ff), F32),
                        pltpu.VMEM((tm, d), F32)],
        compiler_params=_params("arbitrary", "arbitrary"),
    )(x, conv_buf, *prm)


def _rope_tables(pos):
    half = HEAD_DIM // 2
    inv = ROPE_THETA ** (-jnp.arange(half, dtype=F32) / half)
    ang = pos.astype(F32)[:, None] * inv[None, :]
    cos, sin = jnp.cos(ang), jnp.sin(ang)
    cos_tab = jnp.tile(cos, (1, LANES // half))
    sin_tab = jnp.tile(jnp.concatenate([-sin, sin], axis=1), (1, LANES // HEAD_DIM))
    return cos_tab, sin_tab


def _layer_weights(w_in, b_gate, w_pa, w_pb, w_o, a_re, a_im, log_dt, b_re, b_im, c_re, c_im, d_skip, w_glu,
                   b_glu, ln1_g, ln1_b, w_up, conv_w, conv_b, w_down, ln2_g, ln2_b):
    d = w_in.shape[0]
    o = 0
    pieces = []
    for w in (D_ATT, D_ATT, D_ATT, N_IDX_HEADS * IDX_DIM, IDX_DIM + N_IDX_HEADS, D_SSM, 2 * d):
        pieces.append(w_in[:, o:o + w])
        o += w
    wq, wk, wv, wqi, wkw, wu, wg = pieces
    wkw = jnp.pad(wkw, ((0, 0), (0, LANES - wkw.shape[1])))
    inproj = tuple(w.astype(BF16) for w in (wq, wk, wv, wqi, wkw, wu))
    n = N_STATE
    row = lambda a: a.reshape(1, -1).astype(F32)
    same_group = jnp.eye(N_GROUPS, dtype=bool)[:, None, :, None]
    blockdiag_t = lambda c: jnp.where(same_group, c[:, :, None, :], 0.0).reshape(D_SSM, n).astype(BF16)
    dense = lambda bm: jnp.transpose(bm, (2, 0, 1)).reshape(SSM_GROUP, n).astype(F32)
    s5 = (row(a_re), row(a_im), row(jnp.repeat(log_dt, SSM_STATE)), dense(b_re), dense(b_im),
          blockdiag_t(c_re), blockdiag_t(c_im), row(d_skip), w_glu.astype(BF16), row(b_glu))
    merge = (wg.astype(BF16), row(b_gate), w_pa.astype(BF16), w_pb.astype(BF16), w_o.astype(BF16),
             row(ln1_g), row(ln1_b))
    ffn = (w_up.astype(BF16), conv_w.astype(F32), row(conv_b), w_down.astype(BF16), row(ln2_g), row(ln2_b))
    return inproj, s5, merge, ffn


def _largest_tile(n, cap):
    t = min(n, cap)
    while n % t:
        t -= SUBLANES
    return t


def _trunk_layer(x, past_k, past_v, past_ki, h0_re, h0_im, conv_buf, weights, alpha):
    bn, t, d = x.shape
    p = 0 if past_k is None else past_k.shape[1]
    n_keys = p + t
    k_keep = min(TOPK_MAX, n_keys // 4)
    inproj_w, s5_w, merge_w, ffn_w = weights
    rows = bn * t
    tm = _largest_tile(rows, 512)

    cos_tab, sin_tab = _rope_tables(p + jnp.arange(t))
    if t % tm == 0:
        n_tab_blocks = t // tm
    else:
        cos_tab, sin_tab = jnp.tile(cos_tab, (bn, 1)), jnp.tile(sin_tab, (bn, 1))
        n_tab_blocks = rows // tm
    q, k_f, k_b, v_f, v_b, qi, kw, u = _inproj(x.reshape(rows, d), cos_tab, sin_tab, inproj_w, tm, n_tab_blocks)

    r3 = lambda a: a.reshape(bn, t, a.shape[-1])
    ki_b = r3(kw)[:, :, :IDX_DIM].astype(BF16)
    k_all, v_all = r3(k_b), r3(v_b)
    if p:
        k_all = jnp.concatenate([past_k.reshape(bn, p, D_ATT).astype(BF16), k_all], axis=1)
        v_all = jnp.concatenate([past_v.reshape(bn, p, D_ATT).astype(BF16), v_all], axis=1)
        ki_b = jnp.concatenate([past_ki.astype(BF16), ki_b], axis=1)
    l_pad = -(-n_keys // LANES) * LANES
    l_pad = max(l_pad, 2 * LANES)
    pad = lambda a: jnp.pad(a, ((0, 0), (0, l_pad - n_keys), (0, 0)))
    kiki = jnp.concatenate([ki_b, ki_b], axis=-1)
    k_heads = k_all.astype(F32).reshape(bn, n_keys, N_HEADS, HEAD_DIM)
    kmax = jnp.sqrt(jnp.max(jnp.sum(k_heads * k_heads, axis=-1), axis=1))
    kmax = jnp.pad(kmax, ((0, 0), (0, LANES - N_HEADS))).reshape(bn, 1, LANES)
    tq = _largest_tile(t, 128)
    att = _attention(r3(q), r3(qi), r3(kw), kmax, pad(kiki), pad(k_all), pad(v_all),
                     n_keys=n_keys, past=p, k_keep=k_keep, tq=tq)

    ts = _largest_tile(t, 256)
    ssm, h_re, h_im = _s5(r3(u), h0_re.reshape(bn, 1, N_STATE), h0_im.reshape(bn, 1, N_STATE), s5_w, ts)

    x1 = _merge(x.reshape(rows, d), att.reshape(rows, D_ATT), ssm.reshape(rows, D_SSM), merge_w, alpha, tm)
    x2, new_conv = _ffn(x1.reshape(bn, t, d), conv_buf, ffn_w, alpha, _largest_tile(t, 256))

    k_out = r3(k_f).reshape(bn, t, N_HEADS, HEAD_DIM)
    v_out = r3(v_f).reshape(bn, t, N_HEADS, HEAD_DIM)
    ki_out = r3(kw)[:, :, :IDX_DIM]
    return (x2, k_out, v_out, ki_out, h_re.reshape(bn, N_GROUPS, SSM_STATE), h_im.reshape(bn, N_GROUPS, SSM_STATE),
            new_conv)


def kernel(x_prompt, x_sample, cache_k, cache_v, cache_idx_k, state_ssm_re, state_ssm_im, state_conv, w_in, b_gate, w_pa, w_pb, w_o, a_re, a_im, log_dt, b_re, b_im, c_re, c_im, d_skip, w_glu, b_glu, ln1_g, ln1_b, w_up, conv_w, conv_b, w_down, ln2_g, ln2_b):
    depth = w_in.shape[0]
    alpha = (2 * depth) ** 0.25
    bp = x_prompt.shape[0]
    d_ff2 = w_up.shape[2]
    zero_h = jnp.zeros((bp, N_GROUPS, SSM_STATE), F32)
    zero_conv = jnp.zeros((bp, CONV_W - 1, d_ff2), F32)
    layer_params = (w_in, b_gate, w_pa, w_pb, w_o, a_re, a_im, log_dt, b_re, b_im, c_re, c_im, d_skip, w_glu, b_glu,
                    ln1_g, ln1_b, w_up, conv_w, conv_b, w_down, ln2_g, ln2_b)
    xp, xs = x_prompt, x_sample
    outs_p, outs_s = [], []
    for l in range(depth):
        weights = _layer_weights(*(a[l] for a in layer_params))
        rp = _trunk_layer(xp, None, None, None, zero_h, zero_h, zero_conv, weights, alpha)
        rs = _trunk_layer(xs, cache_k[l], cache_v[l], cache_idx_k[l], state_ssm_re[l].astype(F32),
                          state_ssm_im[l].astype(F32), state_conv[l], weights, alpha)
        xp, xs = rp[0], rs[0]
        outs_p.append(rp[1:])
        outs_s.append(rs[1:])
    stack = lambda outs: tuple(jnp.stack([o[j] for o in outs], axis=0) for j in range(6))
    return (xp, xs) + stack(outs_p) + stack(outs_s)
```

```python
import functools
import math

import jax
import jax.numpy as jnp
from jax import lax
from jax.experimental import pallas as pl
from jax.experimental.pallas import tpu as pltpu

CHUNK = 64
N_HEADS = 8
HEAD_DIM = 64
D_ATT = N_HEADS * HEAD_DIM
N_IDX_HEADS = 8
IDX_DIM = 64
TOPK_MAX = 256
D_SSM = 512
SSM_GROUP = 16
N_GROUPS = D_SSM // SSM_GROUP
SSM_STATE = 64
N_STATE = N_GROUPS * SSM_STATE
CONV_W = 3
ROPE_THETA = 10000.0
LN_EPS = 1e-5

LANES = 128
SUBLANES = 8
VMEM_LIMIT = 60 * 1024 * 1024

F32 = jnp.float32
BF16 = jnp.bfloat16
I32 = jnp.int32
NEG_INF = float("-inf")
INT_MIN = -(2 ** 31)
HIGH16 = -(2 ** 16)
MIN_NORMAL_BITS = 0x00800000


def _f32_key_of_neg_inf():
    bits = 0xFF800000
    signed = bits - (1 << 32)
    return signed ^ ((signed >> 31) & 0x7FFFFFFF)


NEG_INF_KEY = _f32_key_of_neg_inf()
SEARCH_UNROLL = 4
LOW_VALUE_BITS = 18
LANE_KEEP = 3
LOW_STEP_BITS = 3
SOFTMAX_SUM_FLOOR = 1e-30


def _resident(block_shape, index_map):
    return pl.BlockSpec(block_shape, index_map, pipeline_mode=pl.Buffered(1))


def _params(*sem):
    return pltpu.CompilerParams(dimension_semantics=sem, vmem_limit_bytes=VMEM_LIMIT)


def _inproj_kernel(x_ref, cos_ref, sin_ref, wq_ref, wk_ref, wv_ref, wqi_ref, wkw_ref, wu_ref,
                   q_ref, kf_ref, kb_ref, vf_ref, vb_ref, qi_ref, kw_ref, u_ref, *, wi_scale):
    x = x_ref[...].astype(BF16)
    cos = cos_ref[...]
    sin = sin_ref[...]
    tm = x.shape[0]
    lane = lax.broadcasted_iota(I32, (tm, LANES), 1)
    first_half = (lane % HEAD_DIM) < (HEAD_DIM // 2)

    def proj(w_ref):
        return jnp.dot(x, w_ref[...], preferred_element_type=F32)

    def rope_group(y):
        rot = jnp.where(first_half, pltpu.roll(y, LANES - HEAD_DIM // 2, 1), pltpu.roll(y, HEAD_DIM // 2, 1))
        return y * cos + rot * sin

    def rope(y):
        return jnp.concatenate([rope_group(y[:, c * LANES:(c + 1) * LANES]) for c in range(y.shape[1] // LANES)],
                               axis=1)

    q = rope(proj(wq_ref))
    q_ref[...] = (q * (HEAD_DIM ** -0.5)).astype(BF16)
    k = rope(proj(wk_ref))
    kf_ref[...] = k
    kb_ref[...] = k.astype(BF16)
    v = proj(wv_ref)
    vf_ref[...] = v
    vb_ref[...] = v.astype(BF16)
    qi_ref[...] = rope(proj(wqi_ref)).astype(BF16)
    kw = proj(wkw_ref)
    kw_ref[...] = jnp.where(lane < IDX_DIM, rope_group(kw), kw * wi_scale)
    u_ref[...] = proj(wu_ref)


def _inproj(x2d, cos_tab, sin_tab, ws, tm, n_tab_blocks):
    rows, d = x2d.shape
    wq, wk, wv, wqi, wkw, wu = ws
    row_blk = lambda w: pl.BlockSpec((tm, w), lambda i: (i, 0))
    tab_blk = pl.BlockSpec((tm, LANES), lambda i: (i % n_tab_blocks, 0))
    w_blk = lambda w: _resident(w.shape, lambda i: (0, 0))
    out_shapes = [
        jax.ShapeDtypeStruct((rows, D_ATT), BF16),
        jax.ShapeDtypeStruct((rows, D_ATT), F32),
        jax.ShapeDtypeStruct((rows, D_ATT), BF16),
        jax.ShapeDtypeStruct((rows, D_ATT), F32),
        jax.ShapeDtypeStruct((rows, D_ATT), BF16),
        jax.ShapeDtypeStruct((rows, D_ATT), BF16),
        jax.ShapeDtypeStruct((rows, LANES), F32),
        jax.ShapeDtypeStruct((rows, D_SSM), F32),
    ]
    wi_scale = (N_IDX_HEADS ** -0.5) * (IDX_DIM ** -0.5)
    return pl.pallas_call(
        functools.partial(_inproj_kernel, wi_scale=wi_scale),
        grid=(rows // tm,),
        in_specs=[row_blk(d), tab_blk, tab_blk, w_blk(wq), w_blk(wk), w_blk(wv), w_blk(wqi), w_blk(wkw), w_blk(wu)],
        out_specs=[row_blk(s.shape[1]) for s in out_shapes],
        out_shape=out_shapes,
        compiler_params=_params("parallel"),
    )(x2d, cos_tab, sin_tab, wq, wk, wv, wqi, wkw, wu)


def _attn_kernel(q_ref, qi_ref, kw_ref, kmax_ref, kiki_ref, k_ref, v_ref, o_ref,
                 key_s, keyhi_s, qm_s, qim_s, wi_s, mb_s, bias_s, l_s, acc_s, m2_s, l2_s, lga_s, lgb_s,
                 *, tq, tk, group, n_keys, past, k_keep, idx_bits):
    q0 = pl.program_id(1) * tq
    n_pairs = N_HEADS // 2
    n_chunks = tk // LANES
    row = lax.broadcasted_iota(I32, (tq, 1), 0)
    q_lim = jnp.minimum(((past + q0 + row) // CHUNK + 1) * CHUNK, n_keys)
    q_lim_b = jnp.broadcast_to(q_lim, (tq, LANES))
    blk_lim = jnp.minimum(((past + q0 + tq - 1) // CHUNK + 1) * CHUNK, n_keys)
    nk = (blk_lim + tk - 1) // tk
    lane_q = lax.broadcasted_iota(I32, (tq, LANES), 1)
    lo_half = lane_q < HEAD_DIM
    nt = (((1,), (1,)), ((), ()))
    pair_of = lambda h: slice((h // 2) * LANES, (h // 2 + 1) * LANES)
    chunk_of = lambda c: slice(c * LANES, (c + 1) * LANES)

    kw = kw_ref[0]
    kmax = kmax_ref[0]
    halves = (slice(0, tq), slice(tq, 2 * tq))
    for pr in range(n_pairs):
        qi_pair = qi_ref[0, :, pair_of(2 * pr)]
        q_pair = q_ref[0, :, pair_of(2 * pr)]
        for half, keep in enumerate((lo_half, jnp.logical_not(lo_half))):
            h = 2 * pr + half
            qim_s[pr, halves[half]] = jnp.where(keep, qi_pair, jnp.zeros_like(qi_pair))
            qm = jnp.where(keep, q_pair, jnp.zeros_like(q_pair))
            qm_s[pr, halves[half]] = qm
            wi_s[h] = jnp.broadcast_to(kw[:, IDX_DIM + h:IDX_DIM + h + 1], (tq, LANES))
            qf = qm.astype(F32)
            q_norm = jnp.sqrt(jnp.sum(qf * qf, axis=1, keepdims=True))
            mb_s[h] = jnp.broadcast_to(q_norm * kmax[:, h:h + 1], (tq, LANES))

    def key_base(kt, c):
        return pl.multiple_of(kt * tk + c * LANES, LANES)

    def tile_start(kt):
        return pl.multiple_of(jnp.minimum(kt, nk - 1) * tk, tk)

    def two_stage_loop(produce, consume):
        produce(0, lga_s)

        def body(j, carry):
            produce(2 * j + 1, lgb_s)
            consume(2 * j, lga_s)
            produce(2 * j + 2, lga_s)
            consume(2 * j + 1, lgb_s)
            return carry

        lax.fori_loop(0, (nk + 1) // 2, body, 0)

    def produce_scores(kt, buf):
        kiki = kiki_ref[0, pl.ds(tile_start(kt), tk), :]
        for pr in range(n_pairs):
            buf[pr] = lax.dot_general(qim_s[pr], kiki, nt, preferred_element_type=F32)

    def consume_scores(kt, buf):
        ktc = jnp.minimum(kt, nk - 1)
        score = [jnp.zeros((tq, LANES), F32) for _ in range(n_chunks)]
        for pr in range(n_pairs):
            for half in range(2):
                w = wi_s[2 * pr + half]
                for c in range(n_chunks):
                    score[c] = score[c] + w * jnp.maximum(buf[pr, halves[half], chunk_of(c)], 0.0)
        for c in range(n_chunks):
            kb = key_base(ktc, c)
            bits = pltpu.bitcast(jnp.where(kb + lane_q < q_lim_b, score[c], NEG_INF), I32)
            key_s[:, pl.ds(kb, LANES)] = bits ^ ((bits >> 31) & 0x7FFFFFFF)
            keyhi_s[:, pl.ds(kb, LANES)] = pltpu.bitcast(bits & HIGH16, F32).astype(BF16)

    two_stage_loop(produce_scores, consume_scores)

    n_groups = (nk + group - 1) // group

    def fill_tile(kt, carry):
        for c in range(n_chunks):
            kb = key_base(kt, c)
            key_s[:, pl.ds(kb, LANES)] = jnp.full((tq, LANES), NEG_INF_KEY, I32)
            keyhi_s[:, pl.ds(kb, LANES)] = jnp.full((tq, LANES), NEG_INF, BF16)
        return carry

    lax.fori_loop(nk, n_groups * group, fill_tile, 0)

    def group_chunks(g_idx):
        return [pl.multiple_of(g_idx * (group * tk) + c * LANES, LANES) for c in range(group * n_chunks)]

    def count(pred):
        def body(g_idx, cnt):
            for kb in group_chunks(g_idx):
                cnt = cnt + pred(key_s[:, pl.ds(kb, LANES)], kb).astype(I32)
            return cnt
        cnt = lax.fori_loop(0, n_groups, body, jnp.zeros((tq, LANES), I32))
        return jnp.sum(cnt, axis=1, keepdims=True)

    def count_ge(c):
        cb = jnp.broadcast_to(c, (tq, LANES))
        return count(lambda key, kb: key >= cb)

    def count_ge_high(c):
        c_bits = (c ^ ((c >> 31) & 0x7FFFFFFF)) & HIGH16
        c_bits = jnp.where(jnp.logical_and(c_bits > 0, c_bits < MIN_NORMAL_BITS), MIN_NORMAL_BITS, c_bits)
        cb = jnp.broadcast_to(pltpu.bitcast(c_bits, F32).astype(BF16), (tq, LANES))
        one, nil = jnp.ones((tq, LANES), BF16), jnp.zeros((tq, LANES), BF16)

        def body(g_idx, cnt):
            for kb in group_chunks(g_idx):
                cnt = cnt + jnp.where(keyhi_s[:, pl.ds(kb, LANES)] >= cb, one, nil)
            return cnt
        cnt = lax.fori_loop(0, n_groups, body, nil)
        return jnp.sum(cnt.astype(F32), axis=1, keepdims=True).astype(I32)

    def search(counter, first_bit, n_bits, state):
        def cond(st):
            step, _, _, done = st
            return jnp.logical_and(step < n_bits // SEARCH_UNROLL, jnp.min(done) == 0)

        def body(st):
            step, t, thr, done = st
            for i in range(SEARCH_UNROLL):
                cand = t + jnp.left_shift(jnp.int32(1), first_bit - i - step * SEARCH_UNROLL)
                cnt = counter(cand)
                t = jnp.where(cnt >= k_keep, cand, t)
                newly = jnp.logical_and(cnt == k_keep, done == 0)
                thr = jnp.where(newly, jnp.maximum(cand - 1, NEG_INF_KEY), thr)
                done = jnp.where(newly, 1, done)
            return step + 1, t, thr, done

        return lax.while_loop(cond, body, (jnp.int32(0),) + state)[1:]

    zero = jnp.zeros((tq, 1), I32)
    lane_sum = lambda a: jnp.sum(a, axis=1, keepdims=True)
    high_state = search(count_ge_high, 31, 16, (jnp.full((tq, 1), INT_MIN, I32), zero, zero))
    t_high = high_state[0]

    def low_half(_):
        above = count_ge_high(t_high + 2 ** 16)
        members = count_ge_high(t_high) - above
        bucket = jnp.broadcast_to(t_high, (tq, LANES))

        def keep_largest(g_idx, kept):
            kept = list(kept)
            for kb in group_chunks(g_idx):
                key = key_s[:, pl.ds(kb, LANES)]
                v = jnp.where((key & HIGH16) == bucket, (key & 0xFFFF) + 1, 0)
                for i in range(LANE_KEEP):
                    kept[i], v = jnp.maximum(kept[i], v), jnp.minimum(kept[i], v)
            return tuple(kept)

        nil = jnp.zeros((tq, LANES), I32)
        kept = lax.fori_loop(0, n_groups, keep_largest, (nil,) * LANE_KEEP)
        count_low = lambda pred: lane_sum(functools.reduce(jnp.add, [pred(m).astype(I32) for m in kept]))
        caught_all = jnp.min((count_low(lambda m: m > 0) == members).astype(I32)) == 1

        def from_lanes(_):
            want_low = k_keep - above
            low = zero
            for step in range(LOW_VALUE_BITS // LOW_STEP_BITS):
                unit = 1 << (LOW_VALUE_BITS - LOW_STEP_BITS * (step + 1))
                taken = zero
                low_lanes = jnp.broadcast_to(low, (tq, LANES))
                for mult in range(1, 2 ** LOW_STEP_BITS):
                    cand = low_lanes + mult * unit
                    taken = taken + (count_low(lambda m: m >= cand) >= want_low).astype(I32)
                low = low + taken * unit
            low = jnp.maximum(low, 1)
            low_b = jnp.broadcast_to(low, (tq, LANES))
            return (t_high + low - 1, above + count_low(lambda m: m > low_b), count_low(lambda m: m == low_b),
                    high_state[1], high_state[2])

        def from_all_keys(_):
            t, thr_e, dn = search(count_ge, 15, 16, high_state)
            t = jnp.maximum(t, NEG_INF_KEY)
            greater = count_ge(t + 1)
            return t, greater, count_ge(t) - greater, thr_e, dn

        return lax.cond(caught_all, from_lanes, from_all_keys, 0)

    t_fin, cnt_gt, n_eq, thr_early, done = lax.cond(
        jnp.min(high_state[2]) == 0, low_half, lambda _: (t_high, zero, zero, high_state[1], high_state[2]), 0)
    t_fin = jnp.maximum(t_fin, NEG_INF_KEY)

    def resolve_ties(_):
        want = k_keep - cnt_gt
        tb = jnp.broadcast_to(t_fin, (tq, LANES))
        need = jnp.logical_and(jnp.logical_and(done == 0, n_eq > want), t_fin > NEG_INF_KEY)

        def cut_search(_):
            def body(j, c):
                cand = c + jnp.left_shift(jnp.int32(1), idx_bits - 1 - j)
                cb = jnp.broadcast_to(cand, (tq, LANES))
                below = count(lambda key, kb: jnp.logical_and(key == tb, kb + lane_q < cb))
                return jnp.where(below < want, cand, c)
            return lax.fori_loop(0, idx_bits, body, zero)

        cut = lax.cond(jnp.max(need.astype(I32)) > 0, cut_search, lambda _: zero, 0)
        all_ties = jnp.where(t_fin > NEG_INF_KEY, jnp.int32(2 ** 30), jnp.int32(-1))
        return jnp.where(need, cut, all_ties)

    cut_full = lax.cond(jnp.min(done) == 0, resolve_ties, lambda _: jnp.full((tq, 1), -1, I32), 0)
    thr = jnp.where(done == 1, thr_early, t_fin)
    cut = jnp.where(done == 1, -1, cut_full)
    thr_b = jnp.broadcast_to(thr, (tq, LANES))
    cut_b = jnp.broadcast_to(cut, (tq, LANES))

    def tile_bias(kt, valid):
        thr_t = jnp.where(valid, thr_b, jnp.int32(2 ** 31 - 1))
        cut_t = jnp.where(valid, cut_b, jnp.int32(-1))
        for c in range(n_chunks):
            kb = key_base(kt, c)
            key = key_s[:, pl.ds(kb, LANES)]
            sel = jnp.logical_or(key > thr_t, jnp.logical_and(key == thr_t, kb + lane_q <= cut_t))
            bias_s[:, chunk_of(c)] = jnp.where(sel, 0.0, NEG_INF)

    def emit(l_of_head):
        for pr in range(n_pairs):
            even = acc_s[pr, halves[0]] / l_of_head(2 * pr)
            odd = acc_s[pr, halves[1]] / l_of_head(2 * pr + 1)
            o_ref[0, :, pair_of(2 * pr)] = jnp.where(lo_half, even, odd).astype(o_ref.dtype)

    l_s[...] = jnp.zeros(l_s.shape, F32)
    acc_s[...] = jnp.zeros(acc_s.shape, F32)

    def produce_logits(kt, buf):
        k0 = tile_start(kt)
        for pr in range(n_pairs):
            buf[pr] = lax.dot_general(qm_s[pr], k_ref[0, pl.ds(k0, tk), pair_of(2 * pr)], nt,
                                      preferred_element_type=F32)

    def consume_logits(kt, buf):
        k0 = tile_start(kt)
        tile_bias(jnp.minimum(kt, nk - 1), kt < nk)
        for pr in range(n_pairs):
            p = []
            for half in range(2):
                h = 2 * pr + half
                shift = mb_s[h]
                l_part = l_s[h]
                p_row = []
                for c in range(n_chunks):
                    e = jnp.exp(buf[pr, halves[half], chunk_of(c)] + bias_s[:, chunk_of(c)] - shift)
                    l_part = l_part + e
                    p_row.append(e.astype(BF16))
                l_s[h] = l_part
                p.append(jnp.concatenate(p_row, axis=1))
            acc_s[pr] += jnp.dot(jnp.concatenate(p, axis=0), v_ref[0, pl.ds(k0, tk), pair_of(2 * pr)],
                                 preferred_element_type=F32)

    two_stage_loop(produce_logits, consume_logits)
    l_rows = [jnp.sum(l_s[h], axis=1, keepdims=True) for h in range(N_HEADS)]
    l_min = functools.reduce(jnp.minimum, [jnp.min(l) for l in l_rows])
    well_scaled = l_min >= SOFTMAX_SUM_FLOOR

    @pl.when(well_scaled)
    def _emit_bound():
        emit(lambda h: l_rows[h])

    @pl.when(jnp.logical_not(well_scaled))
    def _running_max():
        m2_s[...] = jnp.full(m2_s.shape, NEG_INF, F32)
        l2_s[...] = jnp.zeros(l2_s.shape, F32)
        acc_s[...] = jnp.zeros(acc_s.shape, F32)

        def online_tile(kt, carry):
            k0 = pl.multiple_of(kt * tk, tk)
            tile_bias(kt, True)
            for h in range(N_HEADS):
                rows = halves[h % 2]
                logits = lax.dot_general(qm_s[h // 2, rows], k_ref[0, pl.ds(k0, tk), pair_of(h)], nt,
                                         preferred_element_type=F32) + bias_s[...]
                m_old = m2_s[h]
                m_new = jnp.maximum(m_old, jnp.max(logits, axis=1, keepdims=True))
                m_safe = jnp.where(m_new == NEG_INF, 0.0, m_new)
                alpha = jnp.exp(m_old - m_safe)
                p = jnp.exp(logits - m_safe)
                l2_s[h] = alpha * l2_s[h] + jnp.sum(p, axis=1, keepdims=True)
                acc_s[h // 2, rows] = alpha * acc_s[h // 2, rows] + jnp.dot(
                    p.astype(BF16), v_ref[0, pl.ds(k0, tk), pair_of(h)], preferred_element_type=F32)
                m2_s[h] = m_new
            return carry

        lax.fori_loop(0, nk, online_tile, 0)
        emit(lambda h: l2_s[h])


def _pick_key_tile(l_pad, k_keep):
    for tk in (256, 384, 512, 640, 768, 896, 1024):
        if l_pad % tk == 0 and tk >= k_keep:
            return tk
    return l_pad


def _attention(q, qi, kw, kmax, kiki, k_all, v_all, *, n_keys, past, k_keep, tq):
    b, t, _ = q.shape
    l_pad = k_all.shape[1]
    tk = _pick_key_tile(l_pad, k_keep)
    assert l_pad % tk == 0 and tk >= k_keep and t % tq == 0
    assert l_pad // LANES <= 256
    idx_bits = max(1, int(math.ceil(math.log2(l_pad))))
    group = max(g for g in (4, 2, 1) if l_pad % (g * tk) == 0)
    qblk = lambda w: pl.BlockSpec((1, tq, w), lambda bi, i: (bi, i, 0))
    kvblk = lambda w: _resident((1, l_pad, w), lambda bi, i: (bi, 0, 0))
    per_head = lambda w, dt: pltpu.VMEM((N_HEADS, tq, w), dt)
    per_pair = lambda dt: pltpu.VMEM((N_HEADS // 2, 2 * tq, LANES), dt)
    return pl.pallas_call(
        functools.partial(_attn_kernel, tq=tq, tk=tk, group=group, n_keys=n_keys, past=past, k_keep=k_keep,
                          idx_bits=idx_bits),
        grid=(b, t // tq),
        in_specs=[qblk(D_ATT), qblk(D_ATT), qblk(LANES), pl.BlockSpec((1, 1, LANES), lambda bi, i: (bi, 0, 0)),
                  kvblk(LANES), kvblk(D_ATT), kvblk(D_ATT)],
        out_specs=qblk(D_ATT),
        out_shape=jax.ShapeDtypeStruct((b, t, D_ATT), BF16),
        scratch_shapes=[
            pltpu.VMEM((tq, l_pad), I32),
            pltpu.VMEM((tq, l_pad), BF16),
            per_pair(BF16),
            per_pair(BF16),
            per_head(LANES, F32),
            per_head(LANES, F32),
            pltpu.VMEM((tq, tk), F32),
            per_head(LANES, F32),
            per_pair(F32),
            per_head(1, F32),
            per_head(1, F32),
            pltpu.VMEM((N_HEADS // 2, 2 * tq, tk), F32),
            pltpu.VMEM((N_HEADS // 2, 2 * tq, tk), F32),
        ],
        compiler_params=_params("arbitrary", "arbitrary"),
    )(q, qi, kw, kmax, kiki, k_all, v_all)


def _cmul(ar, ai, br, bi):
    return ar * br - ai * bi, ar * bi + ai * br


def _s5_kernel(u_ref, h0re_ref, h0im_ref, are_ref, aim_ref, ldt_ref, bre_ref, bim_ref, cre_ref, cim_ref,
               dskip_ref, wglu_ref, bglu_ref, y_ref, hre_ref, him_ref,
               bbar_s, lvl_s, pw_s, h_s, *, ts, lane_chunk):
    n = N_STATE
    t_idx = pl.program_id(1)
    first = jnp.logical_and(pl.program_id(0) == 0, t_idx == 0)

    @pl.when(first)
    def _discretise():
        dt = jnp.exp(ldt_ref[...])
        ar, ai = are_ref[...], aim_ref[...]
        mag = jnp.exp(dt * ar)
        abr, abi = mag * jnp.cos(dt * ai), mag * jnp.sin(dt * ai)
        den = ar * ar + ai * ai
        nr, ni = abr - 1.0, abi
        f_re, f_im = (nr * ar + ni * ai) / den, (ni * ar - nr * ai) / den
        reps = D_SSM // SSM_GROUP
        bre = jnp.concatenate([bre_ref[...]] * reps, axis=0)
        bim = jnp.concatenate([bim_ref[...]] * reps, axis=0)
        r_grp = lax.broadcasted_iota(I32, (D_SSM, n), 0) // SSM_GROUP
        c_grp = lax.broadcasted_iota(I32, (D_SSM, n), 1) // SSM_STATE
        same = r_grp == c_grp
        bbar_s[:, :n] = jnp.where(same, f_re * bre - f_im * bim, 0.0).astype(BF16)
        bbar_s[:, n:] = jnp.where(same, f_re * bim + f_im * bre, 0.0).astype(BF16)
        a1 = (abr, abi)
        a2 = _cmul(*a1, *a1)
        a3 = _cmul(*a2, *a1)
        a4 = _cmul(*a2, *a2)
        a5 = _cmul(*a4, *a1)
        a6 = _cmul(*a4, *a2)
        a7 = _cmul(*a4, *a3)
        a8 = _cmul(*a4, *a4)
        rows = lax.broadcasted_iota(I32, (SUBLANES, n), 0)
        for part in range(2):
            pw = jnp.zeros((SUBLANES, n), F32)
            for r, a in enumerate((a1, a2, a3, a4, a5, a6, a7, a8)):
                pw = jnp.where(rows == r, a[part], pw)
            pw_s[part] = pw
            for lv, (a, dist) in enumerate(((a1, 1), (a2, 2), (a4, 4))):
                lvl_s[lv, part] = jnp.where(rows >= dist, a[part], 0.0)

    @pl.when(t_idx == 0)
    def _load_state():
        h_s[0:SUBLANES, :] = jnp.zeros((SUBLANES, 2 * n), F32)
        h_s[SUBLANES - 1:SUBLANES, :n] = h0re_ref[0]
        h_s[SUBLANES - 1:SUBLANES, n:] = h0im_ref[0]

    u = u_ref[0]
    u_bf = u.astype(BF16)
    for c in range(2 * n // 512):
        h_s[SUBLANES:, c * 512:(c + 1) * 512] = jnp.dot(u_bf, bbar_s[:, c * 512:(c + 1) * 512],
                                                        preferred_element_type=F32)

    def tile_step(j, carry):
        r0 = pl.multiple_of(j * SUBLANES, SUBLANES)
        for c in range(n // lane_chunk):
            re_sl = slice(c * lane_chunk, (c + 1) * lane_chunk)
            im_sl = slice(n + c * lane_chunk, n + (c + 1) * lane_chunk)
            xr = h_s[pl.ds(r0 + SUBLANES, SUBLANES), re_sl]
            xi = h_s[pl.ds(r0 + SUBLANES, SUBLANES), im_sl]
            for lv, dist in enumerate((1, 2, 4)):
                sr, si = pltpu.roll(xr, dist, 0), pltpu.roll(xi, dist, 0)
                dr, di = _cmul(lvl_s[lv, 0, :, re_sl], lvl_s[lv, 1, :, re_sl], sr, si)
                xr, xi = xr + dr, xi + di
            prev_r = h_s[pl.ds(r0, SUBLANES), re_sl][SUBLANES - 1:SUBLANES, :]
            prev_i = h_s[pl.ds(r0, SUBLANES), im_sl][SUBLANES - 1:SUBLANES, :]
            cr, ci = _cmul(pw_s[0, :, re_sl], pw_s[1, :, re_sl], prev_r, prev_i)
            h_s[pl.ds(r0 + SUBLANES, SUBLANES), re_sl] = xr + cr
            h_s[pl.ds(r0 + SUBLANES, SUBLANES), im_sl] = xi + ci
        return carry

    lax.fori_loop(0, ts // SUBLANES, tile_step, 0)

    nt = (((1,), (1,)), ((), ()))
    h_re = h_s[SUBLANES:, :n].astype(BF16)
    h_im = h_s[SUBLANES:, n:].astype(BF16)
    y = (lax.dot_general(h_re, cre_ref[...], nt, preferred_element_type=F32)
         - lax.dot_general(h_im, cim_ref[...], nt, preferred_element_type=F32))
    y = y + dskip_ref[...] * u
    y = jax.nn.gelu(y)
    gate = jax.nn.sigmoid(jnp.dot(y.astype(BF16), wglu_ref[...], preferred_element_type=F32) + bglu_ref[...])
    y_ref[0] = (y * gate).astype(y_ref.dtype)

    last = h_s[ts:ts + SUBLANES, :]
    h_s[0:SUBLANES, :] = last

    @pl.when(t_idx == pl.num_programs(1) - 1)
    def _emit_state():
        hre_ref[0] = last[SUBLANES - 1:SUBLANES, :n]
        him_ref[0] = last[SUBLANES - 1:SUBLANES, n:]


def _s5(u, h0_re, h0_im, prm, ts):
    b, t, _ = u.shape
    n = N_STATE
    a_re, a_im, ldt, bre_d, bim_d, cre_t, cim_t, d_skip, w_glu, b_glu = prm
    const = lambda a: _resident(a.shape, lambda bi, i: (0,) * a.ndim)
    st_blk = pl.BlockSpec((1, 1, n), lambda bi, i: (bi, 0, 0))
    y, hre, him = pl.pallas_call(
        functools.partial(_s5_kernel, ts=ts, lane_chunk=512),
        grid=(b, t // ts),
        in_specs=[pl.BlockSpec((1, ts, D_SSM), lambda bi, i: (bi, i, 0)), st_blk, st_blk,
                  const(a_re), const(a_im), const(ldt), const(bre_d), const(bim_d), const(cre_t), const(cim_t),
                  const(d_skip), const(w_glu), const(b_glu)],
        out_specs=[pl.BlockSpec((1, ts, D_SSM), lambda bi, i: (bi, i, 0)), st_blk, st_blk],
        out_shape=[jax.ShapeDtypeStruct((b, t, D_SSM), BF16),
                   jax.ShapeDtypeStruct((b, 1, n), F32), jax.ShapeDtypeStruct((b, 1, n), F32)],
        scratch_shapes=[
            pltpu.VMEM((D_SSM, 2 * n), BF16),
            pltpu.VMEM((3, 2, SUBLANES, n), F32),
            pltpu.VMEM((2, SUBLANES, n), F32),
            pltpu.VMEM((SUBLANES + ts, 2 * n), F32),
        ],
        compiler_params=_params("arbitrary", "arbitrary"),
    )(u, h0_re, h0_im, a_re, a_im, ldt, bre_d, bim_d, cre_t, cim_t, d_skip, w_glu, b_glu)
    return y, hre, him


def _layer_norm(z, g, b):
    mu = jnp.mean(z, axis=-1, keepdims=True)
    zc = z - mu
    var = jnp.mean(zc * zc, axis=-1, keepdims=True)
    return zc * lax.rsqrt(var + LN_EPS) * g + b


def _merge_kernel(x_ref, att_ref, ssm_ref, wg_ref, bg_ref, wpa_ref, wpb_ref, wo_ref, g_ref, b_ref, o_ref, *, alpha):
    x = x_ref[...]
    d = x.shape[1]
    gates = jax.nn.sigmoid(jnp.dot(x.astype(BF16), wg_ref[...], preferred_element_type=F32) + bg_ref[...])
    mix = (gates[:, :d] * jnp.dot(att_ref[...], wpa_ref[...], preferred_element_type=F32)
           + gates[:, d:] * jnp.dot(ssm_ref[...], wpb_ref[...], preferred_element_type=F32))
    z = alpha * x + jnp.dot(mix.astype(BF16), wo_ref[...], preferred_element_type=F32)
    o_ref[...] = _layer_norm(z, g_ref[...], b_ref[...])


def _merge(x2d, att2d, ssm2d, prm, alpha, tm):
    rows, d = x2d.shape
    w_g, b_g, w_pa, w_pb, w_o, ln_g, ln_b = prm
    row_blk = lambda w: pl.BlockSpec((tm, w), lambda i: (i, 0))
    const = lambda a: _resident(a.shape, lambda i: (0, 0))
    return pl.pallas_call(
        functools.partial(_merge_kernel, alpha=alpha),
        grid=(rows // tm,),
        in_specs=[row_blk(d), row_blk(D_ATT), row_blk(D_SSM)] + [const(a) for a in prm],
        out_specs=row_blk(d),
        out_shape=jax.ShapeDtypeStruct((rows, d), F32),
        compiler_params=_params("parallel"),
    )(x2d, att2d, ssm2d, *prm)


def _ffn_kernel(x_ref, cbuf_ref, wup_ref, cw_ref, cb_ref, wdn_ref, g_ref, b_ref, o_ref, nconv_ref,
                h_s, acc_s, *, tm, d_ff, fc, alpha):
    t_idx = pl.program_id(1)
    hist = CONV_W - 1

    @pl.when(t_idx == 0)
    def _load_history():
        h_s[0:SUBLANES, :] = jnp.zeros((SUBLANES, h_s.shape[1]), F32)
        h_s[SUBLANES - hist:SUBLANES, :] = cbuf_ref[0]

    x = x_ref[0]
    x_bf = x.astype(BF16)
    n_chunks = d_ff // fc
    column_pair = lambda c: (slice(c * fc, (c + 1) * fc), slice(d_ff + c * fc, d_ff + (c + 1) * fc))

    def up_project(c):
        for cols in column_pair(c):
            h_s[SUBLANES:, cols] = jnp.dot(x_bf, wup_ref[:, cols], preferred_element_type=F32)

    def conv_act_down(c):
        conv = [cb_ref[:, cols] + sum(cw_ref[j:j + 1, cols] * h_s[SUBLANES - hist + j:SUBLANES - hist + j + tm, cols]
                                      for j in range(CONV_W)) for cols in column_pair(c)]
        act = (jax.nn.gelu(conv[0]) * conv[1]).astype(BF16)
        part = jnp.dot(act, wdn_ref[c * fc:(c + 1) * fc, :], preferred_element_type=F32)
        if c == 0:
            acc_s[...] = part
        else:
            acc_s[...] += part

    up_project(0)
    for c in range(n_chunks):
        if c + 1 < n_chunks:
            up_project(c + 1)
        conv_act_down(c)
    o_ref[0] = _layer_norm(alpha * x + acc_s[...], g_ref[...], b_ref[...])

    tail = h_s[tm:tm + SUBLANES, :]
    h_s[0:SUBLANES, :] = tail

    @pl.when(t_idx == pl.num_programs(1) - 1)
    def _emit_history():
        nconv_ref[0] = tail[SUBLANES - hist:, :]


def _ffn(x, conv_buf, prm, alpha, tm):
    b, t, d = x.shape
    w_up, conv_w, conv_b, w_dn, ln_g, ln_b = prm
    d_ff = w_dn.shape[0]
    fc = 256
    assert d_ff % fc == 0 and t % tm == 0 and tm % SUBLANES == 0
    const = lambda a: _resident(a.shape, lambda bi, i: (0, 0))
    hist_blk = pl.BlockSpec((1, CONV_W - 1, 2 * d_ff), lambda bi, i: (bi, 0, 0))
    return pl.pallas_call(
        functools.partial(_ffn_kernel, tm=tm, d_ff=d_ff, fc=fc, alpha=alpha),
        grid=(b, t // tm),
        in_specs=[pl.BlockSpec((1, tm, d), lambda bi, i: (bi, i, 0)), hist_blk] + [const(a) for a in prm],
        out_specs=[pl.BlockSpec((1, tm, d), lambda bi, i: (bi, i, 0)), hist_blk],
        out_shape=[jax.ShapeDtypeStruct((b, t, d), F32), jax.ShapeDtypeStruct((b, CONV_W - 1, 2 * d_ff), F32)],
        scratch_shapes=[pltpu.VMEM((SUBLANES + tm, 2 * d_ff), F32),
                        pltpu.VMEM((tm, d), F32)],
        compiler_params=_params("arbitrary", "arbitrary"),
    )(x, conv_buf, *prm)


def _rope_tables(pos):
    half = HEAD_DIM // 2
    inv = ROPE_THETA ** (-jnp.arange(half, dtype=F32) / half)
    ang = pos.astype(F32)[:, None] * inv[None, :]
    cos, sin = jnp.cos(ang), jnp.sin(ang)
    cos_tab = jnp.tile(cos, (1, LANES // half))
    sin_tab = jnp.tile(jnp.concatenate([-sin, sin], axis=1), (1, LANES // HEAD_DIM))
    return cos_tab, sin_tab


def _layer_weights(w_in, b_gate, w_pa, w_pb, w_o, a_re, a_im, log_dt, b_re, b_im, c_re, c_im, d_skip, w_glu,
                   b_glu, ln1_g, ln1_b, w_up, conv_w, conv_b, w_down, ln2_g, ln2_b):
    d = w_in.shape[0]
    o = 0
    pieces = []
    for w in (D_ATT, D_ATT, D_ATT, N_IDX_HEADS * IDX_DIM, IDX_DIM + N_IDX_HEADS, D_SSM, 2 * d):
        pieces.append(w_in[:, o:o + w])
        o += w
    wq, wk, wv, wqi, wkw, wu, wg = pieces
    wkw = jnp.pad(wkw, ((0, 0), (0, LANES - wkw.shape[1])))
    inproj = tuple(w.astype(BF16) for w in (wq, wk, wv, wqi, wkw, wu))
    n = N_STATE
    row = lambda a: a.reshape(1, -1).astype(F32)
    same_group = jnp.eye(N_GROUPS, dtype=bool)[:, None, :, None]
    blockdiag_t = lambda c: jnp.where(same_group, c[:, :, None, :], 0.0).reshape(D_SSM, n).astype(BF16)
    dense = lambda bm: jnp.transpose(bm, (2, 0, 1)).reshape(SSM_GROUP, n).astype(F32)
    s5 = (row(a_re), row(a_im), row(jnp.repeat(log_dt, SSM_STATE)), dense(b_re), dense(b_im),
          blockdiag_t(c_re), blockdiag_t(c_im), row(d_skip), w_glu.astype(BF16), row(b_glu))
    merge = (wg.astype(BF16), row(b_gate), w_pa.astype(BF16), w_pb.astype(BF16), w_o.astype(BF16),
             row(ln1_g), row(ln1_b))
    ffn = (w_up.astype(BF16), conv_w.astype(F32), row(conv_b), w_down.astype(BF16), row(ln2_g), row(ln2_b))
    return inproj, s5, merge, ffn


def _largest_tile(n, cap):
    t = min(n, cap)
    while n % t:
        t -= SUBLANES
    return t


def _trunk_layer(x, past_k, past_v, past_ki, h0_re, h0_im, conv_buf, weights, alpha):
    bn, t, d = x.shape
    p = 0 if past_k is None else past_k.shape[1]
    n_keys = p + t
    k_keep = min(TOPK_MAX, n_keys // 4)
    inproj_w, s5_w, merge_w, ffn_w = weights
    rows = bn * t
    tm = _largest_tile(rows, 512)

    cos_tab, sin_tab = _rope_tables(p + jnp.arange(t))
    if t % tm == 0:
        n_tab_blocks = t // tm
    else:
        cos_tab, sin_tab = jnp.tile(cos_tab, (bn, 1)), jnp.tile(sin_tab, (bn, 1))
        n_tab_blocks = rows // tm
    q, k_f, k_b, v_f, v_b, qi, kw, u = _inproj(x.reshape(rows, d), cos_tab, sin_tab, inproj_w, tm, n_tab_blocks)

    r3 = lambda a: a.reshape(bn, t, a.shape[-1])
    ki_b = r3(kw)[:, :, :IDX_DIM].astype(BF16)
    k_all, v_all = r3(k_b), r3(v_b)
    if p:
        k_all = jnp.concatenate([past_k.reshape(bn, p, D_ATT).astype(BF16), k_all], axis=1)
        v_all = jnp.concatenate([past_v.reshape(bn, p, D_ATT).astype(BF16), v_all], axis=1)
        ki_b = jnp.concatenate([past_ki.astype(BF16), ki_b], axis=1)
    l_pad = -(-n_keys // LANES) * LANES
    l_pad = max(l_pad, 2 * LANES)
    pad = lambda a: jnp.pad(a, ((0, 0), (0, l_pad - n_keys), (0, 0)))
    kiki = jnp.concatenate([ki_b, ki_b], axis=-1)
    k_heads = k_all.astype(F32).reshape(bn, n_keys, N_HEADS, HEAD_DIM)
    kmax = jnp.sqrt(jnp.max(jnp.sum(k_heads * k_heads, axis=-1), axis=1))
    kmax = jnp.pad(kmax, ((0, 0), (0, LANES - N_HEADS))).reshape(bn, 1, LANES)
    tq = _largest_tile(t, 128)
    att = _attention(r3(q), r3(qi), r3(kw), kmax, pad(kiki), pad(k_all), pad(v_all),
                     n_keys=n_keys, past=p, k_keep=k_keep, tq=tq)

    ts = _largest_tile(t, 256)
    ssm, h_re, h_im = _s5(r3(u), h0_re.reshape(bn, 1, N_STATE), h0_im.reshape(bn, 1, N_STATE), s5_w, ts)

    x1 = _merge(x.reshape(rows, d), att.reshape(rows, D_ATT), ssm.reshape(rows, D_SSM), merge_w, alpha, tm)
    x2, new_conv = _ffn(x1.reshape(bn, t, d), conv_buf, ffn_w, alpha, _largest_tile(t, 256))

    k_out = r3(k_f).reshape(bn, t, N_HEADS, HEAD_DIM)
    v_out = r3(v_f).reshape(bn, t, N_HEADS, HEAD_DIM)
    ki_out = r3(kw)[:, :, :IDX_DIM]
    return (x2, k_out, v_out, ki_out, h_re.reshape(bn, N_GROUPS, SSM_STATE), h_im.reshape(bn, N_GROUPS, SSM_STATE),
            new_conv)


def kernel(x_prompt, x_sample, cache_k, cache_v, cache_idx_k, state_ssm_re, state_ssm_im, state_conv, w_in, b_gate, w_pa, w_pb, w_o, a_re, a_im, log_dt, b_re, b_im, c_re, c_im, d_skip, w_glu, b_glu, ln1_g, ln1_b, w_up, conv_w, conv_b, w_down, ln2_g, ln2_b):
    depth = w_in.shape[0]
    alpha = (2 * depth) ** 0.25
    bp = x_prompt.shape[0]
    d_ff2 = w_up.shape[2]
    zero_h = jnp.zeros((bp, N_GROUPS, SSM_STATE), F32)
    zero_conv = jnp.zeros((bp, CONV_W - 1, d_ff2), F32)
    layer_params = (w_in, b_gate, w_pa, w_pb, w_o, a_re, a_im, log_dt, b_re, b_im, c_re, c_im, d_skip, w_glu, b_glu,
                    ln1_g, ln1_b, w_up, conv_w, conv_b, w_down, ln2_g, ln2_b)
    xp, xs = x_prompt, x_sample
    outs_p, outs_s = [], []
    for l in range(depth):
        weights = _layer_weights(*(a[l] for a in layer_params))
        rp = _trunk_layer(xp, None, None, None, zero_h, zero_h, zero_conv, weights, alpha)
        rs = _trunk_layer(xs, cache_k[l], cache_v[l], cache_idx_k[l], state_ssm_re[l].astype(F32),
                          state_ssm_im[l].astype(F32), state_conv[l], weights, alpha)
        xp, xs = rp[0], rs[0]
        outs_p.append(rp[1:])
        outs_s.append(rs[1:])
    stack = lambda outs: tuple(jnp.stack([o[j] for o in outs], axis=0) for j in range(6))
    return (xp, xs) + stack(outs_p) + stack(outs_s)
```

```python
import functools
import math

import jax
import jax.numpy as jnp
from jax import lax
from jax.experimental import pallas as pl
from jax.experimental.pallas import tpu as pltpu

CHUNK = 64
N_HEADS = 8
HEAD_DIM = 64
D_ATT = N_HEADS * HEAD_DIM
N_IDX_HEADS = 8
IDX_DIM = 64
TOPK_MAX = 256
D_SSM = 512
SSM_GROUP = 16
N_GROUPS = D_SSM // SSM_GROUP
SSM_STATE = 64
N_STATE = N_GROUPS * SSM_STATE
CONV_W = 3
ROPE_THETA = 10000.0
LN_EPS = 1e-5

LANES = 128
SUBLANES = 8
VMEM_LIMIT = 60 * 1024 * 1024

F32 = jnp.float32
BF16 = jnp.bfloat16
I32 = jnp.int32
NEG_INF = float("-inf")
INT_MIN = -(2 ** 31)
HIGH16 = -(2 ** 16)
MIN_NORMAL_BITS = 0x00800000


def _f32_key_of_neg_inf():
    bits = 0xFF800000
    signed = bits - (1 << 32)
    return signed ^ ((signed >> 31) & 0x7FFFFFFF)


NEG_INF_KEY = _f32_key_of_neg_inf()
SEARCH_UNROLL = 4
SOFTMAX_SUM_FLOOR = 1e-30


def _resident(block_shape, index_map):
    return pl.BlockSpec(block_shape, index_map, pipeline_mode=pl.Buffered(1))


def _params(*sem):
    return pltpu.CompilerParams(dimension_semantics=sem, vmem_limit_bytes=VMEM_LIMIT)


def _inproj_kernel(x_ref, cos_ref, sin_ref, wq_ref, wk_ref, wv_ref, wqi_ref, wkw_ref, wu_ref,
                   q_ref, kf_ref, kb_ref, vf_ref, vb_ref, qi_ref, kw_ref, u_ref, *, wi_scale):
    x = x_ref[...].astype(BF16)
    cos = cos_ref[...]
    sin = sin_ref[...]
    tm = x.shape[0]
    lane = lax.broadcasted_iota(I32, (tm, LANES), 1)
    first_half = (lane % HEAD_DIM) < (HEAD_DIM // 2)

    def proj(w_ref):
        return jnp.dot(x, w_ref[...], preferred_element_type=F32)

    def rope_group(y):
        rot = jnp.where(first_half, pltpu.roll(y, LANES - HEAD_DIM // 2, 1), pltpu.roll(y, HEAD_DIM // 2, 1))
        return y * cos + rot * sin

    def rope(y):
        return jnp.concatenate([rope_group(y[:, c * LANES:(c + 1) * LANES]) for c in range(y.shape[1] // LANES)],
                               axis=1)

    q = rope(proj(wq_ref))
    q_ref[...] = (q * (HEAD_DIM ** -0.5)).astype(BF16)
    k = rope(proj(wk_ref))
    kf_ref[...] = k
    kb_ref[...] = k.astype(BF16)
    v = proj(wv_ref)
    vf_ref[...] = v
    vb_ref[...] = v.astype(BF16)
    qi_ref[...] = rope(proj(wqi_ref)).astype(BF16)
    kw = proj(wkw_ref)
    kw_ref[...] = jnp.where(lane < IDX_DIM, rope_group(kw), kw * wi_scale)
    u_ref[...] = proj(wu_ref)


def _inproj(x2d, cos_tab, sin_tab, ws, tm, n_tab_blocks):
    rows, d = x2d.shape
    wq, wk, wv, wqi, wkw, wu = ws
    row_blk = lambda w: pl.BlockSpec((tm, w), lambda i: (i, 0))
    tab_blk = pl.BlockSpec((tm, LANES), lambda i: (i % n_tab_blocks, 0))
    w_blk = lambda w: _resident(w.shape, lambda i: (0, 0))
    out_shapes = [
        jax.ShapeDtypeStruct((rows, D_ATT), BF16),
        jax.ShapeDtypeStruct((rows, D_ATT), F32),
        jax.ShapeDtypeStruct((rows, D_ATT), BF16),
        jax.ShapeDtypeStruct((rows, D_ATT), F32),
        jax.ShapeDtypeStruct((rows, D_ATT), BF16),
        jax.ShapeDtypeStruct((rows, D_ATT), BF16),
        jax.ShapeDtypeStruct((rows, LANES), F32),
        jax.ShapeDtypeStruct((rows, D_SSM), F32),
    ]
    wi_scale = (N_IDX_HEADS ** -0.5) * (IDX_DIM ** -0.5)
    return pl.pallas_call(
        functools.partial(_inproj_kernel, wi_scale=wi_scale),
        grid=(rows // tm,),
        in_specs=[row_blk(d), tab_blk, tab_blk, w_blk(wq), w_blk(wk), w_blk(wv), w_blk(wqi), w_blk(wkw), w_blk(wu)],
        out_specs=[row_blk(s.shape[1]) for s in out_shapes],
        out_shape=out_shapes,
        compiler_params=_params("parallel"),
    )(x2d, cos_tab, sin_tab, wq, wk, wv, wqi, wkw, wu)


def _attn_kernel(q_ref, qi_ref, kw_ref, kmax_ref, kiki_ref, k_ref, v_ref, o_ref,
                 key_s, keyhi_s, qm_s, qim_s, wi_s, mb_s, bias_s, l_s, acc_s, m2_s, l2_s, lga_s, lgb_s,
                 *, tq, tk, group, n_keys, past, k_keep, idx_bits):
    q0 = pl.program_id(1) * tq
    n_pairs = N_HEADS // 2
    n_chunks = tk // LANES
    row = lax.broadcasted_iota(I32, (tq, 1), 0)
    q_lim = jnp.minimum(((past + q0 + row) // CHUNK + 1) * CHUNK, n_keys)
    q_lim_b = jnp.broadcast_to(q_lim, (tq, LANES))
    blk_lim = jnp.minimum(((past + q0 + tq - 1) // CHUNK + 1) * CHUNK, n_keys)
    nk = (blk_lim + tk - 1) // tk
    lane_q = lax.broadcasted_iota(I32, (tq, LANES), 1)
    lo_half = lane_q < HEAD_DIM
    nt = (((1,), (1,)), ((), ()))
    pair_of = lambda h: slice((h // 2) * LANES, (h // 2 + 1) * LANES)
    chunk_of = lambda c: slice(c * LANES, (c + 1) * LANES)

    kw = kw_ref[0]
    kmax = kmax_ref[0]
    halves = (slice(0, tq), slice(tq, 2 * tq))
    for pr in range(n_pairs):
        qi_pair = qi_ref[0, :, pair_of(2 * pr)]
        q_pair = q_ref[0, :, pair_of(2 * pr)]
        for half, keep in enumerate((lo_half, jnp.logical_not(lo_half))):
            h = 2 * pr + half
            qim_s[pr, halves[half]] = jnp.where(keep, qi_pair, jnp.zeros_like(qi_pair))
            qm = jnp.where(keep, q_pair, jnp.zeros_like(q_pair))
            qm_s[pr, halves[half]] = qm
            wi_s[h] = jnp.broadcast_to(kw[:, IDX_DIM + h:IDX_DIM + h + 1], (tq, LANES))
            qf = qm.astype(F32)
            q_norm = jnp.sqrt(jnp.sum(qf * qf, axis=1, keepdims=True))
            mb_s[h] = jnp.broadcast_to(q_norm * kmax[:, h:h + 1], (tq, LANES))

    def key_base(kt, c):
        return pl.multiple_of(kt * tk + c * LANES, LANES)

    def tile_start(kt):
        return pl.multiple_of(jnp.minimum(kt, nk - 1) * tk, tk)

    def two_stage_loop(produce, consume):
        produce(0, lga_s)

        def body(j, carry):
            produce(2 * j + 1, lgb_s)
            consume(2 * j, lga_s)
            produce(2 * j + 2, lga_s)
            consume(2 * j + 1, lgb_s)
            return carry

        lax.fori_loop(0, (nk + 1) // 2, body, 0)

    def produce_scores(kt, buf):
        kiki = kiki_ref[0, pl.ds(tile_start(kt), tk), :]
        for pr in range(n_pairs):
            buf[pr] = lax.dot_general(qim_s[pr], kiki, nt, preferred_element_type=F32)

    def consume_scores(kt, buf):
        ktc = jnp.minimum(kt, nk - 1)
        score = [jnp.zeros((tq, LANES), F32) for _ in range(n_chunks)]
        for pr in range(n_pairs):
            for half in range(2):
                w = wi_s[2 * pr + half]
                for c in range(n_chunks):
                    score[c] = score[c] + w * jnp.maximum(buf[pr, halves[half], chunk_of(c)], 0.0)
        for c in range(n_chunks):
            kb = key_base(ktc, c)
            bits = pltpu.bitcast(jnp.where(kb + lane_q < q_lim_b, score[c], NEG_INF), I32)
            key_s[:, pl.ds(kb, LANES)] = bits ^ ((bits >> 31) & 0x7FFFFFFF)
            keyhi_s[:, pl.ds(kb, LANES)] = pltpu.bitcast(bits & HIGH16, F32).astype(BF16)

    two_stage_loop(produce_scores, consume_scores)

    n_groups = (nk + group - 1) // group

    def fill_tile(kt, carry):
        for c in range(n_chunks):
            kb = key_base(kt, c)
            key_s[:, pl.ds(kb, LANES)] = jnp.full((tq, LANES), NEG_INF_KEY, I32)
            keyhi_s[:, pl.ds(kb, LANES)] = jnp.full((tq, LANES), NEG_INF, BF16)
        return carry

    lax.fori_loop(nk, n_groups * group, fill_tile, 0)

    def group_chunks(g_idx):
        return [pl.multiple_of(g_idx * (group * tk) + c * LANES, LANES) for c in range(group * n_chunks)]

    def count(pred):
        def body(g_idx, cnt):
            for kb in group_chunks(g_idx):
                cnt = cnt + pred(key_s[:, pl.ds(kb, LANES)], kb).astype(I32)
            return cnt
        cnt = lax.fori_loop(0, n_groups, body, jnp.zeros((tq, LANES), I32))
        return jnp.sum(cnt, axis=1, keepdims=True)

    def count_ge(c):
        cb = jnp.broadcast_to(c, (tq, LANES))
        return count(lambda key, kb: key >= cb)

    def count_ge_high(c):
        c_bits = (c ^ ((c >> 31) & 0x7FFFFFFF)) & HIGH16
        c_bits = jnp.where(jnp.logical_and(c_bits > 0, c_bits < MIN_NORMAL_BITS), MIN_NORMAL_BITS, c_bits)
        cb = jnp.broadcast_to(pltpu.bitcast(c_bits, F32).astype(BF16), (tq, LANES))
        one, nil = jnp.ones((tq, LANES), BF16), jnp.zeros((tq, LANES), BF16)

        def body(g_idx, cnt):
            for kb in group_chunks(g_idx):
                cnt = cnt + jnp.where(keyhi_s[:, pl.ds(kb, LANES)] >= cb, one, nil)
            return cnt
        cnt = lax.fori_loop(0, n_groups, body, nil)
        return jnp.sum(cnt.astype(F32), axis=1, keepdims=True).astype(I32)

    def search(counter, first_bit, n_bits, state):
        def cond(st):
            step, _, _, done, _ = st
            return jnp.logical_and(step < n_bits // SEARCH_UNROLL, jnp.min(done) == 0)

        def body(st):
            step, t, thr, done, at_t = st
            for i in range(SEARCH_UNROLL):
                cand = t + jnp.left_shift(jnp.int32(1), first_bit - i - step * SEARCH_UNROLL)
                cnt = counter(cand)
                t = jnp.where(cnt >= k_keep, cand, t)
                at_t = jnp.where(cnt >= k_keep, cnt, at_t)
                newly = jnp.logical_and(cnt == k_keep, done == 0)
                thr = jnp.where(newly, jnp.maximum(cand - 1, NEG_INF_KEY), thr)
                done = jnp.where(newly, 1, done)
            return step + 1, t, thr, done, at_t

        return lax.while_loop(cond, body, (jnp.int32(0),) + state)[1:]

    zero = jnp.zeros((tq, 1), I32)
    state = search(count_ge_high, 31, 16, (jnp.full((tq, 1), INT_MIN, I32), zero, zero, zero))
    t_fin, thr_early, done, at_t = search(count_ge, 15, 16, state)
    t_fin = jnp.maximum(t_fin, NEG_INF_KEY)

    def resolve_ties(_):
        tb = jnp.broadcast_to(t_fin, (tq, LANES))
        need = jnp.logical_and(jnp.logical_and(done == 0, at_t > k_keep), t_fin > NEG_INF_KEY)

        def cut_search(_):
            want = k_keep - count_ge(t_fin + 1)

            def body(j, c):
                cand = c + jnp.left_shift(jnp.int32(1), idx_bits - 1 - j)
                cb = jnp.broadcast_to(cand, (tq, LANES))
                below = count(lambda key, kb: jnp.logical_and(key == tb, kb + lane_q < cb))
                return jnp.where(below < want, cand, c)
            return lax.fori_loop(0, idx_bits, body, zero)

        cut = lax.cond(jnp.max(need.astype(I32)) > 0, cut_search, lambda _: zero, 0)
        all_ties = jnp.where(t_fin > NEG_INF_KEY, jnp.int32(2 ** 30), jnp.int32(-1))
        return jnp.where(need, cut, all_ties)

    cut_full = lax.cond(jnp.min(done) == 0, resolve_ties, lambda _: jnp.full((tq, 1), -1, I32), 0)
    thr = jnp.where(done == 1, thr_early, t_fin)
    cut = jnp.where(done == 1, -1, cut_full)
    thr_b = jnp.broadcast_to(thr, (tq, LANES))
    cut_b = jnp.broadcast_to(cut, (tq, LANES))

    def tile_bias(kt, valid):
        thr_t = jnp.where(valid, thr_b, jnp.int32(2 ** 31 - 1))
        cut_t = jnp.where(valid, cut_b, jnp.int32(-1))
        for c in range(n_chunks):
            kb = key_base(kt, c)
            key = key_s[:, pl.ds(kb, LANES)]
            sel = jnp.logical_or(key > thr_t, jnp.logical_and(key == thr_t, kb + lane_q <= cut_t))
            bias_s[:, chunk_of(c)] = jnp.where(sel, 0.0, NEG_INF)

    def emit(l_of_head):
        for pr in range(n_pairs):
            even = acc_s[pr, halves[0]] / l_of_head(2 * pr)
            odd = acc_s[pr, halves[1]] / l_of_head(2 * pr + 1)
            o_ref[0, :, pair_of(2 * pr)] = jnp.where(lo_half, even, odd).astype(o_ref.dtype)

    l_s[...] = jnp.zeros(l_s.shape, F32)
    acc_s[...] = jnp.zeros(acc_s.shape, F32)

    def produce_logits(kt, buf):
        k0 = tile_start(kt)
        for pr in range(n_pairs):
            buf[pr] = lax.dot_general(qm_s[pr], k_ref[0, pl.ds(k0, tk), pair_of(2 * pr)], nt,
                                      preferred_element_type=F32)

    def consume_logits(kt, buf):
        k0 = tile_start(kt)
        tile_bias(jnp.minimum(kt, nk - 1), kt < nk)
        for pr in range(n_pairs):
            p = []
            for half in range(2):
                h = 2 * pr + half
                shift = mb_s[h]
                l_part = l_s[h]
                p_row = []
                for c in range(n_chunks):
                    e = jnp.exp(buf[pr, halves[half], chunk_of(c)] + bias_s[:, chunk_of(c)] - shift)
                    l_part = l_part + e
                    p_row.append(e.astype(BF16))
                l_s[h] = l_part
                p.append(jnp.concatenate(p_row, axis=1))
            acc_s[pr] += jnp.dot(jnp.concatenate(p, axis=0), v_ref[0, pl.ds(k0, tk), pair_of(2 * pr)],
                                 preferred_element_type=F32)

    two_stage_loop(produce_logits, consume_logits)
    l_rows = [jnp.sum(l_s[h], axis=1, keepdims=True) for h in range(N_HEADS)]
    l_min = functools.reduce(jnp.minimum, [jnp.min(l) for l in l_rows])
    well_scaled = l_min >= SOFTMAX_SUM_FLOOR

    @pl.when(well_scaled)
    def _emit_bound():
        emit(lambda h: l_rows[h])

    @pl.when(jnp.logical_not(well_scaled))
    def _running_max():
        m2_s[...] = jnp.full(m2_s.shape, NEG_INF, F32)
        l2_s[...] = jnp.zeros(l2_s.shape, F32)
        acc_s[...] = jnp.zeros(acc_s.shape, F32)

        def online_tile(kt, carry):
            k0 = pl.multiple_of(kt * tk, tk)
            tile_bias(kt, True)
            for h in range(N_HEADS):
                rows = halves[h % 2]
                logits = lax.dot_general(qm_s[h // 2, rows], k_ref[0, pl.ds(k0, tk), pair_of(h)], nt,
                                         preferred_element_type=F32) + bias_s[...]
                m_old = m2_s[h]
                m_new = jnp.maximum(m_old, jnp.max(logits, axis=1, keepdims=True))
                m_safe = jnp.where(m_new == NEG_INF, 0.0, m_new)
                alpha = jnp.exp(m_old - m_safe)
                p = jnp.exp(logits - m_safe)
                l2_s[h] = alpha * l2_s[h] + jnp.sum(p, axis=1, keepdims=True)
                acc_s[h // 2, rows] = alpha * acc_s[h // 2, rows] + jnp.dot(
                    p.astype(BF16), v_ref[0, pl.ds(k0, tk), pair_of(h)], preferred_element_type=F32)
                m2_s[h] = m_new
            return carry

        lax.fori_loop(0, nk, online_tile, 0)
        emit(lambda h: l2_s[h])


def _pick_key_tile(l_pad, k_keep):
    for tk in (256, 384, 512, 640, 768, 896, 1024):
        if l_pad % tk == 0 and tk >= k_keep:
            return tk
    return l_pad


def _attention(q, qi, kw, kmax, kiki, k_all, v_all, *, n_keys, past, k_keep, tq):
    b, t, _ = q.shape
    l_pad = k_all.shape[1]
    tk = _pick_key_tile(l_pad, k_keep)
    assert l_pad % tk == 0 and tk >= k_keep and t % tq == 0
    assert l_pad // LANES <= 256
    idx_bits = max(1, int(math.ceil(math.log2(l_pad))))
    group = max(g for g in (4, 2, 1) if l_pad % (g * tk) == 0)
    qblk = lambda w: pl.BlockSpec((1, tq, w), lambda bi, i: (bi, i, 0))
    kvblk = lambda w: _resident((1, l_pad, w), lambda bi, i: (bi, 0, 0))
    per_head = lambda w, dt: pltpu.VMEM((N_HEADS, tq, w), dt)
    per_pair = lambda dt: pltpu.VMEM((N_HEADS // 2, 2 * tq, LANES), dt)
    return pl.pallas_call(
        functools.partial(_attn_kernel, tq=tq, tk=tk, group=group, n_keys=n_keys, past=past, k_keep=k_keep,
                          idx_bits=idx_bits),
        grid=(b, t // tq),
        in_specs=[qblk(D_ATT), qblk(D_ATT), qblk(LANES), pl.BlockSpec((1, 1, LANES), lambda bi, i: (bi, 0, 0)),
                  kvblk(LANES), kvblk(D_ATT), kvblk(D_ATT)],
        out_specs=qblk(D_ATT),
        out_shape=jax.ShapeDtypeStruct((b, t, D_ATT), BF16),
        scratch_shapes=[
            pltpu.VMEM((tq, l_pad), I32),
            pltpu.VMEM((tq, l_pad), BF16),
            per_pair(BF16),
            per_pair(BF16),
            per_head(LANES, F32),
            per_head(LANES, F32),
            pltpu.VMEM((tq, tk), F32),
            per_head(LANES, F32),
            per_pair(F32),
            per_head(1, F32),
            per_head(1, F32),
            pltpu.VMEM((N_HEADS // 2, 2 * tq, tk), F32),
            pltpu.VMEM((N_HEADS // 2, 2 * tq, tk), F32),
        ],
        compiler_params=_params("arbitrary", "arbitrary"),
    )(q, qi, kw, kmax, kiki, k_all, v_all)


def _cmul(ar, ai, br, bi):
    return ar * br - ai * bi, ar * bi + ai * br


def _s5_kernel(u_ref, h0re_ref, h0im_ref, are_ref, aim_ref, ldt_ref, bre_ref, bim_ref, cre_ref, cim_ref,
               dskip_ref, wglu_ref, bglu_ref, y_ref, hre_ref, him_ref,
               bbar_s, lvl_s, pw_s, h_s, *, ts, lane_chunk):
    n = N_STATE
    t_idx = pl.program_id(1)
    first = jnp.logical_and(pl.program_id(0) == 0, t_idx == 0)

    @pl.when(first)
    def _discretise():
        dt = jnp.exp(ldt_ref[...])
        ar, ai = are_ref[...], aim_ref[...]
        mag = jnp.exp(dt * ar)
        abr, abi = mag * jnp.cos(dt * ai), mag * jnp.sin(dt * ai)
        den = ar * ar + ai * ai
        nr, ni = abr - 1.0, abi
        f_re, f_im = (nr * ar + ni * ai) / den, (ni * ar - nr * ai) / den
        reps = D_SSM // SSM_GROUP
        bre = jnp.concatenate([bre_ref[...]] * reps, axis=0)
        bim = jnp.concatenate([bim_ref[...]] * reps, axis=0)
        r_grp = lax.broadcasted_iota(I32, (D_SSM, n), 0) // SSM_GROUP
        c_grp = lax.broadcasted_iota(I32, (D_SSM, n), 1) // SSM_STATE
        same = r_grp == c_grp
        bbar_s[:, :n] = jnp.where(same, f_re * bre - f_im * bim, 0.0).astype(BF16)
        bbar_s[:, n:] = jnp.where(same, f_re * bim + f_im * bre, 0.0).astype(BF16)
        a1 = (abr, abi)
        a2 = _cmul(*a1, *a1)
        a3 = _cmul(*a2, *a1)
        a4 = _cmul(*a2, *a2)
        a5 = _cmul(*a4, *a1)
        a6 = _cmul(*a4, *a2)
        a7 = _cmul(*a4, *a3)
        a8 = _cmul(*a4, *a4)
        rows = lax.broadcasted_iota(I32, (SUBLANES, n), 0)
        for part in range(2):
            pw = jnp.zeros((SUBLANES, n), F32)
            for r, a in enumerate((a1, a2, a3, a4, a5, a6, a7, a8)):
                pw = jnp.where(rows == r, a[part], pw)
            pw_s[part] = pw
            for lv, (a, dist) in enumerate(((a1, 1), (a2, 2), (a4, 4))):
                lvl_s[lv, part] = jnp.where(rows >= dist, a[part], 0.0)

    @pl.when(t_idx == 0)
    def _load_state():
        h_s[0:SUBLANES, :] = jnp.zeros((SUBLANES, 2 * n), F32)
        h_s[SUBLANES - 1:SUBLANES, :n] = h0re_ref[0]
        h_s[SUBLANES - 1:SUBLANES, n:] = h0im_ref[0]

    u = u_ref[0]
    u_bf = u.astype(BF16)
    n_slabs = D_SSM // LANES
    slab_w = n // n_slabs
    for part in range(2):
        for j in range(n_slabs):
            cols = slice(part * n + j * slab_w, part * n + (j + 1) * slab_w)
            h_s[SUBLANES:, cols] = jnp.dot(u_bf[:, j * LANES:(j + 1) * LANES], bbar_s[j * LANES:(j + 1) * LANES, cols],
                                           preferred_element_type=F32)

    def tile_step(j, carry):
        r0 = pl.multiple_of(j * SUBLANES, SUBLANES)
        for c in range(n // lane_chunk):
            re_sl = slice(c * lane_chunk, (c + 1) * lane_chunk)
            im_sl = slice(n + c * lane_chunk, n + (c + 1) * lane_chunk)
            xr = h_s[pl.ds(r0 + SUBLANES, SUBLANES), re_sl]
            xi = h_s[pl.ds(r0 + SUBLANES, SUBLANES), im_sl]
            for lv, dist in enumerate((1, 2, 4)):
                sr, si = pltpu.roll(xr, dist, 0), pltpu.roll(xi, dist, 0)
                dr, di = _cmul(lvl_s[lv, 0, :, re_sl], lvl_s[lv, 1, :, re_sl], sr, si)
                xr, xi = xr + dr, xi + di
            prev_r = h_s[pl.ds(r0, SUBLANES), re_sl][SUBLANES - 1:SUBLANES, :]
            prev_i = h_s[pl.ds(r0, SUBLANES), im_sl][SUBLANES - 1:SUBLANES, :]
            cr, ci = _cmul(pw_s[0, :, re_sl], pw_s[1, :, re_sl], prev_r, prev_i)
            h_s[pl.ds(r0 + SUBLANES, SUBLANES), re_sl] = xr + cr
            h_s[pl.ds(r0 + SUBLANES, SUBLANES), im_sl] = xi + ci
        return carry

    lax.fori_loop(0, ts // SUBLANES, tile_step, 0)

    nt = (((1,), (1,)), ((), ()))
    y_slabs = []
    for j in range(n_slabs):
        rows, cols = slice(j * LANES, (j + 1) * LANES), slice(j * slab_w, (j + 1) * slab_w)
        h_re = h_s[SUBLANES:, cols].astype(BF16)
        h_im = h_s[SUBLANES:, n + j * slab_w:n + (j + 1) * slab_w].astype(BF16)
        y_slabs.append(lax.dot_general(h_re, cre_ref[rows, cols], nt, preferred_element_type=F32)
                       - lax.dot_general(h_im, cim_ref[rows, cols], nt, preferred_element_type=F32))
    y = jnp.concatenate(y_slabs, axis=1)
    y = y + dskip_ref[...] * u
    y = jax.nn.gelu(y)
    gate = jax.nn.sigmoid(jnp.dot(y.astype(BF16), wglu_ref[...], preferred_element_type=F32) + bglu_ref[...])
    y_ref[0] = (y * gate).astype(y_ref.dtype)

    last = h_s[ts:ts + SUBLANES, :]
    h_s[0:SUBLANES, :] = last

    @pl.when(t_idx == pl.num_programs(1) - 1)
    def _emit_state():
        hre_ref[0] = last[SUBLANES - 1:SUBLANES, :n]
        him_ref[0] = last[SUBLANES - 1:SUBLANES, n:]


def _s5(u, h0_re, h0_im, prm, ts):
    b, t, _ = u.shape
    n = N_STATE
    a_re, a_im, ldt, bre_d, bim_d, cre_t, cim_t, d_skip, w_glu, b_glu = prm
    const = lambda a: _resident(a.shape, lambda bi, i: (0,) * a.ndim)
    st_blk = pl.BlockSpec((1, 1, n), lambda bi, i: (bi, 0, 0))
    y, hre, him = pl.pallas_call(
        functools.partial(_s5_kernel, ts=ts, lane_chunk=512),
        grid=(b, t // ts),
        in_specs=[pl.BlockSpec((1, ts, D_SSM), lambda bi, i: (bi, i, 0)), st_blk, st_blk,
                  const(a_re), const(a_im), const(ldt), const(bre_d), const(bim_d), const(cre_t), const(cim_t),
                  const(d_skip), const(w_glu), const(b_glu)],
        out_specs=[pl.BlockSpec((1, ts, D_SSM), lambda bi, i: (bi, i, 0)), st_blk, st_blk],
        out_shape=[jax.ShapeDtypeStruct((b, t, D_SSM), BF16),
                   jax.ShapeDtypeStruct((b, 1, n), F32), jax.ShapeDtypeStruct((b, 1, n), F32)],
        scratch_shapes=[
            pltpu.VMEM((D_SSM, 2 * n), BF16),
            pltpu.VMEM((3, 2, SUBLANES, n), F32),
            pltpu.VMEM((2, SUBLANES, n), F32),
            pltpu.VMEM((SUBLANES + ts, 2 * n), F32),
        ],
        compiler_params=_params("arbitrary", "arbitrary"),
    )(u, h0_re, h0_im, a_re, a_im, ldt, bre_d, bim_d, cre_t, cim_t, d_skip, w_glu, b_glu)
    return y, hre, him


def _layer_norm(z, g, b):
    mu = jnp.mean(z, axis=-1, keepdims=True)
    zc = z - mu
    var = jnp.mean(zc * zc, axis=-1, keepdims=True)
    return zc * lax.rsqrt(var + LN_EPS) * g + b


def _merge_kernel(x_ref, att_ref, ssm_ref, wg_ref, bg_ref, wpa_ref, wpb_ref, wo_ref, g_ref, b_ref, o_ref, *, alpha):
    x = x_ref[...]
    d = x.shape[1]
    gates = jax.nn.sigmoid(jnp.dot(x.astype(BF16), wg_ref[...], preferred_element_type=F32) + bg_ref[...])
    mix = (gates[:, :d] * jnp.dot(att_ref[...], wpa_ref[...], preferred_element_type=F32)
           + gates[:, d:] * jnp.dot(ssm_ref[...], wpb_ref[...], preferred_element_type=F32))
    z = alpha * x + jnp.dot(mix.astype(BF16), wo_ref[...], preferred_element_type=F32)
    o_ref[...] = _layer_norm(z, g_ref[...], b_ref[...])


def _merge(x2d, att2d, ssm2d, prm, alpha, tm):
    rows, d = x2d.shape
    w_g, b_g, w_pa, w_pb, w_o, ln_g, ln_b = prm
    row_blk = lambda w: pl.BlockSpec((tm, w), lambda i: (i, 0))
    const = lambda a: _resident(a.shape, lambda i: (0, 0))
    return pl.pallas_call(
        functools.partial(_merge_kernel, alpha=alpha),
        grid=(rows // tm,),
        in_specs=[row_blk(d), row_blk(D_ATT), row_blk(D_SSM)] + [const(a) for a in prm],
        out_specs=row_blk(d),
        out_shape=jax.ShapeDtypeStruct((rows, d), F32),
        compiler_params=_params("parallel"),
    )(x2d, att2d, ssm2d, *prm)


def _ffn_kernel(x_ref, cbuf_ref, wup_ref, cw_ref, cb_ref, wdn_ref, g_ref, b_ref, o_ref, nconv_ref,
                h_s, acc_s, *, tm, d_ff, fc, alpha):
    t_idx = pl.program_id(1)
    hist = CONV_W - 1

    @pl.when(t_idx == 0)
    def _load_history():
        h_s[0:SUBLANES, :] = jnp.zeros((SUBLANES, h_s.shape[1]), F32)
        h_s[SUBLANES - hist:SUBLANES, :] = cbuf_ref[0]

    x = x_ref[0]
    x_bf = x.astype(BF16)
    n_chunks = d_ff // fc
    column_pair = lambda c: (slice(c * fc, (c + 1) * fc), slice(d_ff + c * fc, d_ff + (c + 1) * fc))

    def up_project(c):
        for cols in column_pair(c):
            h_s[SUBLANES:, cols] = jnp.dot(x_bf, wup_ref[:, cols], preferred_element_type=F32)

    def conv_act_down(c):
        conv = [cb_ref[:, cols] + sum(cw_ref[j:j + 1, cols] * h_s[SUBLANES - hist + j:SUBLANES - hist + j + tm, cols]
                                      for j in range(CONV_W)) for cols in column_pair(c)]
        act = (jax.nn.gelu(conv[0]) * conv[1]).astype(BF16)
        part = jnp.dot(act, wdn_ref[c * fc:(c + 1) * fc, :], preferred_element_type=F32)
        if c == 0:
            acc_s[...] = part
        else:
            acc_s[...] += part

    up_project(0)
    for c in range(n_chunks):
        if c + 1 < n_chunks:
            up_project(c + 1)
        conv_act_down(c)
    o_ref[0] = _layer_norm(alpha * x + acc_s[...], g_ref[...], b_ref[...])

    tail = h_s[tm:tm + SUBLANES, :]
    h_s[0:SUBLANES, :] = tail

    @pl.when(t_idx == pl.num_programs(1) - 1)
    def _emit_history():
        nconv_ref[0] = tail[SUBLANES - hist:, :]


def _ffn(x, conv_buf, prm, alpha, tm):
    b, t, d = x.shape
    w_up, conv_w, conv_b, w_dn, ln_g, ln_b = prm
    d_ff = w_dn.shape[0]
    fc = 256
    assert d_ff % fc == 0 and t % tm == 0 and tm % SUBLANES == 0
    const = lambda a: _resident(a.shape, lambda bi, i: (0, 0))
    hist_blk = pl.BlockSpec((1, CONV_W - 1, 2 * d_ff), lambda bi, i: (bi, 0, 0))
    return pl.pallas_call(
        functools.partial(_ffn_kernel, tm=tm, d_ff=d_ff, fc=fc, alpha=alpha),
        grid=(b, t // tm),
        in_specs=[pl.BlockSpec((1, tm, d), lambda bi, i: (bi, i, 0)), hist_blk] + [const(a) for a in prm],
        out_specs=[pl.BlockSpec((1, tm, d), lambda bi, i: (bi, i, 0)), hist_blk],
        out_shape=[jax.ShapeDtypeStruct((b, t, d), F32), jax.ShapeDtypeStruct((b, CONV_W - 1, 2 * d_ff), F32)],
        scratch_shapes=[pltpu.VMEM((SUBLANES + tm, 2 * d_ff), F32),
                        pltpu.VMEM((tm, d), F32)],
        compiler_params=_params("arbitrary", "arbitrary"),
    )(x, conv_buf, *prm)


def _rope_tables(pos):
    half = HEAD_DIM // 2
    inv = ROPE_THETA ** (-jnp.arange(half, dtype=F32) / half)
    ang = pos.astype(F32)[:, None] * inv[None, :]
    cos, sin = jnp.cos(ang), jnp.sin(ang)
    cos_tab = jnp.tile(cos, (1, LANES // half))
    sin_tab = jnp.tile(jnp.concatenate([-sin, sin], axis=1), (1, LANES // HEAD_DIM))
    return cos_tab, sin_tab


def _layer_weights(w_in, b_gate, w_pa, w_pb, w_o, a_re, a_im, log_dt, b_re, b_im, c_re, c_im, d_skip, w_glu,
                   b_glu, ln1_g, ln1_b, w_up, conv_w, conv_b, w_down, ln2_g, ln2_b):
    d = w_in.shape[0]
    o = 0
    pieces = []
    for w in (D_ATT, D_ATT, D_ATT, N_IDX_HEADS * IDX_DIM, IDX_DIM + N_IDX_HEADS, D_SSM, 2 * d):
        pieces.append(w_in[:, o:o + w])
        o += w
    wq, wk, wv, wqi, wkw, wu, wg = pieces
    wkw = jnp.pad(wkw, ((0, 0), (0, LANES - wkw.shape[1])))
    inproj = tuple(w.astype(BF16) for w in (wq, wk, wv, wqi, wkw, wu))
    n = N_STATE
    row = lambda a: a.reshape(1, -1).astype(F32)
    same_group = jnp.eye(N_GROUPS, dtype=bool)[:, None, :, None]
    blockdiag_t = lambda c: jnp.where(same_group, c[:, :, None, :], 0.0).reshape(D_SSM, n).astype(BF16)
    dense = lambda bm: jnp.transpose(bm, (2, 0, 1)).reshape(SSM_GROUP, n).astype(F32)
    s5 = (row(a_re), row(a_im), row(jnp.repeat(log_dt, SSM_STATE)), dense(b_re), dense(b_im),
          blockdiag_t(c_re), blockdiag_t(c_im), row(d_skip), w_glu.astype(BF16), row(b_glu))
    merge = (wg.astype(BF16), row(b_gate), w_pa.astype(BF16), w_pb.astype(BF16), w_o.astype(BF16),
             row(ln1_g), row(ln1_b))
    ffn = (w_up.astype(BF16), conv_w.astype(F32), row(conv_b), w_down.astype(BF16), row(ln2_g), row(ln2_b))
    return inproj, s5, merge, ffn


def _largest_tile(n, cap):
    t = min(n, cap)
    while n % t:
        t -= SUBLANES
    return t


def _trunk_layer(x, past_k, past_v, past_ki, h0_re, h0_im, conv_buf, weights, alpha):
    bn, t, d = x.shape
    p = 0 if past_k is None else past_k.shape[1]
    n_keys = p + t
    k_keep = min(TOPK_MAX, n_keys // 4)
    inproj_w, s5_w, merge_w, ffn_w = weights
    rows = bn * t
    tm = _largest_tile(rows, 512)

    cos_tab, sin_tab = _rope_tables(p + jnp.arange(t))
    if t % tm == 0:
        n_tab_blocks = t // tm
    else:
        cos_tab, sin_tab = jnp.tile(cos_tab, (bn, 1)), jnp.tile(sin_tab, (bn, 1))
        n_tab_blocks = rows // tm
    q, k_f, k_b, v_f, v_b, qi, kw, u = _inproj(x.reshape(rows, d), cos_tab, sin_tab, inproj_w, tm, n_tab_blocks)

    r3 = lambda a: a.reshape(bn, t, a.shape[-1])
    ki_b = r3(kw)[:, :, :IDX_DIM].astype(BF16)
    k_all, v_all = r3(k_b), r3(v_b)
    if p:
        k_all = jnp.concatenate([past_k.reshape(bn, p, D_ATT).astype(BF16), k_all], axis=1)
        v_all = jnp.concatenate([past_v.reshape(bn, p, D_ATT).astype(BF16), v_all], axis=1)
        ki_b = jnp.concatenate([past_ki.astype(BF16), ki_b], axis=1)
    l_pad = -(-n_keys // LANES) * LANES
    l_pad = max(l_pad, 2 * LANES)
    pad = lambda a: jnp.pad(a, ((0, 0), (0, l_pad - n_keys), (0, 0)))
    kiki = jnp.concatenate([ki_b, ki_b], axis=-1)
    k_heads = k_all.astype(F32).reshape(bn, n_keys, N_HEADS, HEAD_DIM)
    kmax = jnp.sqrt(jnp.max(jnp.sum(k_heads * k_heads, axis=-1), axis=1))
    kmax = jnp.pad(kmax, ((0, 0), (0, LANES - N_HEADS))).reshape(bn, 1, LANES)
    tq = _largest_tile(t, 128)
    att = _attention(r3(q), r3(qi), r3(kw), kmax, pad(kiki), pad(k_all), pad(v_all),
                     n_keys=n_keys, past=p, k_keep=k_keep, tq=tq)

    ts = _largest_tile(t, 256)
    ssm, h_re, h_im = _s5(r3(u), h0_re.reshape(bn, 1, N_STATE), h0_im.reshape(bn, 1, N_STATE), s5_w, ts)

    x1 = _merge(x.reshape(rows, d), att.reshape(rows, D_ATT), ssm.reshape(rows, D_SSM), merge_w, alpha, tm)
    x2, new_conv = _ffn(x1.reshape(bn, t, d), conv_buf, ffn_w, alpha, _largest_tile(t, 256))

    k_out = r3(k_f).reshape(bn, t, N_HEADS, HEAD_DIM)
    v_out = r3(v_f).reshape(bn, t, N_HEADS, HEAD_DIM)
    ki_out = r3(kw)[:, :, :IDX_DIM]
    return (x2, k_out, v_out, ki_out, h_re.reshape(bn, N_GROUPS, SSM_STATE), h_im.reshape(bn, N_GROUPS, SSM_STATE),
            new_conv)


def kernel(x_prompt, x_sample, cache_k, cache_v, cache_idx_k, state_ssm_re, state_ssm_im, state_conv, w_in, b_gate, w_pa, w_pb, w_o, a_re, a_im, log_dt, b_re, b_im, c_re, c_im, d_skip, w_glu, b_glu, ln1_g, ln1_b, w_up, conv_w, conv_b, w_down, ln2_g, ln2_b):
    depth = w_in.shape[0]
    alpha = (2 * depth) ** 0.25
    bp = x_prompt.shape[0]
    d_ff2 = w_up.shape[2]
    zero_h = jnp.zeros((bp, N_GROUPS, SSM_STATE), F32)
    zero_conv = jnp.zeros((bp, CONV_W - 1, d_ff2), F32)
    layer_params = (w_in, b_gate, w_pa, w_pb, w_o, a_re, a_im, log_dt, b_re, b_im, c_re, c_im, d_skip, w_glu, b_glu,
                    ln1_g, ln1_b, w_up, conv_w, conv_b, w_down, ln2_g, ln2_b)
    xp, xs = x_prompt, x_sample
    outs_p, outs_s = [], []
    for l in range(depth):
        weights = _layer_weights(*(a[l] for a in layer_params))
        rp = _trunk_layer(xp, None, None, None, zero_h, zero_h, zero_conv, weights, alpha)
        rs = _trunk_layer(xs, cache_k[l], cache_v[l], cache_idx_k[l], state_ssm_re[l].astype(F32),
                          state_ssm_im[l].astype(F32), state_conv[l], weights, alpha)
        xp, xs = rp[0], rs[0]
        outs_p.append(rp[1:])
        outs_s.append(rs[1:])
    stack = lambda outs: tuple(jnp.stack([o[j] for o in outs], axis=0) for j in range(6))
    return (xp, xs) + stack(outs_p) + stack(outs_s)
```

```python
import functools

import jax
import jax.numpy as jnp
from jax import lax
from jax.experimental import pallas as pl
from jax.experimental.pallas import tpu as pltpu

CHUNK = 64
N_HEADS = 8
HEAD_DIM = 64
D_ATT = N_HEADS * HEAD_DIM
N_IDX_HEADS = 8
IDX_DIM = 64
TOPK_MAX = 256
D_SSM = 512
SSM_GROUP = 16
N_GROUPS = D_SSM // SSM_GROUP
SSM_STATE = 64
N_STATE = N_GROUPS * SSM_STATE
CONV_W = 3
ROPE_THETA = 10000.0
LN_EPS = 1e-5

LANES = 128
SUBLANES = 8
VMEM_LIMIT = 60 * 1024 * 1024

F32 = jnp.float32
BF16 = jnp.bfloat16
I32 = jnp.int32
NEG_INF = float("-inf")
INT_MIN = -(2 ** 31)
HIGH16 = -(2 ** 16)
MIN_NORMAL_BITS = 0x00800000


def _f32_key_of_neg_inf():
    bits = 0xFF800000
    signed = bits - (1 << 32)
    return signed ^ ((signed >> 31) & 0x7FFFFFFF)


NEG_INF_KEY = _f32_key_of_neg_inf()
SEARCH_UNROLL = 4
SOFTMAX_SUM_FLOOR = 1e-30


def _resident(block_shape, index_map):
    return pl.BlockSpec(block_shape, index_map, pipeline_mode=pl.Buffered(1))


def _params(*sem):
    return pltpu.CompilerParams(dimension_semantics=sem, vmem_limit_bytes=VMEM_LIMIT)


def _inproj_kernel(x_ref, cos_ref, sin_ref, wq_ref, wk_ref, wv_ref, wqi_ref, wkw_ref, wu_ref,
                   q_ref, kf_ref, kb_ref, vf_ref, vb_ref, qi_ref, kw_ref, u_ref, *, wi_scale):
    x = x_ref[...].astype(BF16)
    cos = cos_ref[...]
    sin = sin_ref[...]
    tm = x.shape[0]
    lane = lax.broadcasted_iota(I32, (tm, LANES), 1)
    first_half = (lane % HEAD_DIM) < (HEAD_DIM // 2)

    def proj(w_ref):
        return jnp.dot(x, w_ref[...], preferred_element_type=F32)

    def rope_group(y):
        rot = jnp.where(first_half, pltpu.roll(y, LANES - HEAD_DIM // 2, 1), pltpu.roll(y, HEAD_DIM // 2, 1))
        return y * cos + rot * sin

    def rope(y):
        return jnp.concatenate([rope_group(y[:, c * LANES:(c + 1) * LANES]) for c in range(y.shape[1] // LANES)],
                               axis=1)

    q = rope(proj(wq_ref))
    q_ref[...] = (q * (HEAD_DIM ** -0.5)).astype(BF16)
    k = rope(proj(wk_ref))
    kf_ref[...] = k
    kb_ref[...] = k.astype(BF16)
    v = proj(wv_ref)
    vf_ref[...] = v
    vb_ref[...] = v.astype(BF16)
    qi_ref[...] = rope(proj(wqi_ref)).astype(BF16)
    kw = proj(wkw_ref)
    kw_ref[...] = jnp.where(lane < IDX_DIM, rope_group(kw), kw * wi_scale)
    u_ref[...] = proj(wu_ref)


def _inproj(x2d, cos_tab, sin_tab, ws, tm, n_tab_blocks):
    rows, d = x2d.shape
    wq, wk, wv, wqi, wkw, wu = ws
    row_blk = lambda w: pl.BlockSpec((tm, w), lambda i: (i, 0))
    tab_blk = pl.BlockSpec((tm, LANES), lambda i: (i % n_tab_blocks, 0))
    w_blk = lambda w: _resident(w.shape, lambda i: (0, 0))
    out_shapes = [
        jax.ShapeDtypeStruct((rows, D_ATT), BF16),
        jax.ShapeDtypeStruct((rows, D_ATT), F32),
        jax.ShapeDtypeStruct((rows, D_ATT), BF16),
        jax.ShapeDtypeStruct((rows, D_ATT), F32),
        jax.ShapeDtypeStruct((rows, D_ATT), BF16),
        jax.ShapeDtypeStruct((rows, D_ATT), BF16),
        jax.ShapeDtypeStruct((rows, LANES), F32),
        jax.ShapeDtypeStruct((rows, D_SSM), F32),
    ]
    wi_scale = (N_IDX_HEADS ** -0.5) * (IDX_DIM ** -0.5)
    return pl.pallas_call(
        functools.partial(_inproj_kernel, wi_scale=wi_scale),
        grid=(rows // tm,),
        in_specs=[row_blk(d), tab_blk, tab_blk, w_blk(wq), w_blk(wk), w_blk(wv), w_blk(wqi), w_blk(wkw), w_blk(wu)],
        out_specs=[row_blk(s.shape[1]) for s in out_shapes],
        out_shape=out_shapes,
        compiler_params=_params("parallel"),
    )(x2d, cos_tab, sin_tab, wq, wk, wv, wqi, wkw, wu)


def _attn_kernel(q_ref, qi_ref, kw_ref, kmax_ref, kiki_ref, k_ref, v_ref, o_ref,
                 key_s, keyhi_s, qm_s, qim_s, wi_s, mb_s, bias_s, l_s, acc_s, m2_s, l2_s, lga_s, lgb_s,
                 *, tq, tk, group, n_keys, past, k_keep):
    q0 = pl.program_id(1) * tq
    n_pairs = N_HEADS // 2
    n_chunks = tk // LANES
    row = lax.broadcasted_iota(I32, (tq, 1), 0)
    q_lim = jnp.minimum(((past + q0 + row) // CHUNK + 1) * CHUNK, n_keys)
    q_lim_b = jnp.broadcast_to(q_lim, (tq, LANES))
    blk_lim = jnp.minimum(((past + q0 + tq - 1) // CHUNK + 1) * CHUNK, n_keys)
    nk = (blk_lim + tk - 1) // tk
    lane_q = lax.broadcasted_iota(I32, (tq, LANES), 1)
    lo_half = lane_q < HEAD_DIM
    nt = (((1,), (1,)), ((), ()))
    pair_of = lambda h: slice((h // 2) * LANES, (h // 2 + 1) * LANES)
    chunk_of = lambda c: slice(c * LANES, (c + 1) * LANES)

    kw = kw_ref[0]
    kmax = kmax_ref[0]
    halves = (slice(0, tq), slice(tq, 2 * tq))
    for pr in range(n_pairs):
        qi_pair = qi_ref[0, :, pair_of(2 * pr)]
        q_pair = q_ref[0, :, pair_of(2 * pr)]
        for half, keep in enumerate((lo_half, jnp.logical_not(lo_half))):
            h = 2 * pr + half
            qim_s[pr, halves[half]] = jnp.where(keep, qi_pair, jnp.zeros_like(qi_pair))
            qm = jnp.where(keep, q_pair, jnp.zeros_like(q_pair))
            qm_s[pr, halves[half]] = qm
            wi_s[h] = jnp.broadcast_to(kw[:, IDX_DIM + h:IDX_DIM + h + 1], (tq, LANES))
            qf = qm.astype(F32)
            q_norm = jnp.sqrt(jnp.sum(qf * qf, axis=1, keepdims=True))
            mb_s[h] = jnp.broadcast_to(q_norm * kmax[:, h:h + 1], (tq, LANES))

    def key_base(kt, c):
        return pl.multiple_of(kt * tk + c * LANES, LANES)

    def tile_start(kt):
        return pl.multiple_of(jnp.minimum(kt, nk - 1) * tk, tk)

    def two_stage_loop(produce, consume):
        produce(0, lga_s)

        def body(j, carry):
            produce(2 * j + 1, lgb_s)
            consume(2 * j, lga_s)
            produce(2 * j + 2, lga_s)
            consume(2 * j + 1, lgb_s)
            return carry

        lax.fori_loop(0, (nk + 1) // 2, body, 0)

    def produce_scores(kt, buf):
        kiki = kiki_ref[0, pl.ds(tile_start(kt), tk), :]
        for pr in range(n_pairs):
            buf[pr] = lax.dot_general(qim_s[pr], kiki, nt, preferred_element_type=F32)

    def consume_scores(kt, buf):
        ktc = jnp.minimum(kt, nk - 1)
        score = [jnp.zeros((tq, LANES), F32) for _ in range(n_chunks)]
        for pr in range(n_pairs):
            for half in range(2):
                w = wi_s[2 * pr + half]
                for c in range(n_chunks):
                    score[c] = score[c] + w * jnp.maximum(buf[pr, halves[half], chunk_of(c)], 0.0)
        for c in range(n_chunks):
            kb = key_base(ktc, c)
            bits = pltpu.bitcast(jnp.where(kb + lane_q < q_lim_b, score[c], NEG_INF), I32)
            key_s[:, pl.ds(kb, LANES)] = bits ^ ((bits >> 31) & 0x7FFFFFFF)
            keyhi_s[:, pl.ds(kb, LANES)] = pltpu.bitcast(bits & HIGH16, F32).astype(BF16)

    two_stage_loop(produce_scores, consume_scores)

    n_groups = (nk + group - 1) // group

    def fill_tile(kt, carry):
        for c in range(n_chunks):
            kb = key_base(kt, c)
            key_s[:, pl.ds(kb, LANES)] = jnp.full((tq, LANES), NEG_INF_KEY, I32)
            keyhi_s[:, pl.ds(kb, LANES)] = jnp.full((tq, LANES), NEG_INF, BF16)
        return carry

    lax.fori_loop(nk, n_groups * group, fill_tile, 0)

    def group_chunks(g_idx):
        return [pl.multiple_of(g_idx * (group * tk) + c * LANES, LANES) for c in range(group * n_chunks)]

    def count(pred):
        def body(g_idx, cnt):
            for kb in group_chunks(g_idx):
                cnt = cnt + pred(key_s[:, pl.ds(kb, LANES)], kb).astype(I32)
            return cnt
        cnt = lax.fori_loop(0, n_groups, body, jnp.zeros((tq, LANES), I32))
        return jnp.sum(cnt, axis=1, keepdims=True)

    def count_ge(c):
        cb = jnp.broadcast_to(c, (tq, LANES))
        return count(lambda key, kb: key >= cb)

    def count_ge_high(c):
        c_bits = (c ^ ((c >> 31) & 0x7FFFFFFF)) & HIGH16
        c_bits = jnp.where(jnp.logical_and(c_bits > 0, c_bits < MIN_NORMAL_BITS), MIN_NORMAL_BITS, c_bits)
        cb = jnp.broadcast_to(pltpu.bitcast(c_bits, F32).astype(BF16), (tq, LANES))
        one, nil = jnp.ones((tq, LANES), BF16), jnp.zeros((tq, LANES), BF16)

        def body(g_idx, cnt):
            for kb in group_chunks(g_idx):
                cnt = cnt + jnp.where(keyhi_s[:, pl.ds(kb, LANES)] >= cb, one, nil)
            return cnt
        cnt = lax.fori_loop(0, n_groups, body, nil)
        return jnp.sum(cnt.astype(F32), axis=1, keepdims=True).astype(I32)

    def search(counter, first_bit, n_bits, state):
        def cond(st):
            step, _, _, done, _ = st
            return jnp.logical_and(step < n_bits // SEARCH_UNROLL, jnp.min(done) == 0)

        def body(st):
            step, t, thr, done, at_t = st
            for i in range(SEARCH_UNROLL):
                cand = t + jnp.left_shift(jnp.int32(1), first_bit - i - step * SEARCH_UNROLL)
                cnt = counter(cand)
                t = jnp.where(cnt >= k_keep, cand, t)
                at_t = jnp.where(cnt >= k_keep, cnt, at_t)
                newly = jnp.logical_and(cnt == k_keep, done == 0)
                thr = jnp.where(newly, jnp.maximum(cand - 1, NEG_INF_KEY), thr)
                done = jnp.where(newly, 1, done)
            return step + 1, t, thr, done, at_t

        return lax.while_loop(cond, body, (jnp.int32(0),) + state)[1:]

    zero = jnp.zeros((tq, 1), I32)
    state = search(count_ge_high, 31, 16, (jnp.full((tq, 1), INT_MIN, I32), zero, zero, zero))
    t_fin, thr_early, done, at_t = search(count_ge, 15, 16, state)
    t_fin = jnp.maximum(t_fin, NEG_INF_KEY)

    def resolve_ties(_):
        tb = jnp.broadcast_to(t_fin, (tq, LANES))
        need = jnp.logical_and(jnp.logical_and(done == 0, at_t > k_keep), t_fin > NEG_INF_KEY)

        def cut_search(_):
            want = (k_keep - count_ge(t_fin + 1)).astype(F32)
            upper = jnp.where(lax.broadcasted_iota(I32, (tk, tk), 0) <= lax.broadcasted_iota(I32, (tk, tk), 1),
                              1.0, 0.0).astype(BF16)

            def body(g_idx, st):
                before, last = st
                bases = group_chunks(g_idx)
                for i in range(group):
                    tile_bases = bases[i * n_chunks:(i + 1) * n_chunks]
                    hits = [key_s[:, pl.ds(kb, LANES)] == tb for kb in tile_bases]
                    ones = jnp.concatenate([jnp.where(h, 1.0, 0.0) for h in hits], axis=1)
                    rank = before + jnp.dot(ones.astype(BF16), upper, preferred_element_type=F32)
                    for c, kb in enumerate(tile_bases):
                        take = jnp.logical_and(hits[c], rank[:, chunk_of(c)] <= want)
                        last = jnp.maximum(last, jnp.where(take, kb + lane_q, -1))
                    before = before + jnp.sum(ones, axis=1, keepdims=True)
                return before, last

            _, last = lax.fori_loop(0, n_groups, body, (jnp.zeros((tq, 1), F32), jnp.full((tq, LANES), -1, I32)))
            return jnp.max(last, axis=1, keepdims=True)

        cut = lax.cond(jnp.max(need.astype(I32)) > 0, cut_search, lambda _: zero, 0)
        all_ties = jnp.where(t_fin > NEG_INF_KEY, jnp.int32(2 ** 30), jnp.int32(-1))
        return jnp.where(need, cut, all_ties)

    cut_full = lax.cond(jnp.min(done) == 0, resolve_ties, lambda _: jnp.full((tq, 1), -1, I32), 0)
    thr = jnp.where(done == 1, thr_early, t_fin)
    cut = jnp.where(done == 1, -1, cut_full)
    thr_b = jnp.broadcast_to(thr, (tq, LANES))
    cut_b = jnp.broadcast_to(cut, (tq, LANES))

    def tile_bias(kt, valid):
        thr_t = jnp.where(valid, thr_b, jnp.int32(2 ** 31 - 1))
        cut_t = jnp.where(valid, cut_b, jnp.int32(-1))
        for c in range(n_chunks):
            kb = key_base(kt, c)
            key = key_s[:, pl.ds(kb, LANES)]
            sel = jnp.logical_or(key > thr_t, jnp.logical_and(key == thr_t, kb + lane_q <= cut_t))
            bias_s[:, chunk_of(c)] = jnp.where(sel, 0.0, NEG_INF)

    def emit(l_of_head):
        for pr in range(n_pairs):
            even = acc_s[pr, halves[0]] / l_of_head(2 * pr)
            odd = acc_s[pr, halves[1]] / l_of_head(2 * pr + 1)
            o_ref[0, :, pair_of(2 * pr)] = jnp.where(lo_half, even, odd).astype(o_ref.dtype)

    l_s[...] = jnp.zeros(l_s.shape, F32)
    acc_s[...] = jnp.zeros(acc_s.shape, F32)

    def produce_logits(kt, buf):
        k0 = tile_start(kt)
        for pr in range(n_pairs):
            buf[pr] = lax.dot_general(qm_s[pr], k_ref[0, pl.ds(k0, tk), pair_of(2 * pr)], nt,
                                      preferred_element_type=F32)

    def consume_logits(kt, buf):
        k0 = tile_start(kt)
        tile_bias(jnp.minimum(kt, nk - 1), kt < nk)
        for pr in range(n_pairs):
            p = []
            for half in range(2):
                h = 2 * pr + half
                shift = mb_s[h]
                l_part = l_s[h]
                p_row = []
                for c in range(n_chunks):
                    e = jnp.exp(buf[pr, halves[half], chunk_of(c)] + bias_s[:, chunk_of(c)] - shift)
                    l_part = l_part + e
                    p_row.append(e.astype(BF16))
                l_s[h] = l_part
                p.append(jnp.concatenate(p_row, axis=1))
            acc_s[pr] += jnp.dot(jnp.concatenate(p, axis=0), v_ref[0, pl.ds(k0, tk), pair_of(2 * pr)],
                                 preferred_element_type=F32)

    two_stage_loop(produce_logits, consume_logits)
    l_rows = [jnp.sum(l_s[h], axis=1, keepdims=True) for h in range(N_HEADS)]
    l_min = functools.reduce(jnp.minimum, [jnp.min(l) for l in l_rows])
    well_scaled = l_min >= SOFTMAX_SUM_FLOOR

    @pl.when(well_scaled)
    def _emit_bound():
        emit(lambda h: l_rows[h])

    @pl.when(jnp.logical_not(well_scaled))
    def _running_max():
        m2_s[...] = jnp.full(m2_s.shape, NEG_INF, F32)
        l2_s[...] = jnp.zeros(l2_s.shape, F32)
        acc_s[...] = jnp.zeros(acc_s.shape, F32)

        def online_tile(kt, carry):
            k0 = pl.multiple_of(kt * tk, tk)
            tile_bias(kt, True)
            for h in range(N_HEADS):
                rows = halves[h % 2]
                logits = lax.dot_general(qm_s[h // 2, rows], k_ref[0, pl.ds(k0, tk), pair_of(h)], nt,
                                         preferred_element_type=F32) + bias_s[...]
                m_old = m2_s[h]
                m_new = jnp.maximum(m_old, jnp.max(logits, axis=1, keepdims=True))
                m_safe = jnp.where(m_new == NEG_INF, 0.0, m_new)
                alpha = jnp.exp(m_old - m_safe)
                p = jnp.exp(logits - m_safe)
                l2_s[h] = alpha * l2_s[h] + jnp.sum(p, axis=1, keepdims=True)
                acc_s[h // 2, rows] = alpha * acc_s[h // 2, rows] + jnp.dot(
                    p.astype(BF16), v_ref[0, pl.ds(k0, tk), pair_of(h)], preferred_element_type=F32)
                m2_s[h] = m_new
            return carry

        lax.fori_loop(0, nk, online_tile, 0)
        emit(lambda h: l2_s[h])


def _pick_key_tile(l_pad, k_keep):
    for tk in (256, 384, 512, 640, 768, 896, 1024):
        if l_pad % tk == 0 and tk >= k_keep:
            return tk
    return l_pad


def _attention(q, qi, kw, kmax, kiki, k_all, v_all, *, n_keys, past, k_keep, tq):
    b, t, _ = q.shape
    l_pad = k_all.shape[1]
    tk = _pick_key_tile(l_pad, k_keep)
    assert l_pad % tk == 0 and tk >= k_keep and t % tq == 0
    assert l_pad // LANES <= 256
    group = max(g for g in (4, 2, 1) if l_pad % (g * tk) == 0)
    qblk = lambda w: pl.BlockSpec((1, tq, w), lambda bi, i: (bi, i, 0))
    kvblk = lambda w: _resident((1, l_pad, w), lambda bi, i: (bi, 0, 0))
    per_head = lambda w, dt: pltpu.VMEM((N_HEADS, tq, w), dt)
    per_pair = lambda dt: pltpu.VMEM((N_HEADS // 2, 2 * tq, LANES), dt)
    return pl.pallas_call(
        functools.partial(_attn_kernel, tq=tq, tk=tk, group=group, n_keys=n_keys, past=past, k_keep=k_keep),
        grid=(b, t // tq),
        in_specs=[qblk(D_ATT), qblk(D_ATT), qblk(LANES), pl.BlockSpec((1, 1, LANES), lambda bi, i: (bi, 0, 0)),
                  kvblk(LANES), kvblk(D_ATT), kvblk(D_ATT)],
        out_specs=qblk(D_ATT),
        out_shape=jax.ShapeDtypeStruct((b, t, D_ATT), BF16),
        scratch_shapes=[
            pltpu.VMEM((tq, l_pad), I32),
            pltpu.VMEM((tq, l_pad), BF16),
            per_pair(BF16),
            per_pair(BF16),
            per_head(LANES, F32),
            per_head(LANES, F32),
            pltpu.VMEM((tq, tk), F32),
            per_head(LANES, F32),
            per_pair(F32),
            per_head(1, F32),
            per_head(1, F32),
            pltpu.VMEM((N_HEADS // 2, 2 * tq, tk), F32),
            pltpu.VMEM((N_HEADS // 2, 2 * tq, tk), F32),
        ],
        compiler_params=_params("arbitrary", "arbitrary"),
    )(q, qi, kw, kmax, kiki, k_all, v_all)


def _cmul(ar, ai, br, bi):
    return ar * br - ai * bi, ar * bi + ai * br


def _s5_kernel(u_ref, h0re_ref, h0im_ref, are_ref, aim_ref, ldt_ref, bre_ref, bim_ref, cre_ref, cim_ref,
               dskip_ref, wglu_ref, bglu_ref, y_ref, hre_ref, him_ref,
               bbar_s, lvl_s, pw_s, h_s, *, ts, lane_chunk):
    n = N_STATE
    t_idx = pl.program_id(1)
    first = jnp.logical_and(pl.program_id(0) == 0, t_idx == 0)

    @pl.when(first)
    def _discretise():
        dt = jnp.exp(ldt_ref[...])
        ar, ai = are_ref[...], aim_ref[...]
        mag = jnp.exp(dt * ar)
        abr, abi = mag * jnp.cos(dt * ai), mag * jnp.sin(dt * ai)
        den = ar * ar + ai * ai
        nr, ni = abr - 1.0, abi
        f_re, f_im = (nr * ar + ni * ai) / den, (ni * ar - nr * ai) / den
        reps = D_SSM // SSM_GROUP
        bre = jnp.concatenate([bre_ref[...]] * reps, axis=0)
        bim = jnp.concatenate([bim_ref[...]] * reps, axis=0)
        r_grp = lax.broadcasted_iota(I32, (D_SSM, n), 0) // SSM_GROUP
        c_grp = lax.broadcasted_iota(I32, (D_SSM, n), 1) // SSM_STATE
        same = r_grp == c_grp
        bbar_s[:, :n] = jnp.where(same, f_re * bre - f_im * bim, 0.0).astype(BF16)
        bbar_s[:, n:] = jnp.where(same, f_re * bim + f_im * bre, 0.0).astype(BF16)
        a1 = (abr, abi)
        a2 = _cmul(*a1, *a1)
        a3 = _cmul(*a2, *a1)
        a4 = _cmul(*a2, *a2)
        a5 = _cmul(*a4, *a1)
        a6 = _cmul(*a4, *a2)
        a7 = _cmul(*a4, *a3)
        a8 = _cmul(*a4, *a4)
        rows = lax.broadcasted_iota(I32, (SUBLANES, n), 0)
        for part in range(2):
            pw = jnp.zeros((SUBLANES, n), F32)
            for r, a in enumerate((a1, a2, a3, a4, a5, a6, a7, a8)):
                pw = jnp.where(rows == r, a[part], pw)
            pw_s[part] = pw
            for lv, (a, dist) in enumerate(((a1, 1), (a2, 2), (a4, 4))):
                lvl_s[lv, part] = jnp.where(rows >= dist, a[part], 0.0)

    @pl.when(t_idx == 0)
    def _load_state():
        h_s[0:SUBLANES, :] = jnp.zeros((SUBLANES, 2 * n), F32)
        h_s[SUBLANES - 1:SUBLANES, :n] = h0re_ref[0]
        h_s[SUBLANES - 1:SUBLANES, n:] = h0im_ref[0]

    u = u_ref[0]
    u_bf = u.astype(BF16)
    n_slabs = D_SSM // LANES
    slab_w = n // n_slabs
    for part in range(2):
        for j in range(n_slabs):
            cols = slice(part * n + j * slab_w, part * n + (j + 1) * slab_w)
            h_s[SUBLANES:, cols] = jnp.dot(u_bf[:, j * LANES:(j + 1) * LANES], bbar_s[j * LANES:(j + 1) * LANES, cols],
                                           preferred_element_type=F32)

    def tile_step(j, carry):
        r0 = pl.multiple_of(j * SUBLANES, SUBLANES)
        for c in range(n // lane_chunk):
            re_sl = slice(c * lane_chunk, (c + 1) * lane_chunk)
            im_sl = slice(n + c * lane_chunk, n + (c + 1) * lane_chunk)
            xr = h_s[pl.ds(r0 + SUBLANES, SUBLANES), re_sl]
            xi = h_s[pl.ds(r0 + SUBLANES, SUBLANES), im_sl]
            for lv, dist in enumerate((1, 2, 4)):
                sr, si = pltpu.roll(xr, dist, 0), pltpu.roll(xi, dist, 0)
                dr, di = _cmul(lvl_s[lv, 0, :, re_sl], lvl_s[lv, 1, :, re_sl], sr, si)
                xr, xi = xr + dr, xi + di
            prev_r = h_s[pl.ds(r0, SUBLANES), re_sl][SUBLANES - 1:SUBLANES, :]
            prev_i = h_s[pl.ds(r0, SUBLANES), im_sl][SUBLANES - 1:SUBLANES, :]
            cr, ci = _cmul(pw_s[0, :, re_sl], pw_s[1, :, re_sl], prev_r, prev_i)
            h_s[pl.ds(r0 + SUBLANES, SUBLANES), re_sl] = xr + cr
            h_s[pl.ds(r0 + SUBLANES, SUBLANES), im_sl] = xi + ci
        return carry

    lax.fori_loop(0, ts // SUBLANES, tile_step, 0)

    nt = (((1,), (1,)), ((), ()))
    y_slabs = []
    for j in range(n_slabs):
        rows, cols = slice(j * LANES, (j + 1) * LANES), slice(j * slab_w, (j + 1) * slab_w)
        h_re = h_s[SUBLANES:, cols].astype(BF16)
        h_im = h_s[SUBLANES:, n + j * slab_w:n + (j + 1) * slab_w].astype(BF16)
        y_slabs.append(lax.dot_general(h_re, cre_ref[rows, cols], nt, preferred_element_type=F32)
                       - lax.dot_general(h_im, cim_ref[rows, cols], nt, preferred_element_type=F32))
    y = jnp.concatenate(y_slabs, axis=1)
    y = y + dskip_ref[...] * u
    y = jax.nn.gelu(y)
    gate = jax.nn.sigmoid(jnp.dot(y.astype(BF16), wglu_ref[...], preferred_element_type=F32) + bglu_ref[...])
    y_ref[0] = (y * gate).astype(y_ref.dtype)

    last = h_s[ts:ts + SUBLANES, :]
    h_s[0:SUBLANES, :] = last

    @pl.when(t_idx == pl.num_programs(1) - 1)
    def _emit_state():
        hre_ref[0] = last[SUBLANES - 1:SUBLANES, :n]
        him_ref[0] = last[SUBLANES - 1:SUBLANES, n:]


def _s5(u, h0_re, h0_im, prm, ts):
    b, t, _ = u.shape
    n = N_STATE
    a_re, a_im, ldt, bre_d, bim_d, cre_t, cim_t, d_skip, w_glu, b_glu = prm
    const = lambda a: _resident(a.shape, lambda bi, i: (0,) * a.ndim)
    st_blk = pl.BlockSpec((1, 1, n), lambda bi, i: (bi, 0, 0))
    y, hre, him = pl.pallas_call(
        functools.partial(_s5_kernel, ts=ts, lane_chunk=512),
        grid=(b, t // ts),
        in_specs=[pl.BlockSpec((1, ts, D_SSM), lambda bi, i: (bi, i, 0)), st_blk, st_blk,
                  const(a_re), const(a_im), const(ldt), const(bre_d), const(bim_d), const(cre_t), const(cim_t),
                  const(d_skip), const(w_glu), const(b_glu)],
        out_specs=[pl.BlockSpec((1, ts, D_SSM), lambda bi, i: (bi, i, 0)), st_blk, st_blk],
        out_shape=[jax.ShapeDtypeStruct((b, t, D_SSM), BF16),
                   jax.ShapeDtypeStruct((b, 1, n), F32), jax.ShapeDtypeStruct((b, 1, n), F32)],
        scratch_shapes=[
            pltpu.VMEM((D_SSM, 2 * n), BF16),
            pltpu.VMEM((3, 2, SUBLANES, n), F32),
            pltpu.VMEM((2, SUBLANES, n), F32),
            pltpu.VMEM((SUBLANES + ts, 2 * n), F32),
        ],
        compiler_params=_params("arbitrary", "arbitrary"),
    )(u, h0_re, h0_im, a_re, a_im, ldt, bre_d, bim_d, cre_t, cim_t, d_skip, w_glu, b_glu)
    return y, hre, him


def _layer_norm(z, g, b):
    mu = jnp.mean(z, axis=-1, keepdims=True)
    zc = z - mu
    var = jnp.mean(zc * zc, axis=-1, keepdims=True)
    return zc * lax.rsqrt(var + LN_EPS) * g + b


def _merge_kernel(x_ref, att_ref, ssm_ref, wg_ref, bg_ref, wpa_ref, wpb_ref, wo_ref, g_ref, b_ref, o_ref, *, alpha):
    x = x_ref[...]
    d = x.shape[1]
    gates = jax.nn.sigmoid(jnp.dot(x.astype(BF16), wg_ref[...], preferred_element_type=F32) + bg_ref[...])
    mix = (gates[:, :d] * jnp.dot(att_ref[...], wpa_ref[...], preferred_element_type=F32)
           + gates[:, d:] * jnp.dot(ssm_ref[...], wpb_ref[...], preferred_element_type=F32))
    z = alpha * x + jnp.dot(mix.astype(BF16), wo_ref[...], preferred_element_type=F32)
    o_ref[...] = _layer_norm(z, g_ref[...], b_ref[...])


def _merge(x2d, att2d, ssm2d, prm, alpha, tm):
    rows, d = x2d.shape
    w_g, b_g, w_pa, w_pb, w_o, ln_g, ln_b = prm
    row_blk = lambda w: pl.BlockSpec((tm, w), lambda i: (i, 0))
    const = lambda a: _resident(a.shape, lambda i: (0, 0))
    return pl.pallas_call(
        functools.partial(_merge_kernel, alpha=alpha),
        grid=(rows // tm,),
        in_specs=[row_blk(d), row_blk(D_ATT), row_blk(D_SSM)] + [const(a) for a in prm],
        out_specs=row_blk(d),
        out_shape=jax.ShapeDtypeStruct((rows, d), F32),
        compiler_params=_params("parallel"),
    )(x2d, att2d, ssm2d, *prm)


def _ffn_kernel(x_ref, cbuf_ref, wup_ref, cw_ref, cb_ref, wdn_ref, g_ref, b_ref, o_ref, nconv_ref,
                h_s, acc_s, *, tm, d_ff, fc, alpha):
    t_idx = pl.program_id(1)
    hist = CONV_W - 1

    @pl.when(t_idx == 0)
    def _load_history():
        h_s[0:SUBLANES, :] = jnp.zeros((SUBLANES, h_s.shape[1]), F32)
        h_s[SUBLANES - hist:SUBLANES, :] = cbuf_ref[0]

    x = x_ref[0]
    x_bf = x.astype(BF16)
    n_chunks = d_ff // fc
    column_pair = lambda c: (slice(c * fc, (c + 1) * fc), slice(d_ff + c * fc, d_ff + (c + 1) * fc))

    def up_project(c):
        for cols in column_pair(c):
            h_s[SUBLANES:, cols] = jnp.dot(x_bf, wup_ref[:, cols], preferred_element_type=F32)

    def conv_act_down(c):
        conv = [cb_ref[:, cols] + sum(cw_ref[j:j + 1, cols] * h_s[SUBLANES - hist + j:SUBLANES - hist + j + tm, cols]
                                      for j in range(CONV_W)) for cols in column_pair(c)]
        act = (jax.nn.gelu(conv[0]) * conv[1]).astype(BF16)
        part = jnp.dot(act, wdn_ref[c * fc:(c + 1) * fc, :], preferred_element_type=F32)
        if c == 0:
            acc_s[...] = part
        else:
            acc_s[...] += part

    up_project(0)
    for c in range(n_chunks):
        if c + 1 < n_chunks:
            up_project(c + 1)
        conv_act_down(c)
    o_ref[0] = _layer_norm(alpha * x + acc_s[...], g_ref[...], b_ref[...])

    tail = h_s[tm:tm + SUBLANES, :]
    h_s[0:SUBLANES, :] = tail

    @pl.when(t_idx == pl.num_programs(1) - 1)
    def _emit_history():
        nconv_ref[0] = tail[SUBLANES - hist:, :]


def _ffn(x, conv_buf, prm, alpha, tm):
    b, t, d = x.shape
    w_up, conv_w, conv_b, w_dn, ln_g, ln_b = prm
    d_ff = w_dn.shape[0]
    fc = 256
    assert d_ff % fc == 0 and t % tm == 0 and tm % SUBLANES == 0
    const = lambda a: _resident(a.shape, lambda bi, i: (0, 0))
    hist_blk = pl.BlockSpec((1, CONV_W - 1, 2 * d_ff), lambda bi, i: (bi, 0, 0))
    return pl.pallas_call(
        functools.partial(_ffn_kernel, tm=tm, d_ff=d_ff, fc=fc, alpha=alpha),
        grid=(b, t // tm),
        in_specs=[pl.BlockSpec((1, tm, d), lambda bi, i: (bi, i, 0)), hist_blk] + [const(a) for a in prm],
        out_specs=[pl.BlockSpec((1, tm, d), lambda bi, i: (bi, i, 0)), hist_blk],
        out_shape=[jax.ShapeDtypeStruct((b, t, d), F32), jax.ShapeDtypeStruct((b, CONV_W - 1, 2 * d_ff), F32)],
        scratch_shapes=[pltpu.VMEM((SUBLANES + tm, 2 * d_ff), F32),
                        pltpu.VMEM((tm, d), F32)],
        compiler_params=_params("arbitrary", "arbitrary"),
    )(x, conv_buf, *prm)


def _rope_tables(pos):
    half = HEAD_DIM // 2
    inv = ROPE_THETA ** (-jnp.arange(half, dtype=F32) / half)
    ang = pos.astype(F32)[:, None] * inv[None, :]
    cos, sin = jnp.cos(ang), jnp.sin(ang)
    cos_tab = jnp.tile(cos, (1, LANES // half))
    sin_tab = jnp.tile(jnp.concatenate([-sin, sin], axis=1), (1, LANES // HEAD_DIM))
    return cos_tab, sin_tab


def _layer_weights(w_in, b_gate, w_pa, w_pb, w_o, a_re, a_im, log_dt, b_re, b_im, c_re, c_im, d_skip, w_glu,
                   b_glu, ln1_g, ln1_b, w_up, conv_w, conv_b, w_down, ln2_g, ln2_b):
    d = w_in.shape[0]
    o = 0
    pieces = []
    for w in (D_ATT, D_ATT, D_ATT, N_IDX_HEADS * IDX_DIM, IDX_DIM + N_IDX_HEADS, D_SSM, 2 * d):
        pieces.append(w_in[:, o:o + w])
        o += w
    wq, wk, wv, wqi, wkw, wu, wg = pieces
    wkw = jnp.pad(wkw, ((0, 0), (0, LANES - wkw.shape[1])))
    inproj = tuple(w.astype(BF16) for w in (wq, wk, wv, wqi, wkw, wu))
    n = N_STATE
    row = lambda a: a.reshape(1, -1).astype(F32)
    same_group = jnp.eye(N_GROUPS, dtype=bool)[:, None, :, None]
    blockdiag_t = lambda c: jnp.where(same_group, c[:, :, None, :], 0.0).reshape(D_SSM, n).astype(BF16)
    dense = lambda bm: jnp.transpose(bm, (2, 0, 1)).reshape(SSM_GROUP, n).astype(F32)
    s5 = (row(a_re), row(a_im), row(jnp.repeat(log_dt, SSM_STATE)), dense(b_re), dense(b_im),
          blockdiag_t(c_re), blockdiag_t(c_im), row(d_skip), w_glu.astype(BF16), row(b_glu))
    merge = (wg.astype(BF16), row(b_gate), w_pa.astype(BF16), w_pb.astype(BF16), w_o.astype(BF16),
             row(ln1_g), row(ln1_b))
    ffn = (w_up.astype(BF16), conv_w.astype(F32), row(conv_b), w_down.astype(BF16), row(ln2_g), row(ln2_b))
    return inproj, s5, merge, ffn


def _largest_tile(n, cap):
    t = min(n, cap)
    while n % t:
        t -= SUBLANES
    return t


def _trunk_layer(x, past_k, past_v, past_ki, h0_re, h0_im, conv_buf, weights, alpha):
    bn, t, d = x.shape
    p = 0 if past_k is None else past_k.shape[1]
    n_keys = p + t
    k_keep = min(TOPK_MAX, n_keys // 4)
    inproj_w, s5_w, merge_w, ffn_w = weights
    rows = bn * t
    tm = _largest_tile(rows, 512)

    cos_tab, sin_tab = _rope_tables(p + jnp.arange(t))
    if t % tm == 0:
        n_tab_blocks = t // tm
    else:
        cos_tab, sin_tab = jnp.tile(cos_tab, (bn, 1)), jnp.tile(sin_tab, (bn, 1))
        n_tab_blocks = rows // tm
    q, k_f, k_b, v_f, v_b, qi, kw, u = _inproj(x.reshape(rows, d), cos_tab, sin_tab, inproj_w, tm, n_tab_blocks)

    r3 = lambda a: a.reshape(bn, t, a.shape[-1])
    ki_b = r3(kw)[:, :, :IDX_DIM].astype(BF16)
    k_all, v_all = r3(k_b), r3(v_b)
    if p:
        k_all = jnp.concatenate([past_k.reshape(bn, p, D_ATT).astype(BF16), k_all], axis=1)
        v_all = jnp.concatenate([past_v.reshape(bn, p, D_ATT).astype(BF16), v_all], axis=1)
        ki_b = jnp.concatenate([past_ki.astype(BF16), ki_b], axis=1)
    l_pad = -(-n_keys // LANES) * LANES
    l_pad = max(l_pad, 2 * LANES)
    pad = lambda a: jnp.pad(a, ((0, 0), (0, l_pad - n_keys), (0, 0)))
    kiki = jnp.concatenate([ki_b, ki_b], axis=-1)
    k_heads = k_all.astype(F32).reshape(bn, n_keys, N_HEADS, HEAD_DIM)
    kmax = jnp.sqrt(jnp.max(jnp.sum(k_heads * k_heads, axis=-1), axis=1))
    kmax = jnp.pad(kmax, ((0, 0), (0, LANES - N_HEADS))).reshape(bn, 1, LANES)
    tq = _largest_tile(t, 128)
    att = _attention(r3(q), r3(qi), r3(kw), kmax, pad(kiki), pad(k_all), pad(v_all),
                     n_keys=n_keys, past=p, k_keep=k_keep, tq=tq)

    ts = _largest_tile(t, 256)
    ssm, h_re, h_im = _s5(r3(u), h0_re.reshape(bn, 1, N_STATE), h0_im.reshape(bn, 1, N_STATE), s5_w, ts)

    x1 = _merge(x.reshape(rows, d), att.reshape(rows, D_ATT), ssm.reshape(rows, D_SSM), merge_w, alpha, tm)
    x2, new_conv = _ffn(x1.reshape(bn, t, d), conv_buf, ffn_w, alpha, _largest_tile(t, 256))

    k_out = r3(k_f).reshape(bn, t, N_HEADS, HEAD_DIM)
    v_out = r3(v_f).reshape(bn, t, N_HEADS, HEAD_DIM)
    ki_out = r3(kw)[:, :, :IDX_DIM]
    return (x2, k_out, v_out, ki_out, h_re.reshape(bn, N_GROUPS, SSM_STATE), h_im.reshape(bn, N_GROUPS, SSM_STATE),
            new_conv)


def kernel(x_prompt, x_sample, cache_k, cache_v, cache_idx_k, state_ssm_re, state_ssm_im, state_conv, w_in, b_gate, w_pa, w_pb, w_o, a_re, a_im, log_dt, b_re, b_im, c_re, c_im, d_skip, w_glu, b_glu, ln1_g, ln1_b, w_up, conv_w, conv_b, w_down, ln2_g, ln2_b):
    depth = w_in.shape[0]
    alpha = (2 * depth) ** 0.25
    bp = x_prompt.shape[0]
    d_ff2 = w_up.shape[2]
    zero_h = jnp.zeros((bp, N_GROUPS, SSM_STATE), F32)
    zero_conv = jnp.zeros((bp, CONV_W - 1, d_ff2), F32)
    layer_params = (w_in, b_gate, w_pa, w_pb, w_o, a_re, a_im, log_dt, b_re, b_im, c_re, c_im, d_skip, w_glu, b_glu,
                    ln1_g, ln1_b, w_up, conv_w, conv_b, w_down, ln2_g, ln2_b)
    xp, xs = x_prompt, x_sample
    outs_p, outs_s = [], []
    for l in range(depth):
        weights = _layer_weights(*(a[l] for a in layer_params))
        rp = _trunk_layer(xp, None, None, None, zero_h, zero_h, zero_conv, weights, alpha)
        rs = _trunk_layer(xs, cache_k[l], cache_v[l], cache_idx_k[l], state_ssm_re[l].astype(F32),
                          state_ssm_im[l].astype(F32), state_conv[l], weights, alpha)
        xp, xs = rp[0], rs[0]
        outs_p.append(rp[1:])
        outs_s.append(rs[1:])
    stack = lambda outs: tuple(jnp.stack([o[j] for o in outs], axis=0) for j in range(6))
    return (xp, xs) + stack(outs_p) + stack(outs_s)
```

```python
import functools

import jax
import jax.numpy as jnp
from jax import lax
from jax.experimental import pallas as pl
from jax.experimental.pallas import tpu as pltpu

CHUNK = 64
N_HEADS = 8
HEAD_DIM = 64
D_ATT = N_HEADS * HEAD_DIM
N_IDX_HEADS = 8
IDX_DIM = 64
TOPK_MAX = 256
D_SSM = 512
SSM_GROUP = 16
N_GROUPS = D_SSM // SSM_GROUP
SSM_STATE = 64
N_STATE = N_GROUPS * SSM_STATE
CONV_W = 3
ROPE_THETA = 10000.0
LN_EPS = 1e-5

LANES = 128
SUBLANES = 8
VMEM_LIMIT = 60 * 1024 * 1024

F32 = jnp.float32
BF16 = jnp.bfloat16
I32 = jnp.int32
NEG_INF = float("-inf")
INT_MIN = -(2 ** 31)
HIGH16 = -(2 ** 16)
MIN_NORMAL_BITS = 0x00800000


def _f32_key_of_neg_inf():
    bits = 0xFF800000
    signed = bits - (1 << 32)
    return signed ^ ((signed >> 31) & 0x7FFFFFFF)


NEG_INF_KEY = _f32_key_of_neg_inf()
SEARCH_UNROLL = 4
LOW_VALUE_BITS = 18
LOW_STEP_BITS = 3
LANE_KEEP = 3
SOFTMAX_SUM_FLOOR = 1e-30


def _resident(block_shape, index_map):
    return pl.BlockSpec(block_shape, index_map, pipeline_mode=pl.Buffered(1))


def _params(*sem):
    return pltpu.CompilerParams(dimension_semantics=sem, vmem_limit_bytes=VMEM_LIMIT)


def _inproj_kernel(x_ref, cos_ref, sin_ref, wq_ref, wk_ref, wv_ref, wqi_ref, wkw_ref, wu_ref,
                   q_ref, kf_ref, kb_ref, vf_ref, vb_ref, qi_ref, kw_ref, u_ref, *, wi_scale):
    x = x_ref[...].astype(BF16)
    cos = cos_ref[...]
    sin = sin_ref[...]
    tm = x.shape[0]
    lane = lax.broadcasted_iota(I32, (tm, LANES), 1)
    first_half = (lane % HEAD_DIM) < (HEAD_DIM // 2)

    def proj(w_ref):
        return jnp.dot(x, w_ref[...], preferred_element_type=F32)

    def rope_group(y):
        rot = jnp.where(first_half, pltpu.roll(y, LANES - HEAD_DIM // 2, 1), pltpu.roll(y, HEAD_DIM // 2, 1))
        return y * cos + rot * sin

    def rope(y):
        return jnp.concatenate([rope_group(y[:, c * LANES:(c + 1) * LANES]) for c in range(y.shape[1] // LANES)],
                               axis=1)

    q = rope(proj(wq_ref))
    q_ref[...] = (q * (HEAD_DIM ** -0.5)).astype(BF16)
    k = rope(proj(wk_ref))
    kf_ref[...] = k
    kb_ref[...] = k.astype(BF16)
    v = proj(wv_ref)
    vf_ref[...] = v
    vb_ref[...] = v.astype(BF16)
    qi_ref[...] = rope(proj(wqi_ref)).astype(BF16)
    kw = proj(wkw_ref)
    kw_ref[...] = jnp.where(lane < IDX_DIM, rope_group(kw), kw * wi_scale)
    u_ref[...] = proj(wu_ref)


def _inproj(x2d, cos_tab, sin_tab, ws, tm, n_tab_blocks):
    rows, d = x2d.shape
    wq, wk, wv, wqi, wkw, wu = ws
    row_blk = lambda w: pl.BlockSpec((tm, w), lambda i: (i, 0))
    tab_blk = pl.BlockSpec((tm, LANES), lambda i: (i % n_tab_blocks, 0))
    w_blk = lambda w: _resident(w.shape, lambda i: (0, 0))
    out_shapes = [
        jax.ShapeDtypeStruct((rows, D_ATT), BF16),
        jax.ShapeDtypeStruct((rows, D_ATT), F32),
        jax.ShapeDtypeStruct((rows, D_ATT), BF16),
        jax.ShapeDtypeStruct((rows, D_ATT), F32),
        jax.ShapeDtypeStruct((rows, D_ATT), BF16),
        jax.ShapeDtypeStruct((rows, D_ATT), BF16),
        jax.ShapeDtypeStruct((rows, LANES), F32),
        jax.ShapeDtypeStruct((rows, D_SSM), F32),
    ]
    wi_scale = (N_IDX_HEADS ** -0.5) * (IDX_DIM ** -0.5)
    return pl.pallas_call(
        functools.partial(_inproj_kernel, wi_scale=wi_scale),
        grid=(rows // tm,),
        in_specs=[row_blk(d), tab_blk, tab_blk, w_blk(wq), w_blk(wk), w_blk(wv), w_blk(wqi), w_blk(wkw), w_blk(wu)],
        out_specs=[row_blk(s.shape[1]) for s in out_shapes],
        out_shape=out_shapes,
        compiler_params=_params("parallel"),
    )(x2d, cos_tab, sin_tab, wq, wk, wv, wqi, wkw, wu)


def _attn_kernel(q_ref, qi_ref, kw_ref, kmax_ref, kiki_ref, k_ref, v_ref, o_ref,
                 key_s, keyhi_s, qm_s, qim_s, wi_s, mb_s, bias_s, l_s, acc_s, m2_s, l2_s, lga_s, lgb_s,
                 *, tq, tk, group, n_keys, past, k_keep):
    q0 = pl.program_id(1) * tq
    n_pairs = N_HEADS // 2
    n_chunks = tk // LANES
    row = lax.broadcasted_iota(I32, (tq, 1), 0)
    q_lim = jnp.minimum(((past + q0 + row) // CHUNK + 1) * CHUNK, n_keys)
    q_lim_b = jnp.broadcast_to(q_lim, (tq, LANES))
    blk_lim = jnp.minimum(((past + q0 + tq - 1) // CHUNK + 1) * CHUNK, n_keys)
    nk = (blk_lim + tk - 1) // tk
    lane_q = lax.broadcasted_iota(I32, (tq, LANES), 1)
    lo_half = lane_q < HEAD_DIM
    nt = (((1,), (1,)), ((), ()))
    pair_of = lambda h: slice((h // 2) * LANES, (h // 2 + 1) * LANES)
    chunk_of = lambda c: slice(c * LANES, (c + 1) * LANES)

    kw = kw_ref[0]
    kmax = kmax_ref[0]
    halves = (slice(0, tq), slice(tq, 2 * tq))
    for pr in range(n_pairs):
        qi_pair = qi_ref[0, :, pair_of(2 * pr)]
        q_pair = q_ref[0, :, pair_of(2 * pr)]
        for half, keep in enumerate((lo_half, jnp.logical_not(lo_half))):
            h = 2 * pr + half
            qim_s[pr, halves[half]] = jnp.where(keep, qi_pair, jnp.zeros_like(qi_pair))
            qm = jnp.where(keep, q_pair, jnp.zeros_like(q_pair))
            qm_s[pr, halves[half]] = qm
            wi_s[h] = jnp.broadcast_to(kw[:, IDX_DIM + h:IDX_DIM + h + 1], (tq, LANES))
            qf = qm.astype(F32)
            q_norm = jnp.sqrt(jnp.sum(qf * qf, axis=1, keepdims=True))
            mb_s[h] = jnp.broadcast_to(q_norm * kmax[:, h:h + 1], (tq, LANES))

    def key_base(kt, c):
        return pl.multiple_of(kt * tk + c * LANES, LANES)

    def tile_start(kt):
        return pl.multiple_of(jnp.minimum(kt, nk - 1) * tk, tk)

    def two_stage_loop(produce, consume):
        produce(0, lga_s)

        def body(j, carry):
            produce(2 * j + 1, lgb_s)
            consume(2 * j, lga_s)
            produce(2 * j + 2, lga_s)
            consume(2 * j + 1, lgb_s)
            return carry

        lax.fori_loop(0, (nk + 1) // 2, body, 0)

    def produce_scores(kt, buf):
        kiki = kiki_ref[0, pl.ds(tile_start(kt), tk), :]
        for pr in range(n_pairs):
            buf[pr] = lax.dot_general(qim_s[pr], kiki, nt, preferred_element_type=F32)

    def consume_scores(kt, buf):
        ktc = jnp.minimum(kt, nk - 1)
        score = [jnp.zeros((tq, LANES), F32) for _ in range(n_chunks)]
        for pr in range(n_pairs):
            for half in range(2):
                w = wi_s[2 * pr + half]
                for c in range(n_chunks):
                    score[c] = score[c] + w * jnp.maximum(buf[pr, halves[half], chunk_of(c)], 0.0)
        for c in range(n_chunks):
            kb = key_base(ktc, c)
            bits = pltpu.bitcast(jnp.where(kb + lane_q < q_lim_b, score[c], NEG_INF), I32)
            key_s[:, pl.ds(kb, LANES)] = bits ^ ((bits >> 31) & 0x7FFFFFFF)
            keyhi_s[:, pl.ds(kb, LANES)] = pltpu.bitcast(bits & HIGH16, F32).astype(BF16)

    two_stage_loop(produce_scores, consume_scores)

    n_groups = (nk + group - 1) // group

    def fill_tile(kt, carry):
        for c in range(n_chunks):
            kb = key_base(kt, c)
            key_s[:, pl.ds(kb, LANES)] = jnp.full((tq, LANES), NEG_INF_KEY, I32)
            keyhi_s[:, pl.ds(kb, LANES)] = jnp.full((tq, LANES), NEG_INF, BF16)
        return carry

    lax.fori_loop(nk, n_groups * group, fill_tile, 0)

    def group_chunks(g_idx):
        return [pl.multiple_of(g_idx * (group * tk) + c * LANES, LANES) for c in range(group * n_chunks)]

    def count(pred):
        def body(g_idx, cnt):
            for kb in group_chunks(g_idx):
                cnt = cnt + pred(key_s[:, pl.ds(kb, LANES)], kb).astype(I32)
            return cnt
        cnt = lax.fori_loop(0, n_groups, body, jnp.zeros((tq, LANES), I32))
        return jnp.sum(cnt, axis=1, keepdims=True)

    def count_ge(c):
        cb = jnp.broadcast_to(c, (tq, LANES))
        return count(lambda key, kb: key >= cb)

    def count_ge_high(c):
        c_bits = (c ^ ((c >> 31) & 0x7FFFFFFF)) & HIGH16
        c_bits = jnp.where(jnp.logical_and(c_bits > 0, c_bits < MIN_NORMAL_BITS), MIN_NORMAL_BITS, c_bits)
        cb = jnp.broadcast_to(pltpu.bitcast(c_bits, F32).astype(BF16), (tq, LANES))
        one, nil = jnp.ones((tq, LANES), BF16), jnp.zeros((tq, LANES), BF16)

        def body(g_idx, cnt):
            for kb in group_chunks(g_idx):
                cnt = cnt + jnp.where(keyhi_s[:, pl.ds(kb, LANES)] >= cb, one, nil)
            return cnt
        cnt = lax.fori_loop(0, n_groups, body, nil)
        return jnp.sum(cnt.astype(F32), axis=1, keepdims=True).astype(I32)

    def search(counter, first_bit, n_bits, state):
        def cond(st):
            step, _, _, done, _ = st
            return jnp.logical_and(step < n_bits // SEARCH_UNROLL, jnp.min(done) == 0)

        def body(st):
            step, t, thr, done, at_t = st
            for i in range(SEARCH_UNROLL):
                cand = t + jnp.left_shift(jnp.int32(1), first_bit - i - step * SEARCH_UNROLL)
                cnt = counter(cand)
                t = jnp.where(cnt >= k_keep, cand, t)
                at_t = jnp.where(cnt >= k_keep, cnt, at_t)
                newly = jnp.logical_and(cnt == k_keep, done == 0)
                thr = jnp.where(newly, jnp.maximum(cand - 1, NEG_INF_KEY), thr)
                done = jnp.where(newly, 1, done)
            return step + 1, t, thr, done, at_t

        return lax.while_loop(cond, body, (jnp.int32(0),) + state)[1:]

    zero = jnp.zeros((tq, 1), I32)
    high_state = search(count_ge_high, 31, 16, (jnp.full((tq, 1), INT_MIN, I32), zero, zero, zero))
    t_high = high_state[0]
    lane_sum = lambda a: jnp.sum(a.astype(F32), axis=1, keepdims=True).astype(I32)

    def low_half(_):
        above = count_ge_high(t_high + 2 ** 16)
        members = count_ge_high(t_high) - above
        bucket = jnp.broadcast_to(t_high, (tq, LANES))

        def keep_largest(g_idx, kept):
            kept = list(kept)
            for kb in group_chunks(g_idx):
                key = key_s[:, pl.ds(kb, LANES)]
                v = jnp.where((key & HIGH16) == bucket, (key & 0xFFFF) + 1, 0)
                for i in range(LANE_KEEP):
                    kept[i], v = jnp.maximum(kept[i], v), jnp.minimum(kept[i], v)
            return tuple(kept)

        nil = jnp.zeros((tq, LANES), I32)
        kept = lax.fori_loop(0, n_groups, keep_largest, (nil,) * LANE_KEEP)
        count_low = lambda pred: lane_sum(functools.reduce(jnp.add, [pred(m).astype(I32) for m in kept]))
        caught_all = jnp.min((count_low(lambda m: m > 0) == members).astype(I32)) == 1

        def from_lanes(_):
            want_low = k_keep - above
            low = zero
            for step in range(LOW_VALUE_BITS // LOW_STEP_BITS):
                unit = 1 << (LOW_VALUE_BITS - LOW_STEP_BITS * (step + 1))
                taken = zero
                low_lanes = jnp.broadcast_to(low, (tq, LANES))
                for mult in range(1, 2 ** LOW_STEP_BITS):
                    cand = low_lanes + mult * unit
                    taken = taken + (count_low(lambda m: m >= cand) >= want_low).astype(I32)
                low = low + taken * unit
            low = jnp.maximum(low, 1)
            low_b = jnp.broadcast_to(low, (tq, LANES))
            return t_high + low - 1, high_state[1], high_state[2], above + count_low(lambda m: m >= low_b)

        return lax.cond(caught_all, from_lanes, lambda _: search(count_ge, 15, 16, high_state), 0)

    t_fin, thr_early, done, at_t = lax.cond(jnp.min(high_state[2]) == 0, low_half, lambda _: high_state, 0)
    t_fin = jnp.maximum(t_fin, NEG_INF_KEY)

    def resolve_ties(_):
        tb = jnp.broadcast_to(t_fin, (tq, LANES))
        need = jnp.logical_and(jnp.logical_and(done == 0, at_t > k_keep), t_fin > NEG_INF_KEY)

        def cut_search(_):
            want = (k_keep - count_ge(t_fin + 1)).astype(F32)
            upper = jnp.where(lax.broadcasted_iota(I32, (tk, tk), 0) <= lax.broadcasted_iota(I32, (tk, tk), 1),
                              1.0, 0.0).astype(BF16)

            def body(g_idx, st):
                before, last = st
                bases = group_chunks(g_idx)
                for i in range(group):
                    tile_bases = bases[i * n_chunks:(i + 1) * n_chunks]
                    hits = [key_s[:, pl.ds(kb, LANES)] == tb for kb in tile_bases]
                    ones = jnp.concatenate([jnp.where(h, 1.0, 0.0) for h in hits], axis=1)
                    rank = before + jnp.dot(ones.astype(BF16), upper, preferred_element_type=F32)
                    for c, kb in enumerate(tile_bases):
                        take = jnp.logical_and(hits[c], rank[:, chunk_of(c)] <= want)
                        last = jnp.maximum(last, jnp.where(take, kb + lane_q, -1))
                    before = before + jnp.sum(ones, axis=1, keepdims=True)
                return before, last

            _, last = lax.fori_loop(0, n_groups, body, (jnp.zeros((tq, 1), F32), jnp.full((tq, LANES), -1, I32)))
            return jnp.max(last, axis=1, keepdims=True)

        cut = lax.cond(jnp.max(need.astype(I32)) > 0, cut_search, lambda _: zero, 0)
        all_ties = jnp.where(t_fin > NEG_INF_KEY, jnp.int32(2 ** 30), jnp.int32(-1))
        return jnp.where(need, cut, all_ties)

    cut_full = lax.cond(jnp.min(done) == 0, resolve_ties, lambda _: jnp.full((tq, 1), -1, I32), 0)
    thr = jnp.where(done == 1, thr_early, t_fin)
    cut = jnp.where(done == 1, -1, cut_full)
    thr_b = jnp.broadcast_to(thr, (tq, LANES))
    cut_b = jnp.broadcast_to(cut, (tq, LANES))

    def tile_bias(kt, valid):
        thr_t = jnp.where(valid, thr_b, jnp.int32(2 ** 31 - 1))
        cut_t = jnp.where(valid, cut_b, jnp.int32(-1))
        for c in range(n_chunks):
            kb = key_base(kt, c)
            key = key_s[:, pl.ds(kb, LANES)]
            sel = jnp.logical_or(key > thr_t, jnp.logical_and(key == thr_t, kb + lane_q <= cut_t))
            bias_s[:, chunk_of(c)] = jnp.where(sel, 0.0, NEG_INF)

    def emit(l_of_head):
        for pr in range(n_pairs):
            even = acc_s[pr, halves[0]] / l_of_head(2 * pr)
            odd = acc_s[pr, halves[1]] / l_of_head(2 * pr + 1)
            o_ref[0, :, pair_of(2 * pr)] = jnp.where(lo_half, even, odd).astype(o_ref.dtype)

    l_s[...] = jnp.zeros(l_s.shape, F32)
    acc_s[...] = jnp.zeros(acc_s.shape, F32)

    def produce_logits(kt, buf):
        k0 = tile_start(kt)
        for pr in range(n_pairs):
            buf[pr] = lax.dot_general(qm_s[pr], k_ref[0, pl.ds(k0, tk), pair_of(2 * pr)], nt,
                                      preferred_element_type=F32)

    def consume_logits(kt, buf):
        k0 = tile_start(kt)
        tile_bias(jnp.minimum(kt, nk - 1), kt < nk)
        for pr in range(n_pairs):
            p = []
            for half in range(2):
                h = 2 * pr + half
                shift = mb_s[h]
                l_part = l_s[h]
                p_row = []
                for c in range(n_chunks):
                    e = jnp.exp(buf[pr, halves[half], chunk_of(c)] + bias_s[:, chunk_of(c)] - shift)
                    l_part = l_part + e
                    p_row.append(e.astype(BF16))
                l_s[h] = l_part
                p.append(jnp.concatenate(p_row, axis=1))
            acc_s[pr] += jnp.dot(jnp.concatenate(p, axis=0), v_ref[0, pl.ds(k0, tk), pair_of(2 * pr)],
                                 preferred_element_type=F32)

    two_stage_loop(produce_logits, consume_logits)
    l_rows = [jnp.sum(l_s[h], axis=1, keepdims=True) for h in range(N_HEADS)]
    l_min = functools.reduce(jnp.minimum, [jnp.min(l) for l in l_rows])
    well_scaled = l_min >= SOFTMAX_SUM_FLOOR

    @pl.when(well_scaled)
    def _emit_bound():
        emit(lambda h: l_rows[h])

    @pl.when(jnp.logical_not(well_scaled))
    def _running_max():
        m2_s[...] = jnp.full(m2_s.shape, NEG_INF, F32)
        l2_s[...] = jnp.zeros(l2_s.shape, F32)
        acc_s[...] = jnp.zeros(acc_s.shape, F32)

        def online_tile(kt, carry):
            k0 = pl.multiple_of(kt * tk, tk)
            tile_bias(kt, True)
            for h in range(N_HEADS):
                rows = halves[h % 2]
                logits = lax.dot_general(qm_s[h // 2, rows], k_ref[0, pl.ds(k0, tk), pair_of(h)], nt,
                                         preferred_element_type=F32) + bias_s[...]
                m_old = m2_s[h]
                m_new = jnp.maximum(m_old, jnp.max(logits, axis=1, keepdims=True))
                m_safe = jnp.where(m_new == NEG_INF, 0.0, m_new)
                alpha = jnp.exp(m_old - m_safe)
                p = jnp.exp(logits - m_safe)
                l2_s[h] = alpha * l2_s[h] + jnp.sum(p, axis=1, keepdims=True)
                acc_s[h // 2, rows] = alpha * acc_s[h // 2, rows] + jnp.dot(
                    p.astype(BF16), v_ref[0, pl.ds(k0, tk), pair_of(h)], preferred_element_type=F32)
                m2_s[h] = m_new
            return carry

        lax.fori_loop(0, nk, online_tile, 0)
        emit(lambda h: l2_s[h])


def _pick_key_tile(l_pad, k_keep):
    for tk in (256, 384, 512, 640, 768, 896, 1024):
        if l_pad % tk == 0 and tk >= k_keep:
            return tk
    return l_pad


def _attention(q, qi, kw, kmax, kiki, k_all, v_all, *, n_keys, past, k_keep, tq):
    b, t, _ = q.shape
    l_pad = k_all.shape[1]
    tk = _pick_key_tile(l_pad, k_keep)
    assert l_pad % tk == 0 and tk >= k_keep and t % tq == 0
    assert l_pad // LANES <= 256
    group = max(g for g in (4, 2, 1) if l_pad % (g * tk) == 0)
    qblk = lambda w: pl.BlockSpec((1, tq, w), lambda bi, i: (bi, i, 0))
    kvblk = lambda w: _resident((1, l_pad, w), lambda bi, i: (bi, 0, 0))
    per_head = lambda w, dt: pltpu.VMEM((N_HEADS, tq, w), dt)
    per_pair = lambda dt: pltpu.VMEM((N_HEADS // 2, 2 * tq, LANES), dt)
    return pl.pallas_call(
        functools.partial(_attn_kernel, tq=tq, tk=tk, group=group, n_keys=n_keys, past=past, k_keep=k_keep),
        grid=(b, t // tq),
        in_specs=[qblk(D_ATT), qblk(D_ATT), qblk(LANES), pl.BlockSpec((1, 1, LANES), lambda bi, i: (bi, 0, 0)),
                  kvblk(LANES), kvblk(D_ATT), kvblk(D_ATT)],
        out_specs=qblk(D_ATT),
        out_shape=jax.ShapeDtypeStruct((b, t, D_ATT), BF16),
        scratch_shapes=[
            pltpu.VMEM((tq, l_pad), I32),
            pltpu.VMEM((tq, l_pad), BF16),
            per_pair(BF16),
            per_pair(BF16),
            per_head(LANES, F32),
            per_head(LANES, F32),
            pltpu.VMEM((tq, tk), F32),
            per_head(LANES, F32),
            per_pair(F32),
            per_head(1, F32),
            per_head(1, F32),
            pltpu.VMEM((N_HEADS // 2, 2 * tq, tk), F32),
            pltpu.VMEM((N_HEADS // 2, 2 * tq, tk), F32),
        ],
        compiler_params=_params("arbitrary", "arbitrary"),
    )(q, qi, kw, kmax, kiki, k_all, v_all)


def _cmul(ar, ai, br, bi):
    return ar * br - ai * bi, ar * bi + ai * br


def _s5_kernel(u_ref, h0re_ref, h0im_ref, are_ref, aim_ref, ldt_ref, bre_ref, bim_ref, cre_ref, cim_ref,
               dskip_ref, wglu_ref, bglu_ref, y_ref, hre_ref, him_ref,
               bbar_s, lvl_s, pw_s, h_s, *, ts, lane_chunk):
    n = N_STATE
    t_idx = pl.program_id(1)
    first = jnp.logical_and(pl.program_id(0) == 0, t_idx == 0)

    @pl.when(first)
    def _discretise():
        dt = jnp.exp(ldt_ref[...])
        ar, ai = are_ref[...], aim_ref[...]
        mag = jnp.exp(dt * ar)
        abr, abi = mag * jnp.cos(dt * ai), mag * jnp.sin(dt * ai)
        den = ar * ar + ai * ai
        nr, ni = abr - 1.0, abi
        f_re, f_im = (nr * ar + ni * ai) / den, (ni * ar - nr * ai) / den
        reps = D_SSM // SSM_GROUP
        bre = jnp.concatenate([bre_ref[...]] * reps, axis=0)
        bim = jnp.concatenate([bim_ref[...]] * reps, axis=0)
        r_grp = lax.broadcasted_iota(I32, (D_SSM, n), 0) // SSM_GROUP
        c_grp = lax.broadcasted_iota(I32, (D_SSM, n), 1) // SSM_STATE
        same = r_grp == c_grp
        bbar_s[:, :n] = jnp.where(same, f_re * bre - f_im * bim, 0.0).astype(BF16)
        bbar_s[:, n:] = jnp.where(same, f_re * bim + f_im * bre, 0.0).astype(BF16)
        a1 = (abr, abi)
        a2 = _cmul(*a1, *a1)
        a3 = _cmul(*a2, *a1)
        a4 = _cmul(*a2, *a2)
        a5 = _cmul(*a4, *a1)
        a6 = _cmul(*a4, *a2)
        a7 = _cmul(*a4, *a3)
        a8 = _cmul(*a4, *a4)
        rows = lax.broadcasted_iota(I32, (SUBLANES, n), 0)
        for part in range(2):
            pw = jnp.zeros((SUBLANES, n), F32)
            for r, a in enumerate((a1, a2, a3, a4, a5, a6, a7, a8)):
                pw = jnp.where(rows == r, a[part], pw)
            pw_s[part] = pw
            for lv, (a, dist) in enumerate(((a1, 1), (a2, 2), (a4, 4))):
                lvl_s[lv, part] = jnp.where(rows >= dist, a[part], 0.0)

    @pl.when(t_idx == 0)
    def _load_state():
        h_s[0:SUBLANES, :] = jnp.zeros((SUBLANES, 2 * n), F32)
        h_s[SUBLANES - 1:SUBLANES, :n] = h0re_ref[0]
        h_s[SUBLANES - 1:SUBLANES, n:] = h0im_ref[0]

    u = u_ref[0]
    u_bf = u.astype(BF16)
    n_slabs = D_SSM // LANES
    slab_w = n // n_slabs
    for part in range(2):
        for j in range(n_slabs):
            cols = slice(part * n + j * slab_w, part * n + (j + 1) * slab_w)
            h_s[SUBLANES:, cols] = jnp.dot(u_bf[:, j * LANES:(j + 1) * LANES], bbar_s[j * LANES:(j + 1) * LANES, cols],
                                           preferred_element_type=F32)

    def tile_step(j, carry):
        r0 = pl.multiple_of(j * SUBLANES, SUBLANES)
        for c in range(n // lane_chunk):
            re_sl = slice(c * lane_chunk, (c + 1) * lane_chunk)
            im_sl = slice(n + c * lane_chunk, n + (c + 1) * lane_chunk)
            xr = h_s[pl.ds(r0 + SUBLANES, SUBLANES), re_sl]
            xi = h_s[pl.ds(r0 + SUBLANES, SUBLANES), im_sl]
            for lv, dist in enumerate((1, 2, 4)):
                sr, si = pltpu.roll(xr, dist, 0), pltpu.roll(xi, dist, 0)
                dr, di = _cmul(lvl_s[lv, 0, :, re_sl], lvl_s[lv, 1, :, re_sl], sr, si)
                xr, xi = xr + dr, xi + di
            prev_r = h_s[pl.ds(r0, SUBLANES), re_sl][SUBLANES - 1:SUBLANES, :]
            prev_i = h_s[pl.ds(r0, SUBLANES), im_sl][SUBLANES - 1:SUBLANES, :]
            cr, ci = _cmul(pw_s[0, :, re_sl], pw_s[1, :, re_sl], prev_r, prev_i)
            h_s[pl.ds(r0 + SUBLANES, SUBLANES), re_sl] = xr + cr
            h_s[pl.ds(r0 + SUBLANES, SUBLANES), im_sl] = xi + ci
        return carry

    lax.fori_loop(0, ts // SUBLANES, tile_step, 0)

    nt = (((1,), (1,)), ((), ()))
    y_slabs = []
    for j in range(n_slabs):
        rows, cols = slice(j * LANES, (j + 1) * LANES), slice(j * slab_w, (j + 1) * slab_w)
        h_re = h_s[SUBLANES:, cols].astype(BF16)
        h_im = h_s[SUBLANES:, n + j * slab_w:n + (j + 1) * slab_w].astype(BF16)
        y_slabs.append(lax.dot_general(h_re, cre_ref[rows, cols], nt, preferred_element_type=F32)
                       - lax.dot_general(h_im, cim_ref[rows, cols], nt, preferred_element_type=F32))
    y = jnp.concatenate(y_slabs, axis=1)
    y = y + dskip_ref[...] * u
    y = jax.nn.gelu(y)
    gate = jax.nn.sigmoid(jnp.dot(y.astype(BF16), wglu_ref[...], preferred_element_type=F32) + bglu_ref[...])
    y_ref[0] = (y * gate).astype(y_ref.dtype)

    last = h_s[ts:ts + SUBLANES, :]
    h_s[0:SUBLANES, :] = last

    @pl.when(t_idx == pl.num_programs(1) - 1)
    def _emit_state():
        hre_ref[0] = last[SUBLANES - 1:SUBLANES, :n]
        him_ref[0] = last[SUBLANES - 1:SUBLANES, n:]


def _s5(u, h0_re, h0_im, prm, ts):
    b, t, _ = u.shape
    n = N_STATE
    a_re, a_im, ldt, bre_d, bim_d, cre_t, cim_t, d_skip, w_glu, b_glu = prm
    const = lambda a: _resident(a.shape, lambda bi, i: (0,) * a.ndim)
    st_blk = pl.BlockSpec((1, 1, n), lambda bi, i: (bi, 0, 0))
    y, hre, him = pl.pallas_call(
        functools.partial(_s5_kernel, ts=ts, lane_chunk=512),
        grid=(b, t // ts),
        in_specs=[pl.BlockSpec((1, ts, D_SSM), lambda bi, i: (bi, i, 0)), st_blk, st_blk,
                  const(a_re), const(a_im), const(ldt), const(bre_d), const(bim_d), const(cre_t), const(cim_t),
                  const(d_skip), const(w_glu), const(b_glu)],
        out_specs=[pl.BlockSpec((1, ts, D_SSM), lambda bi, i: (bi, i, 0)), st_blk, st_blk],
        out_shape=[jax.ShapeDtypeStruct((b, t, D_SSM), BF16),
                   jax.ShapeDtypeStruct((b, 1, n), F32), jax.ShapeDtypeStruct((b, 1, n), F32)],
        scratch_shapes=[
            pltpu.VMEM((D_SSM, 2 * n), BF16),
            pltpu.VMEM((3, 2, SUBLANES, n), F32),
            pltpu.VMEM((2, SUBLANES, n), F32),
            pltpu.VMEM((SUBLANES + ts, 2 * n), F32),
        ],
        compiler_params=_params("arbitrary", "arbitrary"),
    )(u, h0_re, h0_im, a_re, a_im, ldt, bre_d, bim_d, cre_t, cim_t, d_skip, w_glu, b_glu)
    return y, hre, him


def _layer_norm(z, g, b):
    mu = jnp.mean(z, axis=-1, keepdims=True)
    zc = z - mu
    var = jnp.mean(zc * zc, axis=-1, keepdims=True)
    return zc * lax.rsqrt(var + LN_EPS) * g + b


def _merge_kernel(x_ref, att_ref, ssm_ref, wg_ref, bg_ref, wpa_ref, wpb_ref, wo_ref, g_ref, b_ref, o_ref, *, alpha):
    x = x_ref[...]
    d = x.shape[1]
    gates = jax.nn.sigmoid(jnp.dot(x.astype(BF16), wg_ref[...], preferred_element_type=F32) + bg_ref[...])
    mix = (gates[:, :d] * jnp.dot(att_ref[...], wpa_ref[...], preferred_element_type=F32)
           + gates[:, d:] * jnp.dot(ssm_ref[...], wpb_ref[...], preferred_element_type=F32))
    z = alpha * x + jnp.dot(mix.astype(BF16), wo_ref[...], preferred_element_type=F32)
    o_ref[...] = _layer_norm(z, g_ref[...], b_ref[...])


def _merge(x2d, att2d, ssm2d, prm, alpha, tm):
    rows, d = x2d.shape
    w_g, b_g, w_pa, w_pb, w_o, ln_g, ln_b = prm
    row_blk = lambda w: pl.BlockSpec((tm, w), lambda i: (i, 0))
    const = lambda a: _resident(a.shape, lambda i: (0, 0))
    return pl.pallas_call(
        functools.partial(_merge_kernel, alpha=alpha),
        grid=(rows // tm,),
        in_specs=[row_blk(d), row_blk(D_ATT), row_blk(D_SSM)] + [const(a) for a in prm],
        out_specs=row_blk(d),
        out_shape=jax.ShapeDtypeStruct((rows, d), F32),
        compiler_params=_params("parallel"),
    )(x2d, att2d, ssm2d, *prm)


def _ffn_kernel(x_ref, cbuf_ref, wup_ref, cw_ref, cb_ref, wdn_ref, g_ref, b_ref, o_ref, nconv_ref,
                h_s, acc_s, *, tm, d_ff, fc, alpha):
    t_idx = pl.program_id(1)
    hist = CONV_W - 1

    @pl.when(t_idx == 0)
    def _load_history():
        h_s[0:SUBLANES, :] = jnp.zeros((SUBLANES, h_s.shape[1]), F32)
        h_s[SUBLANES - hist:SUBLANES, :] = cbuf_ref[0]

    x = x_ref[0]
    x_bf = x.astype(BF16)
    n_chunks = d_ff // fc
    column_pair = lambda c: (slice(c * fc, (c + 1) * fc), slice(d_ff + c * fc, d_ff + (c + 1) * fc))

    def up_project(c):
        for cols in column_pair(c):
            h_s[SUBLANES:, cols] = jnp.dot(x_bf, wup_ref[:, cols], preferred_element_type=F32)

    def conv_act_down(c):
        conv = [cb_ref[:, cols] + sum(cw_ref[j:j + 1, cols] * h_s[SUBLANES - hist + j:SUBLANES - hist + j + tm, cols]
                                      for j in range(CONV_W)) for cols in column_pair(c)]
        act = (jax.nn.gelu(conv[0]) * conv[1]).astype(BF16)
        part = jnp.dot(act, wdn_ref[c * fc:(c + 1) * fc, :], preferred_element_type=F32)
        if c == 0:
            acc_s[...] = part
        else:
            acc_s[...] += part

    up_project(0)
    for c in range(n_chunks):
        if c + 1 < n_chunks:
            up_project(c + 1)
        conv_act_down(c)
    o_ref[0] = _layer_norm(alpha * x + acc_s[...], g_ref[...], b_ref[...])

    tail = h_s[tm:tm + SUBLANES, :]
    h_s[0:SUBLANES, :] = tail

    @pl.when(t_idx == pl.num_programs(1) - 1)
    def _emit_history():
        nconv_ref[0] = tail[SUBLANES - hist:, :]


def _ffn(x, conv_buf, prm, alpha, tm):
    b, t, d = x.shape
    w_up, conv_w, conv_b, w_dn, ln_g, ln_b = prm
    d_ff = w_dn.shape[0]
    fc = 256
    assert d_ff % fc == 0 and t % tm == 0 and tm % SUBLANES == 0
    const = lambda a: _resident(a.shape, lambda bi, i: (0, 0))
    hist_blk = pl.BlockSpec((1, CONV_W - 1, 2 * d_ff), lambda bi, i: (bi, 0, 0))
    return pl.pallas_call(
        functools.partial(_ffn_kernel, tm=tm, d_ff=d_ff, fc=fc, alpha=alpha),
        grid=(b, t // tm),
        in_specs=[pl.BlockSpec((1, tm, d), lambda bi, i: (bi, i, 0)), hist_blk] + [const(a) for a in prm],
        out_specs=[pl.BlockSpec((1, tm, d), lambda bi, i: (bi, i, 0)), hist_blk],
        out_shape=[jax.ShapeDtypeStruct((b, t, d), F32), jax.ShapeDtypeStruct((b, CONV_W - 1, 2 * d_ff), F32)],
        scratch_shapes=[pltpu.VMEM((SUBLANES + tm, 2 * d_ff), F32),
                        pltpu.VMEM((tm, d), F32)],
        compiler_params=_params("arbitrary", "arbitrary"),
    )(x, conv_buf, *prm)


def _rope_tables(pos):
    half = HEAD_DIM // 2
    inv = ROPE_THETA ** (-jnp.arange(half, dtype=F32) / half)
    ang = pos.astype(F32)[:, None] * inv[None, :]
    cos, sin = jnp.cos(ang), jnp.sin(ang)
    cos_tab = jnp.tile(cos, (1, LANES // half))
    sin_tab = jnp.tile(jnp.concatenate([-sin, sin], axis=1), (1, LANES // HEAD_DIM))
    return cos_tab, sin_tab


def _layer_weights(w_in, b_gate, w_pa, w_pb, w_o, a_re, a_im, log_dt, b_re, b_im, c_re, c_im, d_skip, w_glu,
                   b_glu, ln1_g, ln1_b, w_up, conv_w, conv_b, w_down, ln2_g, ln2_b):
    d = w_in.shape[0]
    o = 0
    pieces = []
    for w in (D_ATT, D_ATT, D_ATT, N_IDX_HEADS * IDX_DIM, IDX_DIM + N_IDX_HEADS, D_SSM, 2 * d):
        pieces.append(w_in[:, o:o + w])
        o += w
    wq, wk, wv, wqi, wkw, wu, wg = pieces
    wkw = jnp.pad(wkw, ((0, 0), (0, LANES - wkw.shape[1])))
    inproj = tuple(w.astype(BF16) for w in (wq, wk, wv, wqi, wkw, wu))
    n = N_STATE
    row = lambda a: a.reshape(1, -1).astype(F32)
    same_group = jnp.eye(N_GROUPS, dtype=bool)[:, None, :, None]
    blockdiag_t = lambda c: jnp.where(same_group, c[:, :, None, :], 0.0).reshape(D_SSM, n).astype(BF16)
    dense = lambda bm: jnp.transpose(bm, (2, 0, 1)).reshape(SSM_GROUP, n).astype(F32)
    s5 = (row(a_re), row(a_im), row(jnp.repeat(log_dt, SSM_STATE)), dense(b_re), dense(b_im),
          blockdiag_t(c_re), blockdiag_t(c_im), row(d_skip), w_glu.astype(BF16), row(b_glu))
    merge = (wg.astype(BF16), row(b_gate), w_pa.astype(BF16), w_pb.astype(BF16), w_o.astype(BF16),
             row(ln1_g), row(ln1_b))
    ffn = (w_up.astype(BF16), conv_w.astype(F32), row(conv_b), w_down.astype(BF16), row(ln2_g), row(ln2_b))
    return inproj, s5, merge, ffn


def _largest_tile(n, cap):
    t = min(n, cap)
    while n % t:
        t -= SUBLANES
    return t


def _trunk_layer(x, past_k, past_v, past_ki, h0_re, h0_im, conv_buf, weights, alpha):
    bn, t, d = x.shape
    p = 0 if past_k is None else past_k.shape[1]
    n_keys = p + t
    k_keep = min(TOPK_MAX, n_keys // 4)
    inproj_w, s5_w, merge_w, ffn_w = weights
    rows = bn * t
    tm = _largest_tile(rows, 512)

    cos_tab, sin_tab = _rope_tables(p + jnp.arange(t))
    if t % tm == 0:
        n_tab_blocks = t // tm
    else:
        cos_tab, sin_tab = jnp.tile(cos_tab, (bn, 1)), jnp.tile(sin_tab, (bn, 1))
        n_tab_blocks = rows // tm
    q, k_f, k_b, v_f, v_b, qi, kw, u = _inproj(x.reshape(rows, d), cos_tab, sin_tab, inproj_w, tm, n_tab_blocks)

    r3 = lambda a: a.reshape(bn, t, a.shape[-1])
    ki_b = r3(kw)[:, :, :IDX_DIM].astype(BF16)
    k_all, v_all = r3(k_b), r3(v_b)
    if p:
        k_all = jnp.concatenate([past_k.reshape(bn, p, D_ATT).astype(BF16), k_all], axis=1)
        v_all = jnp.concatenate([past_v.reshape(bn, p, D_ATT).astype(BF16), v_all], axis=1)
        ki_b = jnp.concatenate([past_ki.astype(BF16), ki_b], axis=1)
    l_pad = -(-n_keys // LANES) * LANES
    l_pad = max(l_pad, 2 * LANES)
    pad = lambda a: jnp.pad(a, ((0, 0), (0, l_pad - n_keys), (0, 0)))
    kiki = jnp.concatenate([ki_b, ki_b], axis=-1)
    k_heads = k_all.astype(F32).reshape(bn, n_keys, N_HEADS, HEAD_DIM)
    kmax = jnp.sqrt(jnp.max(jnp.sum(k_heads * k_heads, axis=-1), axis=1))
    kmax = jnp.pad(kmax, ((0, 0), (0, LANES - N_HEADS))).reshape(bn, 1, LANES)
    tq = _largest_tile(t, 128)
    att = _attention(r3(q), r3(qi), r3(kw), kmax, pad(kiki), pad(k_all), pad(v_all),
                     n_keys=n_keys, past=p, k_keep=k_keep, tq=tq)

    ts = _largest_tile(t, 256)
    ssm, h_re, h_im = _s5(r3(u), h0_re.reshape(bn, 1, N_STATE), h0_im.reshape(bn, 1, N_STATE), s5_w, ts)

    x1 = _merge(x.reshape(rows, d), att.reshape(rows, D_ATT), ssm.reshape(rows, D_SSM), merge_w, alpha, tm)
    x2, new_conv = _ffn(x1.reshape(bn, t, d), conv_buf, ffn_w, alpha, _largest_tile(t, 256))

    k_out = r3(k_f).reshape(bn, t, N_HEADS, HEAD_DIM)
    v_out = r3(v_f).reshape(bn, t, N_HEADS, HEAD_DIM)
    ki_out = r3(kw)[:, :, :IDX_DIM]
    return (x2, k_out, v_out, ki_out, h_re.reshape(bn, N_GROUPS, SSM_STATE), h_im.reshape(bn, N_GROUPS, SSM_STATE),
            new_conv)


def kernel(x_prompt, x_sample, cache_k, cache_v, cache_idx_k, state_ssm_re, state_ssm_im, state_conv, w_in, b_gate, w_pa, w_pb, w_o, a_re, a_im, log_dt, b_re, b_im, c_re, c_im, d_skip, w_glu, b_glu, ln1_g, ln1_b, w_up, conv_w, conv_b, w_down, ln2_g, ln2_b):
    depth = w_in.shape[0]
    alpha = (2 * depth) ** 0.25
    bp = x_prompt.shape[0]
    d_ff2 = w_up.shape[2]
    zero_h = jnp.zeros((bp, N_GROUPS, SSM_STATE), F32)
    zero_conv = jnp.zeros((bp, CONV_W - 1, d_ff2), F32)
    layer_params = (w_in, b_gate, w_pa, w_pb, w_o, a_re, a_im, log_dt, b_re, b_im, c_re, c_im, d_skip, w_glu, b_glu,
                    ln1_g, ln1_b, w_up, conv_w, conv_b, w_down, ln2_g, ln2_b)
    xp, xs = x_prompt, x_sample
    outs_p, outs_s = [], []
    for l in range(depth):
        weights = _layer_weights(*(a[l] for a in layer_params))
        rp = _trunk_layer(xp, None, None, None, zero_h, zero_h, zero_conv, weights, alpha)
        rs = _trunk_layer(xs, cache_k[l], cache_v[l], cache_idx_k[l], state_ssm_re[l].astype(F32),
                          state_ssm_im[l].astype(F32), state_conv[l], weights, alpha)
        xp, xs = rp[0], rs[0]
        outs_p.append(rp[1:])
        outs_s.append(rs[1:])
    stack = lambda outs: tuple(jnp.stack([o[j] for o in outs], axis=0) for j in range(6))
    return (xp, xs) + stack(outs_p) + stack(outs_s)
```

```python
import functools

import jax
import jax.numpy as jnp
from jax import lax
from jax.experimental import pallas as pl
from jax.experimental.pallas import tpu as pltpu

CHUNK = 64
N_HEADS = 8
HEAD_DIM = 64
D_ATT = N_HEADS * HEAD_DIM
N_IDX_HEADS = 8
IDX_DIM = 64
TOPK_MAX = 256
D_SSM = 512
SSM_GROUP = 16
N_GROUPS = D_SSM // SSM_GROUP
SSM_STATE = 64
N_STATE = N_GROUPS * SSM_STATE
CONV_W = 3
ROPE_THETA = 10000.0
LN_EPS = 1e-5

LANES = 128
SUBLANES = 8
VMEM_LIMIT = 60 * 1024 * 1024

F32 = jnp.float32
BF16 = jnp.bfloat16
I32 = jnp.int32
NEG_INF = float("-inf")
INT_MIN = -(2 ** 31)
HIGH16 = -(2 ** 16)
MIN_NORMAL_BITS = 0x00800000


def _f32_key_of_neg_inf():
    bits = 0xFF800000
    signed = bits - (1 << 32)
    return signed ^ ((signed >> 31) & 0x7FFFFFFF)


NEG_INF_KEY = _f32_key_of_neg_inf()
SEARCH_UNROLL = 4
LOW_VALUE_BITS = 18
LOW_STEP_BITS = 2
LANE_KEEP = 3
SOFTMAX_SUM_FLOOR = 1e-30


def _resident(block_shape, index_map):
    return pl.BlockSpec(block_shape, index_map, pipeline_mode=pl.Buffered(1))


def _params(*sem):
    return pltpu.CompilerParams(dimension_semantics=sem, vmem_limit_bytes=VMEM_LIMIT)


def _inproj_kernel(x_ref, cos_ref, sin_ref, wq_ref, wk_ref, wv_ref, wqi_ref, wkw_ref, wu_ref,
                   q_ref, kf_ref, kb_ref, vf_ref, vb_ref, qi_ref, kw_ref, u_ref, *, wi_scale):
    x = x_ref[...].astype(BF16)
    cos = cos_ref[...]
    sin = sin_ref[...]
    tm = x.shape[0]
    lane = lax.broadcasted_iota(I32, (tm, LANES), 1)
    first_half = (lane % HEAD_DIM) < (HEAD_DIM // 2)

    def proj(w_ref):
        return jnp.dot(x, w_ref[...], preferred_element_type=F32)

    def rope_group(y):
        rot = jnp.where(first_half, pltpu.roll(y, LANES - HEAD_DIM // 2, 1), pltpu.roll(y, HEAD_DIM // 2, 1))
        return y * cos + rot * sin

    def rope(y):
        return jnp.concatenate([rope_group(y[:, c * LANES:(c + 1) * LANES]) for c in range(y.shape[1] // LANES)],
                               axis=1)

    q = rope(proj(wq_ref))
    q_ref[...] = (q * (HEAD_DIM ** -0.5)).astype(BF16)
    k = rope(proj(wk_ref))
    kf_ref[...] = k
    kb_ref[...] = k.astype(BF16)
    v = proj(wv_ref)
    vf_ref[...] = v
    vb_ref[...] = v.astype(BF16)
    qi_ref[...] = rope(proj(wqi_ref)).astype(BF16)
    kw = proj(wkw_ref)
    kw_ref[...] = jnp.where(lane < IDX_DIM, rope_group(kw), kw * wi_scale)
    u_ref[...] = proj(wu_ref)


def _inproj(x2d, cos_tab, sin_tab, ws, tm, n_tab_blocks):
    rows, d = x2d.shape
    wq, wk, wv, wqi, wkw, wu = ws
    row_blk = lambda w: pl.BlockSpec((tm, w), lambda i: (i, 0))
    tab_blk = pl.BlockSpec((tm, LANES), lambda i: (i % n_tab_blocks, 0))
    w_blk = lambda w: _resident(w.shape, lambda i: (0, 0))
    out_shapes = [
        jax.ShapeDtypeStruct((rows, D_ATT), BF16),
        jax.ShapeDtypeStruct((rows, D_ATT), F32),
        jax.ShapeDtypeStruct((rows, D_ATT), BF16),
        jax.ShapeDtypeStruct((rows, D_ATT), F32),
        jax.ShapeDtypeStruct((rows, D_ATT), BF16),
        jax.ShapeDtypeStruct((rows, D_ATT), BF16),
        jax.ShapeDtypeStruct((rows, LANES), F32),
        jax.ShapeDtypeStruct((rows, D_SSM), F32),
    ]
    wi_scale = (N_IDX_HEADS ** -0.5) * (IDX_DIM ** -0.5)
    return pl.pallas_call(
        functools.partial(_inproj_kernel, wi_scale=wi_scale),
        grid=(rows // tm,),
        in_specs=[row_blk(d), tab_blk, tab_blk, w_blk(wq), w_blk(wk), w_blk(wv), w_blk(wqi), w_blk(wkw), w_blk(wu)],
        out_specs=[row_blk(s.shape[1]) for s in out_shapes],
        out_shape=out_shapes,
        compiler_params=_params("parallel"),
    )(x2d, cos_tab, sin_tab, wq, wk, wv, wqi, wkw, wu)


def _attn_kernel(q_ref, qi_ref, kw_ref, kmax_ref, kiki_ref, k_ref, v_ref, o_ref,
                 key_s, keyhi_s, qm_s, qim_s, wi_s, mb_s, bias_s, l_s, acc_s, m2_s, l2_s, lga_s, lgb_s,
                 *, tq, tk, group, n_keys, past, k_keep):
    q0 = pl.program_id(1) * tq
    n_pairs = N_HEADS // 2
    n_chunks = tk // LANES
    row = lax.broadcasted_iota(I32, (tq, 1), 0)
    q_lim = jnp.minimum(((past + q0 + row) // CHUNK + 1) * CHUNK, n_keys)
    q_lim_b = jnp.broadcast_to(q_lim, (tq, LANES))
    blk_lim = jnp.minimum(((past + q0 + tq - 1) // CHUNK + 1) * CHUNK, n_keys)
    nk = (blk_lim + tk - 1) // tk
    lane_q = lax.broadcasted_iota(I32, (tq, LANES), 1)
    lo_half = lane_q < HEAD_DIM
    nt = (((1,), (1,)), ((), ()))
    pair_of = lambda h: slice((h // 2) * LANES, (h // 2 + 1) * LANES)
    chunk_of = lambda c: slice(c * LANES, (c + 1) * LANES)

    kw = kw_ref[0]
    kmax = kmax_ref[0]
    halves = (slice(0, tq), slice(tq, 2 * tq))
    for pr in range(n_pairs):
        qi_pair = qi_ref[0, :, pair_of(2 * pr)]
        q_pair = q_ref[0, :, pair_of(2 * pr)]
        for half, keep in enumerate((lo_half, jnp.logical_not(lo_half))):
            h = 2 * pr + half
            qim_s[pr, halves[half]] = jnp.where(keep, qi_pair, jnp.zeros_like(qi_pair))
            qm = jnp.where(keep, q_pair, jnp.zeros_like(q_pair))
            qm_s[pr, halves[half]] = qm
            wi_s[h] = jnp.broadcast_to(kw[:, IDX_DIM + h:IDX_DIM + h + 1], (tq, LANES))
            qf = qm.astype(F32)
            q_norm = jnp.sqrt(jnp.sum(qf * qf, axis=1, keepdims=True))
            mb_s[h] = jnp.broadcast_to(q_norm * kmax[:, h:h + 1], (tq, LANES))

    def key_base(kt, c):
        return pl.multiple_of(kt * tk + c * LANES, LANES)

    def tile_start(kt):
        return pl.multiple_of(jnp.minimum(kt, nk - 1) * tk, tk)

    def two_stage_loop(produce, consume):
        produce(0, lga_s)

        def body(j, carry):
            produce(2 * j + 1, lgb_s)
            consume(2 * j, lga_s)
            produce(2 * j + 2, lga_s)
            consume(2 * j + 1, lgb_s)
            return carry

        lax.fori_loop(0, (nk + 1) // 2, body, 0)

    def produce_scores(kt, buf):
        kiki = kiki_ref[0, pl.ds(tile_start(kt), tk), :]
        for pr in range(n_pairs):
            buf[pr] = lax.dot_general(qim_s[pr], kiki, nt, preferred_element_type=F32)

    def consume_scores(kt, buf):
        ktc = jnp.minimum(kt, nk - 1)
        score = [jnp.zeros((tq, LANES), F32) for _ in range(n_chunks)]
        for pr in range(n_pairs):
            for half in range(2):
                w = wi_s[2 * pr + half]
                for c in range(n_chunks):
                    score[c] = score[c] + w * jnp.maximum(buf[pr, halves[half], chunk_of(c)], 0.0)
        for c in range(n_chunks):
            kb = key_base(ktc, c)
            bits = pltpu.bitcast(jnp.where(kb + lane_q < q_lim_b, score[c], NEG_INF), I32)
            key_s[:, pl.ds(kb, LANES)] = bits ^ ((bits >> 31) & 0x7FFFFFFF)
            keyhi_s[:, pl.ds(kb, LANES)] = pltpu.bitcast(bits & HIGH16, F32).astype(BF16)

    two_stage_loop(produce_scores, consume_scores)

    n_groups = (nk + group - 1) // group

    def fill_tile(kt, carry):
        for c in range(n_chunks):
            kb = key_base(kt, c)
            key_s[:, pl.ds(kb, LANES)] = jnp.full((tq, LANES), NEG_INF_KEY, I32)
            keyhi_s[:, pl.ds(kb, LANES)] = jnp.full((tq, LANES), NEG_INF, BF16)
        return carry

    lax.fori_loop(nk, n_groups * group, fill_tile, 0)

    def group_chunks(g_idx):
        return [pl.multiple_of(g_idx * (group * tk) + c * LANES, LANES) for c in range(group * n_chunks)]

    def count(pred):
        def body(g_idx, cnt):
            for kb in group_chunks(g_idx):
                cnt = cnt + pred(key_s[:, pl.ds(kb, LANES)], kb).astype(I32)
            return cnt
        cnt = lax.fori_loop(0, n_groups, body, jnp.zeros((tq, LANES), I32))
        return jnp.sum(cnt, axis=1, keepdims=True)

    def count_ge(c):
        cb = jnp.broadcast_to(c, (tq, LANES))
        return count(lambda key, kb: key >= cb)

    def count_ge_high(cands):
        def lanes_of(c):
            c_bits = (c ^ ((c >> 31) & 0x7FFFFFFF)) & HIGH16
            c_bits = jnp.where(jnp.logical_and(c_bits > 0, c_bits < MIN_NORMAL_BITS), MIN_NORMAL_BITS, c_bits)
            return jnp.broadcast_to(pltpu.bitcast(c_bits, F32).astype(BF16), (tq, LANES))

        cbs = [lanes_of(c) for c in cands]
        one, nil = jnp.ones((tq, LANES), BF16), jnp.zeros((tq, LANES), BF16)

        def body(g_idx, cnts):
            cnts = list(cnts)
            for kb in group_chunks(g_idx):
                x = keyhi_s[:, pl.ds(kb, LANES)]
                for i, cb in enumerate(cbs):
                    cnts[i] = cnts[i] + jnp.where(x >= cb, one, nil)
            return tuple(cnts)
        cnts = lax.fori_loop(0, n_groups, body, (nil,) * len(cbs))
        return [jnp.sum(c.astype(F32), axis=1, keepdims=True).astype(I32) for c in cnts]

    def visit(cand, cnt, st):
        t, thr, done, at_t = st
        t = jnp.where(cnt >= k_keep, cand, t)
        at_t = jnp.where(cnt >= k_keep, cnt, at_t)
        newly = jnp.logical_and(cnt == k_keep, done == 0)
        thr = jnp.where(newly, jnp.maximum(cand - 1, NEG_INF_KEY), thr)
        done = jnp.where(newly, 1, done)
        return t, thr, done, at_t

    def search_loop(n_trips, trip, state):
        def cond(st):
            return jnp.logical_and(st[0] < n_trips, jnp.min(st[3]) == 0)
        return lax.while_loop(cond, lambda st: (st[0] + 1,) + trip(st[0], st[1:]), (jnp.int32(0),) + state)[1:]

    def search(counter, first_bit, n_bits, per_trip, state):
        def trip(step, st):
            for i in range(per_trip):
                cand = st[0] + jnp.left_shift(jnp.int32(1), first_bit - i - step * per_trip)
                st = visit(cand, counter(cand), st)
            return st
        return search_loop(n_bits // per_trip, trip, state)

    zero = jnp.zeros((tq, 1), I32)
    high_state = search(lambda c: count_ge_high([c])[0], 31, 16, 16,
                        (jnp.full((tq, 1), INT_MIN, I32), zero, zero, zero))
    t_high = high_state[0]
    lane_sum = lambda a: jnp.sum(a.astype(F32), axis=1, keepdims=True).astype(I32)

    def low_half(_):
        above, from_bucket = count_ge_high([t_high + 2 ** 16, t_high])
        members = from_bucket - above
        bucket = jnp.broadcast_to(t_high, (tq, LANES))

        def keep_largest(g_idx, kept):
            kept = list(kept)
            for kb in group_chunks(g_idx):
                key = key_s[:, pl.ds(kb, LANES)]
                v = jnp.where((key & HIGH16) == bucket, (key & 0xFFFF) + 1, 0)
                for i in range(LANE_KEEP):
                    kept[i], v = jnp.maximum(kept[i], v), jnp.minimum(kept[i], v)
            return tuple(kept)

        nil = jnp.zeros((tq, LANES), I32)
        kept = lax.fori_loop(0, n_groups, keep_largest, (nil,) * LANE_KEEP)
        count_low = lambda pred: lane_sum(functools.reduce(jnp.add, [pred(m).astype(I32) for m in kept]))
        caught_all = jnp.min((count_low(lambda m: m > 0) == members).astype(I32)) == 1

        def from_lanes(_):
            want_low = k_keep - above
            low = zero
            for step in range(LOW_VALUE_BITS // LOW_STEP_BITS):
                unit = 1 << (LOW_VALUE_BITS - LOW_STEP_BITS * (step + 1))
                taken = zero
                low_lanes = jnp.broadcast_to(low, (tq, LANES))
                for mult in range(1, 2 ** LOW_STEP_BITS):
                    cand = low_lanes + mult * unit
                    taken = taken + (count_low(lambda m: m >= cand) >= want_low).astype(I32)
                low = low + taken * unit
            low = jnp.maximum(low, 1)
            low_b = jnp.broadcast_to(low, (tq, LANES))
            return t_high + low - 1, high_state[1], high_state[2], above + count_low(lambda m: m >= low_b)

        return lax.cond(caught_all, from_lanes, lambda _: search(count_ge, 15, 16, SEARCH_UNROLL, high_state), 0)

    t_fin, thr_early, done, at_t = lax.cond(jnp.min(high_state[2]) == 0, low_half, lambda _: high_state, 0)
    t_fin = jnp.maximum(t_fin, NEG_INF_KEY)

    def resolve_ties(_):
        tb = jnp.broadcast_to(t_fin, (tq, LANES))
        need = jnp.logical_and(jnp.logical_and(done == 0, at_t > k_keep), t_fin > NEG_INF_KEY)

        def cut_search(_):
            want = (k_keep - count_ge(t_fin + 1)).astype(F32)
            upper = jnp.where(lax.broadcasted_iota(I32, (tk, tk), 0) <= lax.broadcasted_iota(I32, (tk, tk), 1),
                              1.0, 0.0).astype(BF16)

            def body(g_idx, st):
                before, last = st
                bases = group_chunks(g_idx)
                for i in range(group):
                    tile_bases = bases[i * n_chunks:(i + 1) * n_chunks]
                    hits = [key_s[:, pl.ds(kb, LANES)] == tb for kb in tile_bases]
                    ones = jnp.concatenate([jnp.where(h, 1.0, 0.0) for h in hits], axis=1)
                    rank = before + jnp.dot(ones.astype(BF16), upper, preferred_element_type=F32)
                    for c, kb in enumerate(tile_bases):
                        take = jnp.logical_and(hits[c], rank[:, chunk_of(c)] <= want)
                        last = jnp.maximum(last, jnp.where(take, kb + lane_q, -1))
                    before = before + jnp.sum(ones, axis=1, keepdims=True)
                return before, last

            _, last = lax.fori_loop(0, n_groups, body, (jnp.zeros((tq, 1), F32), jnp.full((tq, LANES), -1, I32)))
            return jnp.max(last, axis=1, keepdims=True)

        cut = lax.cond(jnp.max(need.astype(I32)) > 0, cut_search, lambda _: zero, 0)
        all_ties = jnp.where(t_fin > NEG_INF_KEY, jnp.int32(2 ** 30), jnp.int32(-1))
        return jnp.where(need, cut, all_ties)

    cut_full = lax.cond(jnp.min(done) == 0, resolve_ties, lambda _: jnp.full((tq, 1), -1, I32), 0)
    thr = jnp.where(done == 1, thr_early, t_fin)
    cut = jnp.where(done == 1, -1, cut_full)
    thr_b = jnp.broadcast_to(thr, (tq, LANES))
    cut_b = jnp.broadcast_to(cut, (tq, LANES))

    def tile_bias(kt, valid):
        thr_t = jnp.where(valid, thr_b, jnp.int32(2 ** 31 - 1))
        cut_t = jnp.where(valid, cut_b, jnp.int32(-1))
        for c in range(n_chunks):
            kb = key_base(kt, c)
            key = key_s[:, pl.ds(kb, LANES)]
            sel = jnp.logical_or(key > thr_t, jnp.logical_and(key == thr_t, kb + lane_q <= cut_t))
            bias_s[:, chunk_of(c)] = jnp.where(sel, 0.0, NEG_INF)

    def emit(l_of_head):
        for pr in range(n_pairs):
            even = acc_s[pr, halves[0]] / l_of_head(2 * pr)
            odd = acc_s[pr, halves[1]] / l_of_head(2 * pr + 1)
            o_ref[0, :, pair_of(2 * pr)] = jnp.where(lo_half, even, odd).astype(o_ref.dtype)

    l_s[...] = jnp.zeros(l_s.shape, F32)
    acc_s[...] = jnp.zeros(acc_s.shape, F32)

    def produce_logits(kt, buf):
        k0 = tile_start(kt)
        for pr in range(n_pairs):
            buf[pr] = lax.dot_general(qm_s[pr], k_ref[0, pl.ds(k0, tk), pair_of(2 * pr)], nt,
                                      preferred_element_type=F32)

    def consume_logits(kt, buf):
        k0 = tile_start(kt)
        tile_bias(jnp.minimum(kt, nk - 1), kt < nk)
        for pr in range(n_pairs):
            p = []
            for half in range(2):
                h = 2 * pr + half
                shift = mb_s[h]
                l_part = l_s[h]
                p_row = []
                for c in range(n_chunks):
                    e = jnp.exp(buf[pr, halves[half], chunk_of(c)] + bias_s[:, chunk_of(c)] - shift)
                    l_part = l_part + e
                    p_row.append(e.astype(BF16))
                l_s[h] = l_part
                p.append(jnp.concatenate(p_row, axis=1))
            acc_s[pr] += jnp.dot(jnp.concatenate(p, axis=0), v_ref[0, pl.ds(k0, tk), pair_of(2 * pr)],
                                 preferred_element_type=F32)

    two_stage_loop(produce_logits, consume_logits)
    l_rows = [jnp.sum(l_s[h], axis=1, keepdims=True) for h in range(N_HEADS)]
    l_min = functools.reduce(jnp.minimum, [jnp.min(l) for l in l_rows])
    well_scaled = l_min >= SOFTMAX_SUM_FLOOR

    @pl.when(well_scaled)
    def _emit_bound():
        emit(lambda h: l_rows[h])

    @pl.when(jnp.logical_not(well_scaled))
    def _running_max():
        m2_s[...] = jnp.full(m2_s.shape, NEG_INF, F32)
        l2_s[...] = jnp.zeros(l2_s.shape, F32)
        acc_s[...] = jnp.zeros(acc_s.shape, F32)

        def online_tile(kt, carry):
            k0 = pl.multiple_of(kt * tk, tk)
            tile_bias(kt, True)
            for h in range(N_HEADS):
                rows = halves[h % 2]
                logits = lax.dot_general(qm_s[h // 2, rows], k_ref[0, pl.ds(k0, tk), pair_of(h)], nt,
                                         preferred_element_type=F32) + bias_s[...]
                m_old = m2_s[h]
                m_new = jnp.maximum(m_old, jnp.max(logits, axis=1, keepdims=True))
                m_safe = jnp.where(m_new == NEG_INF, 0.0, m_new)
                alpha = jnp.exp(m_old - m_safe)
                p = jnp.exp(logits - m_safe)
                l2_s[h] = alpha * l2_s[h] + jnp.sum(p, axis=1, keepdims=True)
                acc_s[h // 2, rows] = alpha * acc_s[h // 2, rows] + jnp.dot(
                    p.astype(BF16), v_ref[0, pl.ds(k0, tk), pair_of(h)], preferred_element_type=F32)
                m2_s[h] = m_new
            return carry

        lax.fori_loop(0, nk, online_tile, 0)
        emit(lambda h: l2_s[h])


def _pick_key_tile(l_pad, k_keep):
    for tk in (256, 384, 512, 640, 768, 896, 1024):
        if l_pad % tk == 0 and tk >= k_keep:
            return tk
    return l_pad


def _attention(q, qi, kw, kmax, kiki, k_all, v_all, *, n_keys, past, k_keep, tq):
    b, t, _ = q.shape
    l_pad = k_all.shape[1]
    tk = _pick_key_tile(l_pad, k_keep)
    assert l_pad % tk == 0 and tk >= k_keep and t % tq == 0
    assert l_pad // LANES <= 256
    group = max(g for g in (4, 2, 1) if l_pad % (g * tk) == 0)
    qblk = lambda w: pl.BlockSpec((1, tq, w), lambda bi, i: (bi, i, 0))
    kvblk = lambda w: _resident((1, l_pad, w), lambda bi, i: (bi, 0, 0))
    per_head = lambda w, dt: pltpu.VMEM((N_HEADS, tq, w), dt)
    per_pair = lambda dt: pltpu.VMEM((N_HEADS // 2, 2 * tq, LANES), dt)
    return pl.pallas_call(
        functools.partial(_attn_kernel, tq=tq, tk=tk, group=group, n_keys=n_keys, past=past, k_keep=k_keep),
        grid=(b, t // tq),
        in_specs=[qblk(D_ATT), qblk(D_ATT), qblk(LANES), pl.BlockSpec((1, 1, LANES), lambda bi, i: (bi, 0, 0)),
                  kvblk(LANES), kvblk(D_ATT), kvblk(D_ATT)],
        out_specs=qblk(D_ATT),
        out_shape=jax.ShapeDtypeStruct((b, t, D_ATT), BF16),
        scratch_shapes=[
            pltpu.VMEM((tq, l_pad), I32),
            pltpu.VMEM((tq, l_pad), BF16),
            per_pair(BF16),
            per_pair(BF16),
            per_head(LANES, F32),
            per_head(LANES, F32),
            pltpu.VMEM((tq, tk), F32),
            per_head(LANES, F32),
            per_pair(F32),
            per_head(1, F32),
            per_head(1, F32),
            pltpu.VMEM((N_HEADS // 2, 2 * tq, tk), F32),
            pltpu.VMEM((N_HEADS // 2, 2 * tq, tk), F32),
        ],
        compiler_params=_params("arbitrary", "arbitrary"),
    )(q, qi, kw, kmax, kiki, k_all, v_all)


def _cmul(ar, ai, br, bi):
    return ar * br - ai * bi, ar * bi + ai * br


def _s5_kernel(u_ref, h0re_ref, h0im_ref, are_ref, aim_ref, ldt_ref, bre_ref, bim_ref, cre_ref, cim_ref,
               dskip_ref, wglu_ref, bglu_ref, y_ref, hre_ref, him_ref,
               bbar_s, lvl_s, pw_s, h_s, *, ts, lane_chunk):
    n = N_STATE
    t_idx = pl.program_id(1)
    first = jnp.logical_and(pl.program_id(0) == 0, t_idx == 0)

    @pl.when(first)
    def _discretise():
        dt = jnp.exp(ldt_ref[...])
        ar, ai = are_ref[...], aim_ref[...]
        mag = jnp.exp(dt * ar)
        abr, abi = mag * jnp.cos(dt * ai), mag * jnp.sin(dt * ai)
        den = ar * ar + ai * ai
        nr, ni = abr - 1.0, abi
        f_re, f_im = (nr * ar + ni * ai) / den, (ni * ar - nr * ai) / den
        reps = D_SSM // SSM_GROUP
        bre = jnp.concatenate([bre_ref[...]] * reps, axis=0)
        bim = jnp.concatenate([bim_ref[...]] * reps, axis=0)
        r_grp = lax.broadcasted_iota(I32, (D_SSM, n), 0) // SSM_GROUP
        c_grp = lax.broadcasted_iota(I32, (D_SSM, n), 1) // SSM_STATE
        same = r_grp == c_grp
        bbar_s[:, :n] = jnp.where(same, f_re * bre - f_im * bim, 0.0).astype(BF16)
        bbar_s[:, n:] = jnp.where(same, f_re * bim + f_im * bre, 0.0).astype(BF16)
        a1 = (abr, abi)
        a2 = _cmul(*a1, *a1)
        a3 = _cmul(*a2, *a1)
        a4 = _cmul(*a2, *a2)
        a5 = _cmul(*a4, *a1)
        a6 = _cmul(*a4, *a2)
        a7 = _cmul(*a4, *a3)
        a8 = _cmul(*a4, *a4)
        rows = lax.broadcasted_iota(I32, (SUBLANES, n), 0)
        for part in range(2):
            pw = jnp.zeros((SUBLANES, n), F32)
            for r, a in enumerate((a1, a2, a3, a4, a5, a6, a7, a8)):
                pw = jnp.where(rows == r, a[part], pw)
            pw_s[part] = pw
            for lv, (a, dist) in enumerate(((a1, 1), (a2, 2), (a4, 4))):
                lvl_s[lv, part] = jnp.where(rows >= dist, a[part], 0.0)

    @pl.when(t_idx == 0)
    def _load_state():
        h_s[0:SUBLANES, :] = jnp.zeros((SUBLANES, 2 * n), F32)
        h_s[SUBLANES - 1:SUBLANES, :n] = h0re_ref[0]
        h_s[SUBLANES - 1:SUBLANES, n:] = h0im_ref[0]

    u = u_ref[0]
    u_bf = u.astype(BF16)
    n_slabs = D_SSM // LANES
    slab_w = n // n_slabs
    for part in range(2):
        for j in range(n_slabs):
            cols = slice(part * n + j * slab_w, part * n + (j + 1) * slab_w)
            h_s[SUBLANES:, cols] = jnp.dot(u_bf[:, j * LANES:(j + 1) * LANES], bbar_s[j * LANES:(j + 1) * LANES, cols],
                                           preferred_element_type=F32)

    def tile_step(j, carry):
        r0 = pl.multiple_of(j * SUBLANES, SUBLANES)
        for c in range(n // lane_chunk):
            re_sl = slice(c * lane_chunk, (c + 1) * lane_chunk)
            im_sl = slice(n + c * lane_chunk, n + (c + 1) * lane_chunk)
            xr = h_s[pl.ds(r0 + SUBLANES, SUBLANES), re_sl]
            xi = h_s[pl.ds(r0 + SUBLANES, SUBLANES), im_sl]
            for lv, dist in enumerate((1, 2, 4)):
                sr, si = pltpu.roll(xr, dist, 0), pltpu.roll(xi, dist, 0)
                dr, di = _cmul(lvl_s[lv, 0, :, re_sl], lvl_s[lv, 1, :, re_sl], sr, si)
                xr, xi = xr + dr, xi + di
            prev_r = h_s[pl.ds(r0, SUBLANES), re_sl][SUBLANES - 1:SUBLANES, :]
            prev_i = h_s[pl.ds(r0, SUBLANES), im_sl][SUBLANES - 1:SUBLANES, :]
            cr, ci = _cmul(pw_s[0, :, re_sl], pw_s[1, :, re_sl], prev_r, prev_i)
            h_s[pl.ds(r0 + SUBLANES, SUBLANES), re_sl] = xr + cr
            h_s[pl.ds(r0 + SUBLANES, SUBLANES), im_sl] = xi + ci
        return carry

    lax.fori_loop(0, ts // SUBLANES, tile_step, 0)

    nt = (((1,), (1,)), ((), ()))
    y_slabs = []
    for j in range(n_slabs):
        rows, cols = slice(j * LANES, (j + 1) * LANES), slice(j * slab_w, (j + 1) * slab_w)
        h_re = h_s[SUBLANES:, cols].astype(BF16)
        h_im = h_s[SUBLANES:, n + j * slab_w:n + (j + 1) * slab_w].astype(BF16)
        y_slabs.append(lax.dot_general(h_re, cre_ref[rows, cols], nt, preferred_element_type=F32)
                       - lax.dot_general(h_im, cim_ref[rows, cols], nt, preferred_element_type=F32))
    y = jnp.concatenate(y_slabs, axis=1)
    y = y + dskip_ref[...] * u
    y = jax.nn.gelu(y)
    gate = jax.nn.sigmoid(jnp.dot(y.astype(BF16), wglu_ref[...], preferred_element_type=F32) + bglu_ref[...])
    y_ref[0] = (y * gate).astype(y_ref.dtype)

    last = h_s[ts:ts + SUBLANES, :]
    h_s[0:SUBLANES, :] = last

    @pl.when(t_idx == pl.num_programs(1) - 1)
    def _emit_state():
        hre_ref[0] = last[SUBLANES - 1:SUBLANES, :n]
        him_ref[0] = last[SUBLANES - 1:SUBLANES, n:]


def _s5(u, h0_re, h0_im, prm, ts):
    b, t, _ = u.shape
    n = N_STATE
    a_re, a_im, ldt, bre_d, bim_d, cre_t, cim_t, d_skip, w_glu, b_glu = prm
    const = lambda a: _resident(a.shape, lambda bi, i: (0,) * a.ndim)
    st_blk = pl.BlockSpec((1, 1, n), lambda bi, i: (bi, 0, 0))
    y, hre, him = pl.pallas_call(
        functools.partial(_s5_kernel, ts=ts, lane_chunk=512),
        grid=(b, t // ts),
        in_specs=[pl.BlockSpec((1, ts, D_SSM), lambda bi, i: (bi, i, 0)), st_blk, st_blk,
                  const(a_re), const(a_im), const(ldt), const(bre_d), const(bim_d), const(cre_t), const(cim_t),
                  const(d_skip), const(w_glu), const(b_glu)],
        out_specs=[pl.BlockSpec((1, ts, D_SSM), lambda bi, i: (bi, i, 0)), st_blk, st_blk],
        out_shape=[jax.ShapeDtypeStruct((b, t, D_SSM), BF16),
                   jax.ShapeDtypeStruct((b, 1, n), F32), jax.ShapeDtypeStruct((b, 1, n), F32)],
        scratch_shapes=[
            pltpu.VMEM((D_SSM, 2 * n), BF16),
            pltpu.VMEM((3, 2, SUBLANES, n), F32),
            pltpu.VMEM((2, SUBLANES, n), F32),
            pltpu.VMEM((SUBLANES + ts, 2 * n), F32),
        ],
        compiler_params=_params("arbitrary", "arbitrary"),
    )(u, h0_re, h0_im, a_re, a_im, ldt, bre_d, bim_d, cre_t, cim_t, d_skip, w_glu, b_glu)
    return y, hre, him


def _layer_norm(z, g, b):
    mu = jnp.mean(z, axis=-1, keepdims=True)
    zc = z - mu
    var = jnp.mean(zc * zc, axis=-1, keepdims=True)
    return zc * lax.rsqrt(var + LN_EPS) * g + b


def _merge_kernel(x_ref, att_ref, ssm_ref, wg_ref, bg_ref, wpa_ref, wpb_ref, wo_ref, g_ref, b_ref, o_ref, *, alpha):
    x = x_ref[...]
    d = x.shape[1]
    gates = jax.nn.sigmoid(jnp.dot(x.astype(BF16), wg_ref[...], preferred_element_type=F32) + bg_ref[...])
    mix = (gates[:, :d] * jnp.dot(att_ref[...], wpa_ref[...], preferred_element_type=F32)
           + gates[:, d:] * jnp.dot(ssm_ref[...], wpb_ref[...], preferred_element_type=F32))
    z = alpha * x + jnp.dot(mix.astype(BF16), wo_ref[...], preferred_element_type=F32)
    o_ref[...] = _layer_norm(z, g_ref[...], b_ref[...])


def _merge(x2d, att2d, ssm2d, prm, alpha, tm):
    rows, d = x2d.shape
    w_g, b_g, w_pa, w_pb, w_o, ln_g, ln_b = prm
    row_blk = lambda w: pl.BlockSpec((tm, w), lambda i: (i, 0))
    const = lambda a: _resident(a.shape, lambda i: (0, 0))
    return pl.pallas_call(
        functools.partial(_merge_kernel, alpha=alpha),
        grid=(rows // tm,),
        in_specs=[row_blk(d), row_blk(D_ATT), row_blk(D_SSM)] + [const(a) for a in prm],
        out_specs=row_blk(d),
        out_shape=jax.ShapeDtypeStruct((rows, d), F32),
        compiler_params=_params("parallel"),
    )(x2d, att2d, ssm2d, *prm)


def _ffn_kernel(x_ref, cbuf_ref, wup_ref, cw_ref, cb_ref, wdn_ref, g_ref, b_ref, o_ref, nconv_ref,
                h_s, acc_s, *, tm, d_ff, fc, alpha):
    t_idx = pl.program_id(1)
    hist = CONV_W - 1

    @pl.when(t_idx == 0)
    def _load_history():
        h_s[0:SUBLANES, :] = jnp.zeros((SUBLANES, h_s.shape[1]), F32)
        h_s[SUBLANES - hist:SUBLANES, :] = cbuf_ref[0]

    x = x_ref[0]
    x_bf = x.astype(BF16)
    n_chunks = d_ff // fc
    column_pair = lambda c: (slice(c * fc, (c + 1) * fc), slice(d_ff + c * fc, d_ff + (c + 1) * fc))

    def up_project(c):
        for cols in column_pair(c):
            h_s[SUBLANES:, cols] = jnp.dot(x_bf, wup_ref[:, cols], preferred_element_type=F32)

    def conv_act_down(c):
        conv = [cb_ref[:, cols] + sum(cw_ref[j:j + 1, cols] * h_s[SUBLANES - hist + j:SUBLANES - hist + j + tm, cols]
                                      for j in range(CONV_W)) for cols in column_pair(c)]
        act = (jax.nn.gelu(conv[0]) * conv[1]).astype(BF16)
        part = jnp.dot(act, wdn_ref[c * fc:(c + 1) * fc, :], preferred_element_type=F32)
        if c == 0:
            acc_s[...] = part
        else:
            acc_s[...] += part

    up_project(0)
    for c in range(n_chunks):
        if c + 1 < n_chunks:
            up_project(c + 1)
        conv_act_down(c)
    o_ref[0] = _layer_norm(alpha * x + acc_s[...], g_ref[...], b_ref[...])

    tail = h_s[tm:tm + SUBLANES, :]
    h_s[0:SUBLANES, :] = tail

    @pl.when(t_idx == pl.num_programs(1) - 1)
    def _emit_history():
        nconv_ref[0] = tail[SUBLANES - hist:, :]


def _ffn(x, conv_buf, prm, alpha, tm):
    b, t, d = x.shape
    w_up, conv_w, conv_b, w_dn, ln_g, ln_b = prm
    d_ff = w_dn.shape[0]
    fc = 256
    assert d_ff % fc == 0 and t % tm == 0 and tm % SUBLANES == 0
    const = lambda a: _resident(a.shape, lambda bi, i: (0, 0))
    hist_blk = pl.BlockSpec((1, CONV_W - 1, 2 * d_ff), lambda bi, i: (bi, 0, 0))
    return pl.pallas_call(
        functools.partial(_ffn_kernel, tm=tm, d_ff=d_ff, fc=fc, alpha=alpha),
        grid=(b, t // tm),
        in_specs=[pl.BlockSpec((1, tm, d), lambda bi, i: (bi, i, 0)), hist_blk] + [const(a) for a in prm],
        out_specs=[pl.BlockSpec((1, tm, d), lambda bi, i: (bi, i, 0)), hist_blk],
        out_shape=[jax.ShapeDtypeStruct((b, t, d), F32), jax.ShapeDtypeStruct((b, CONV_W - 1, 2 * d_ff), F32)],
        scratch_shapes=[pltpu.VMEM((SUBLANES + tm, 2 * d_ff), F32),
                        pltpu.VMEM((tm, d), F32)],
        compiler_params=_params("arbitrary", "arbitrary"),
    )(x, conv_buf, *prm)


def _rope_tables(pos):
    half = HEAD_DIM // 2
    inv = ROPE_THETA ** (-jnp.arange(half, dtype=F32) / half)
    ang = pos.astype(F32)[:, None] * inv[None, :]
    cos, sin = jnp.cos(ang), jnp.sin(ang)
    cos_tab = jnp.tile(cos, (1, LANES // half))
    sin_tab = jnp.tile(jnp.concatenate([-sin, sin], axis=1), (1, LANES // HEAD_DIM))
    return cos_tab, sin_tab


def _layer_weights(w_in, b_gate, w_pa, w_pb, w_o, a_re, a_im, log_dt, b_re, b_im, c_re, c_im, d_skip, w_glu,
                   b_glu, ln1_g, ln1_b, w_up, conv_w, conv_b, w_down, ln2_g, ln2_b):
    d = w_in.shape[0]
    o = 0
    pieces = []
    for w in (D_ATT, D_ATT, D_ATT, N_IDX_HEADS * IDX_DIM, IDX_DIM + N_IDX_HEADS, D_SSM, 2 * d):
        pieces.append(w_in[:, o:o + w])
        o += w
    wq, wk, wv, wqi, wkw, wu, wg = pieces
    wkw = jnp.pad(wkw, ((0, 0), (0, LANES - wkw.shape[1])))
    inproj = tuple(w.astype(BF16) for w in (wq, wk, wv, wqi, wkw, wu))
    n = N_STATE
    row = lambda a: a.reshape(1, -1).astype(F32)
    same_group = jnp.eye(N_GROUPS, dtype=bool)[:, None, :, None]
    blockdiag_t = lambda c: jnp.where(same_group, c[:, :, None, :], 0.0).reshape(D_SSM, n).astype(BF16)
    dense = lambda bm: jnp.transpose(bm, (2, 0, 1)).reshape(SSM_GROUP, n).astype(F32)
    s5 = (row(a_re), row(a_im), row(jnp.repeat(log_dt, SSM_STATE)), dense(b_re), dense(b_im),
          blockdiag_t(c_re), blockdiag_t(c_im), row(d_skip), w_glu.astype(BF16), row(b_glu))
    merge = (wg.astype(BF16), row(b_gate), w_pa.astype(BF16), w_pb.astype(BF16), w_o.astype(BF16),
             row(ln1_g), row(ln1_b))
    ffn = (w_up.astype(BF16), conv_w.astype(F32), row(conv_b), w_down.astype(BF16), row(ln2_g), row(ln2_b))
    return inproj, s5, merge, ffn


def _largest_tile(n, cap):
    t = min(n, cap)
    while n % t:
        t -= SUBLANES
    return t


def _trunk_layer(x, past_k, past_v, past_ki, h0_re, h0_im, conv_buf, weights, alpha):
    bn, t, d = x.shape
    p = 0 if past_k is None else past_k.shape[1]
    n_keys = p + t
    k_keep = min(TOPK_MAX, n_keys // 4)
    inproj_w, s5_w, merge_w, ffn_w = weights
    rows = bn * t
    tm = _largest_tile(rows, 512)

    cos_tab, sin_tab = _rope_tables(p + jnp.arange(t))
    if t % tm == 0:
        n_tab_blocks = t // tm
    else:
        cos_tab, sin_tab = jnp.tile(cos_tab, (bn, 1)), jnp.tile(sin_tab, (bn, 1))
        n_tab_blocks = rows // tm
    q, k_f, k_b, v_f, v_b, qi, kw, u = _inproj(x.reshape(rows, d), cos_tab, sin_tab, inproj_w, tm, n_tab_blocks)

    r3 = lambda a: a.reshape(bn, t, a.shape[-1])
    ki_b = r3(kw)[:, :, :IDX_DIM].astype(BF16)
    k_all, v_all = r3(k_b), r3(v_b)
    if p:
        k_all = jnp.concatenate([past_k.reshape(bn, p, D_ATT).astype(BF16), k_all], axis=1)
        v_all = jnp.concatenate([past_v.reshape(bn, p, D_ATT).astype(BF16), v_all], axis=1)
        ki_b = jnp.concatenate([past_ki.astype(BF16), ki_b], axis=1)
    l_pad = -(-n_keys // LANES) * LANES
    l_pad = max(l_pad, 2 * LANES)
    pad = lambda a: jnp.pad(a, ((0, 0), (0, l_pad - n_keys), (0, 0)))
    kiki = jnp.concatenate([ki_b, ki_b], axis=-1)
    k_heads = k_all.astype(F32).reshape(bn, n_keys, N_HEADS, HEAD_DIM)
    kmax = jnp.sqrt(jnp.max(jnp.sum(k_heads * k_heads, axis=-1), axis=1))
    kmax = jnp.pad(kmax, ((0, 0), (0, LANES - N_HEADS))).reshape(bn, 1, LANES)
    tq = _largest_tile(t, 128)
    att = _attention(r3(q), r3(qi), r3(kw), kmax, pad(kiki), pad(k_all), pad(v_all),
                     n_keys=n_keys, past=p, k_keep=k_keep, tq=tq)

    ts = _largest_tile(t, 256)
    ssm, h_re, h_im = _s5(r3(u), h0_re.reshape(bn, 1, N_STATE), h0_im.reshape(bn, 1, N_STATE), s5_w, ts)

    x1 = _merge(x.reshape(rows, d), att.reshape(rows, D_ATT), ssm.reshape(rows, D_SSM), merge_w, alpha, tm)
    x2, new_conv = _ffn(x1.reshape(bn, t, d), conv_buf, ffn_w, alpha, _largest_tile(t, 256))

    k_out = r3(k_f).reshape(bn, t, N_HEADS, HEAD_DIM)
    v_out = r3(v_f).reshape(bn, t, N_HEADS, HEAD_DIM)
    ki_out = r3(kw)[:, :, :IDX_DIM]
    return (x2, k_out, v_out, ki_out, h_re.reshape(bn, N_GROUPS, SSM_STATE), h_im.reshape(bn, N_GROUPS, SSM_STATE),
            new_conv)


def kernel(x_prompt, x_sample, cache_k, cache_v, cache_idx_k, state_ssm_re, state_ssm_im, state_conv, w_in, b_gate, w_pa, w_pb, w_o, a_re, a_im, log_dt, b_re, b_im, c_re, c_im, d_skip, w_glu, b_glu, ln1_g, ln1_b, w_up, conv_w, conv_b, w_down, ln2_g, ln2_b):
    depth = w_in.shape[0]
    alpha = (2 * depth) ** 0.25
    bp = x_prompt.shape[0]
    d_ff2 = w_up.shape[2]
    zero_h = jnp.zeros((bp, N_GROUPS, SSM_STATE), F32)
    zero_conv = jnp.zeros((bp, CONV_W - 1, d_ff2), F32)
    layer_params = (w_in, b_gate, w_pa, w_pb, w_o, a_re, a_im, log_dt, b_re, b_im, c_re, c_im, d_skip, w_glu, b_glu,
                    ln1_g, ln1_b, w_up, conv_w, conv_b, w_down, ln2_g, ln2_b)
    xp, xs = x_prompt, x_sample
    outs_p, outs_s = [], []
    for l in range(depth):
        weights = _layer_weights(*(a[l] for a in layer_params))
        rp = _trunk_layer(xp, None, None, None, zero_h, zero_h, zero_conv, weights, alpha)
        rs = _trunk_layer(xs, cache_k[l], cache_v[l], cache_idx_k[l], state_ssm_re[l].astype(F32),
                          state_ssm_im[l].astype(F32), state_conv[l], weights, alpha)
        xp, xs = rp[0], rs[0]
        outs_p.append(rp[1:])
        outs_s.append(rs[1:])
    stack = lambda outs: tuple(jnp.stack([o[j] for o in outs], axis=0) for j in range(6))
    return (xp, xs) + stack(outs_p) + stack(outs_s)
```

```python
import functools

import jax
import jax.numpy as jnp
from jax import lax
from jax.experimental import pallas as pl
from jax.experimental.pallas import tpu as pltpu

CHUNK = 64
N_HEADS = 8
HEAD_DIM = 64
D_ATT = N_HEADS * HEAD_DIM
N_IDX_HEADS = 8
IDX_DIM = 64
TOPK_MAX = 256
D_SSM = 512
SSM_GROUP = 16
N_GROUPS = D_SSM // SSM_GROUP
SSM_STATE = 64
N_STATE = N_GROUPS * SSM_STATE
CONV_W = 3
ROPE_THETA = 10000.0
LN_EPS = 1e-5
LOG2_E = 1.4426950408889634

LANES = 128
SUBLANES = 8
VMEM_LIMIT = 60 * 1024 * 1024

F32 = jnp.float32
BF16 = jnp.bfloat16
I32 = jnp.int32
NEG_INF = float("-inf")
INT_MIN = -(2 ** 31)
HIGH16 = -(2 ** 16)
MIN_NORMAL_BITS = 0x00800000


def _f32_key_of_neg_inf():
    bits = 0xFF800000
    signed = bits - (1 << 32)
    return signed ^ ((signed >> 31) & 0x7FFFFFFF)


NEG_INF_KEY = _f32_key_of_neg_inf()
TILES_PER_TRIP = 4
SEARCH_UNROLL = 4
LOW_VALUE_BITS = 18
LOW_STEP_BITS = 2
LANE_KEEP = 3
SOFTMAX_SUM_FLOOR = 1e-30


def _resident(block_shape, index_map):
    return pl.BlockSpec(block_shape, index_map, pipeline_mode=pl.Buffered(1))


def _params(*sem):
    return pltpu.CompilerParams(dimension_semantics=sem, vmem_limit_bytes=VMEM_LIMIT)


def _inproj_kernel(x_ref, cos_ref, sin_ref, wq_ref, wk_ref, wv_ref, wqi_ref, wkw_ref, wu_ref,
                   q_ref, kf_ref, kb_ref, vf_ref, vb_ref, qi_ref, kw_ref, u_ref, *, wi_scale):
    x = x_ref[...].astype(BF16)
    cos = cos_ref[...]
    sin = sin_ref[...]
    tm = x.shape[0]
    lane = lax.broadcasted_iota(I32, (tm, LANES), 1)
    first_half = (lane % HEAD_DIM) < (HEAD_DIM // 2)

    def proj(w_ref):
        return jnp.dot(x, w_ref[...], preferred_element_type=F32)

    def rope_group(y):
        rot = jnp.where(first_half, pltpu.roll(y, LANES - HEAD_DIM // 2, 1), pltpu.roll(y, HEAD_DIM // 2, 1))
        return y * cos + rot * sin

    def rope(y):
        return jnp.concatenate([rope_group(y[:, c * LANES:(c + 1) * LANES]) for c in range(y.shape[1] // LANES)],
                               axis=1)

    q = rope(proj(wq_ref))
    q_ref[...] = (q * (HEAD_DIM ** -0.5 * LOG2_E)).astype(BF16)
    k = rope(proj(wk_ref))
    kf_ref[...] = k
    kb_ref[...] = k.astype(BF16)
    v = proj(wv_ref)
    vf_ref[...] = v
    vb_ref[...] = v.astype(BF16)
    qi_ref[...] = rope(proj(wqi_ref)).astype(BF16)
    kw = proj(wkw_ref)
    kw_ref[...] = jnp.where(lane < IDX_DIM, rope_group(kw), kw * wi_scale)
    u_ref[...] = proj(wu_ref)


def _inproj(x2d, cos_tab, sin_tab, ws, tm, n_tab_blocks):
    rows, d = x2d.shape
    wq, wk, wv, wqi, wkw, wu = ws
    row_blk = lambda w: pl.BlockSpec((tm, w), lambda i: (i, 0))
    tab_blk = pl.BlockSpec((tm, LANES), lambda i: (i % n_tab_blocks, 0))
    w_blk = lambda w: _resident(w.shape, lambda i: (0, 0))
    out_shapes = [
        jax.ShapeDtypeStruct((rows, D_ATT), BF16),
        jax.ShapeDtypeStruct((rows, D_ATT), F32),
        jax.ShapeDtypeStruct((rows, D_ATT), BF16),
        jax.ShapeDtypeStruct((rows, D_ATT), F32),
        jax.ShapeDtypeStruct((rows, D_ATT), BF16),
        jax.ShapeDtypeStruct((rows, D_ATT), BF16),
        jax.ShapeDtypeStruct((rows, LANES), F32),
        jax.ShapeDtypeStruct((rows, D_SSM), F32),
    ]
    wi_scale = (N_IDX_HEADS ** -0.5) * (IDX_DIM ** -0.5)
    return pl.pallas_call(
        functools.partial(_inproj_kernel, wi_scale=wi_scale),
        grid=(rows // tm,),
        in_specs=[row_blk(d), tab_blk, tab_blk, w_blk(wq), w_blk(wk), w_blk(wv), w_blk(wqi), w_blk(wkw), w_blk(wu)],
        out_specs=[row_blk(s.shape[1]) for s in out_shapes],
        out_shape=out_shapes,
        compiler_params=_params("parallel"),
    )(x2d, cos_tab, sin_tab, wq, wk, wv, wqi, wkw, wu)


def _attn_kernel(q_ref, qi_ref, kw_ref, kmax_ref, kiki_ref, k_ref, v_ref, o_ref,
                 key_s, keyhi_s, qm_s, qim_s, wi_s, mb_s, bias_s, l_s, acc_s, m2_s, l2_s, lga_s, lgb_s,
                 *, tq, tk, group, n_keys, past, k_keep):
    q0 = pl.program_id(1) * tq
    n_pairs = N_HEADS // 2
    n_chunks = tk // LANES
    row = lax.broadcasted_iota(I32, (tq, 1), 0)
    q_lim = jnp.minimum(((past + q0 + row) // CHUNK + 1) * CHUNK, n_keys)
    q_lim_b = jnp.broadcast_to(q_lim, (tq, LANES))
    blk_lim = jnp.minimum(((past + q0 + tq - 1) // CHUNK + 1) * CHUNK, n_keys)
    nk = (blk_lim + tk - 1) // tk
    lane_q = lax.broadcasted_iota(I32, (tq, LANES), 1)
    lo_half = lane_q < HEAD_DIM
    nt = (((1,), (1,)), ((), ()))
    pair_of = lambda h: slice((h // 2) * LANES, (h // 2 + 1) * LANES)
    chunk_of = lambda c: slice(c * LANES, (c + 1) * LANES)

    kw = kw_ref[0]
    kmax = kmax_ref[0]
    halves = (slice(0, tq), slice(tq, 2 * tq))
    for pr in range(n_pairs):
        qi_pair = qi_ref[0, :, pair_of(2 * pr)]
        q_pair = q_ref[0, :, pair_of(2 * pr)]
        for half, keep in enumerate((lo_half, jnp.logical_not(lo_half))):
            h = 2 * pr + half
            qim_s[pr, halves[half]] = jnp.where(keep, qi_pair, jnp.zeros_like(qi_pair))
            qm = jnp.where(keep, q_pair, jnp.zeros_like(q_pair))
            qm_s[pr, halves[half]] = qm
            wi_s[h] = jnp.broadcast_to(kw[:, IDX_DIM + h:IDX_DIM + h + 1], (tq, LANES))
            qf = qm.astype(F32)
            q_norm = jnp.sqrt(jnp.sum(qf * qf, axis=1, keepdims=True))
            mb_s[h] = jnp.broadcast_to(q_norm * kmax[:, h:h + 1], (tq, LANES))

    def key_base(kt, c):
        return pl.multiple_of(kt * tk + c * LANES, LANES)

    def tile_start(kt):
        return pl.multiple_of(jnp.minimum(kt, nk - 1) * tk, tk)

    def two_stage_loop(produce, consume):
        produce(0, lga_s)

        def body(j, carry):
            for i in range(TILES_PER_TRIP):
                now, nxt = (lga_s, lgb_s) if i % 2 == 0 else (lgb_s, lga_s)
                produce(TILES_PER_TRIP * j + i + 1, nxt)
                consume(TILES_PER_TRIP * j + i, now)
            return carry

        lax.fori_loop(0, (nk + TILES_PER_TRIP - 1) // TILES_PER_TRIP, body, 0)

    def produce_scores(kt, buf):
        kiki = kiki_ref[0, pl.ds(tile_start(kt), tk), :]
        for pr in range(n_pairs):
            buf[pr] = lax.dot_general(qim_s[pr], kiki, nt, preferred_element_type=F32)

    def consume_scores(kt, buf):
        ktc = jnp.minimum(kt, nk - 1)
        score = [jnp.zeros((tq, LANES), F32) for _ in range(n_chunks)]
        for pr in range(n_pairs):
            for half in range(2):
                w = wi_s[2 * pr + half]
                for c in range(n_chunks):
                    score[c] = score[c] + w * jnp.maximum(buf[pr, halves[half], chunk_of(c)], 0.0)
        for c in range(n_chunks):
            kb = key_base(ktc, c)
            bits = pltpu.bitcast(jnp.where(kb + lane_q < q_lim_b, score[c], NEG_INF), I32)
            key_s[:, pl.ds(kb, LANES)] = bits ^ ((bits >> 31) & 0x7FFFFFFF)
            keyhi_s[:, pl.ds(kb, LANES)] = pltpu.bitcast(bits & HIGH16, F32).astype(BF16)

    two_stage_loop(produce_scores, consume_scores)

    n_groups = (nk + group - 1) // group

    def fill_tile(kt, carry):
        for c in range(n_chunks):
            kb = key_base(kt, c)
            key_s[:, pl.ds(kb, LANES)] = jnp.full((tq, LANES), NEG_INF_KEY, I32)
            keyhi_s[:, pl.ds(kb, LANES)] = jnp.full((tq, LANES), NEG_INF, BF16)
        return carry

    lax.fori_loop(nk, n_groups * group, fill_tile, 0)

    def group_chunks(g_idx):
        return [pl.multiple_of(g_idx * (group * tk) + c * LANES, LANES) for c in range(group * n_chunks)]

    def count(pred):
        def body(g_idx, cnt):
            for kb in group_chunks(g_idx):
                cnt = cnt + pred(key_s[:, pl.ds(kb, LANES)], kb).astype(I32)
            return cnt
        cnt = lax.fori_loop(0, n_groups, body, jnp.zeros((tq, LANES), I32))
        return jnp.sum(cnt, axis=1, keepdims=True)

    def count_ge(c):
        cb = jnp.broadcast_to(c, (tq, LANES))
        return count(lambda key, kb: key >= cb)

    def count_ge_high(cands):
        def lanes_of(c):
            c_bits = (c ^ ((c >> 31) & 0x7FFFFFFF)) & HIGH16
            c_bits = jnp.where(c_bits > 0, jnp.maximum(c_bits, MIN_NORMAL_BITS), c_bits)
            return jnp.broadcast_to(pltpu.bitcast(c_bits, F32).astype(BF16), (tq, LANES))

        cbs = [lanes_of(c) for c in cands]
        one, nil = jnp.ones((tq, LANES), BF16), jnp.zeros((tq, LANES), BF16)

        def body(g_idx, cnts):
            cnts = list(cnts)
            for kb in group_chunks(g_idx):
                x = keyhi_s[:, pl.ds(kb, LANES)]
                for i, cb in enumerate(cbs):
                    cnts[i] = cnts[i] + jnp.where(x >= cb, one, nil)
            return tuple(cnts)
        cnts = lax.fori_loop(0, n_groups, body, (nil,) * len(cbs))
        return [jnp.sum(c.astype(F32), axis=1, keepdims=True).astype(I32) for c in cnts]

    def visit(cand, cnt, st):
        t, thr, done, at_t = st
        t = jnp.where(cnt >= k_keep, cand, t)
        at_t = jnp.where(cnt >= k_keep, cnt, at_t)
        exact = cnt == k_keep
        thr = jnp.where(done == 0, jnp.where(exact, jnp.maximum(cand - 1, NEG_INF_KEY), thr), thr)
        done = jnp.where(exact, 1, done)
        return t, thr, done, at_t

    def search_loop(n_trips, trip, state):
        def cond(st):
            return jnp.logical_and(st[0] < n_trips, jnp.min(st[3]) == 0)
        return lax.while_loop(cond, lambda st: (st[0] + 1,) + trip(st[0], st[1:]), (jnp.int32(0),) + state)[1:]

    def search(counter, first_bit, n_bits, per_trip, state):
        def trip(step, st):
            for i in range(per_trip):
                cand = st[0] + jnp.left_shift(jnp.int32(1), first_bit - i - step * per_trip)
                st = visit(cand, counter(cand), st)
            return st
        return search_loop(n_bits // per_trip, trip, state)

    zero = jnp.zeros((tq, 1), I32)
    high_state = search(lambda c: count_ge_high([c])[0], 31, 16, 16,
                        (jnp.full((tq, 1), INT_MIN, I32), zero, zero, zero))
    t_high = high_state[0]
    lane_sum = lambda a: jnp.sum(a.astype(F32), axis=1, keepdims=True).astype(I32)

    def low_half(_):
        above, from_bucket = count_ge_high([t_high + 2 ** 16, t_high])
        members = from_bucket - above
        bucket = jnp.broadcast_to(t_high, (tq, LANES))

        def keep_largest(g_idx, kept):
            kept = list(kept)
            for kb in group_chunks(g_idx):
                key = key_s[:, pl.ds(kb, LANES)]
                v = jnp.where((key & HIGH16) == bucket, (key & 0xFFFF) + 1, 0)
                for i in range(LANE_KEEP):
                    kept[i], v = jnp.maximum(kept[i], v), jnp.minimum(kept[i], v)
            return tuple(kept)

        nil = jnp.zeros((tq, LANES), I32)
        kept = lax.fori_loop(0, n_groups, keep_largest, (nil,) * LANE_KEEP)
        count_low = lambda pred: lane_sum(functools.reduce(jnp.add, [pred(m).astype(I32) for m in kept]))
        caught_all = jnp.min((count_low(lambda m: m > 0) == members).astype(I32)) == 1

        def from_lanes(_):
            want_low = k_keep - above
            low = zero
            for step in range(LOW_VALUE_BITS // LOW_STEP_BITS):
                unit = 1 << (LOW_VALUE_BITS - LOW_STEP_BITS * (step + 1))
                taken = zero
                low_lanes = jnp.broadcast_to(low, (tq, LANES))
                for mult in range(1, 2 ** LOW_STEP_BITS):
                    cand = low_lanes + mult * unit
                    taken = taken + (count_low(lambda m: m >= cand) >= want_low).astype(I32)
                low = low + taken * unit
            low = jnp.maximum(low, 1)
            low_b = jnp.broadcast_to(low, (tq, LANES))
            return t_high + low - 1, high_state[1], high_state[2], above + count_low(lambda m: m >= low_b)

        return lax.cond(caught_all, from_lanes, lambda _: search(count_ge, 15, 16, SEARCH_UNROLL, high_state), 0)

    t_fin, thr_early, done, at_t = lax.cond(jnp.min(high_state[2]) == 0, low_half, lambda _: high_state, 0)
    t_fin = jnp.maximum(t_fin, NEG_INF_KEY)

    def resolve_ties(_):
        tb = jnp.broadcast_to(t_fin, (tq, LANES))
        need = jnp.logical_and(jnp.logical_and(done == 0, at_t > k_keep), t_fin > NEG_INF_KEY)

        def cut_search(_):
            want = (k_keep - count_ge(t_fin + 1)).astype(F32)
            upper = jnp.where(lax.broadcasted_iota(I32, (tk, tk), 0) <= lax.broadcasted_iota(I32, (tk, tk), 1),
                              1.0, 0.0).astype(BF16)

            def body(g_idx, st):
                before, last = st
                bases = group_chunks(g_idx)
                for i in range(group):
                    tile_bases = bases[i * n_chunks:(i + 1) * n_chunks]
                    hits = [key_s[:, pl.ds(kb, LANES)] == tb for kb in tile_bases]
                    ones = jnp.concatenate([jnp.where(h, 1.0, 0.0) for h in hits], axis=1)
                    rank = before + jnp.dot(ones.astype(BF16), upper, preferred_element_type=F32)
                    for c, kb in enumerate(tile_bases):
                        take = jnp.logical_and(hits[c], rank[:, chunk_of(c)] <= want)
                        last = jnp.maximum(last, jnp.where(take, kb + lane_q, -1))
                    before = before + jnp.sum(ones, axis=1, keepdims=True)
                return before, last

            _, last = lax.fori_loop(0, n_groups, body, (jnp.zeros((tq, 1), F32), jnp.full((tq, LANES), -1, I32)))
            return jnp.max(last, axis=1, keepdims=True)

        cut = lax.cond(jnp.max(need.astype(I32)) > 0, cut_search, lambda _: zero, 0)
        all_ties = jnp.where(t_fin > NEG_INF_KEY, jnp.int32(2 ** 30), jnp.int32(-1))
        return jnp.where(need, cut, all_ties)

    cut_full = lax.cond(jnp.min(done) == 0, resolve_ties, lambda _: jnp.full((tq, 1), -1, I32), 0)
    thr = jnp.where(done == 1, thr_early, t_fin)
    cut = jnp.where(done == 1, -1, cut_full)
    thr_b = jnp.broadcast_to(thr, (tq, LANES))
    cut_b = jnp.broadcast_to(cut, (tq, LANES))

    def tile_bias(kt, valid):
        thr_t = jnp.where(valid, thr_b, jnp.int32(2 ** 31 - 1))
        cut_t = jnp.where(valid, cut_b, jnp.int32(-1))
        for c in range(n_chunks):
            kb = key_base(kt, c)
            key = key_s[:, pl.ds(kb, LANES)]
            bound = jnp.where(kb + lane_q <= cut_t, thr_t - 1, thr_t)
            bias_s[:, chunk_of(c)] = jnp.where(key > bound, 0.0, NEG_INF)

    def emit(l_of_head):
        for pr in range(n_pairs):
            even = acc_s[pr, halves[0]] / l_of_head(2 * pr)
            odd = acc_s[pr, halves[1]] / l_of_head(2 * pr + 1)
            o_ref[0, :, pair_of(2 * pr)] = jnp.where(lo_half, even, odd).astype(o_ref.dtype)

    l_s[...] = jnp.zeros(l_s.shape, F32)
    acc_s[...] = jnp.zeros(acc_s.shape, F32)

    def produce_logits(kt, buf):
        k0 = tile_start(kt)
        for pr in range(n_pairs):
            buf[pr] = lax.dot_general(qm_s[pr], k_ref[0, pl.ds(k0, tk), pair_of(2 * pr)], nt,
                                      preferred_element_type=F32)

    def consume_logits(kt, buf):
        k0 = tile_start(kt)
        tile_bias(jnp.minimum(kt, nk - 1), kt < nk)
        for pr in range(n_pairs):
            p = []
            for half in range(2):
                h = 2 * pr + half
                shift = mb_s[h]
                l_part = l_s[h]
                p_row = []
                for c in range(n_chunks):
                    e = jnp.exp2(buf[pr, halves[half], chunk_of(c)] + bias_s[:, chunk_of(c)] - shift)
                    l_part = l_part + e
                    p_row.append(e.astype(BF16))
                l_s[h] = l_part
                p.append(jnp.concatenate(p_row, axis=1))
            acc_s[pr] += jnp.dot(jnp.concatenate(p, axis=0), v_ref[0, pl.ds(k0, tk), pair_of(2 * pr)],
                                 preferred_element_type=F32)

    two_stage_loop(produce_logits, consume_logits)
    l_rows = [jnp.sum(l_s[h], axis=1, keepdims=True) for h in range(N_HEADS)]
    l_min = functools.reduce(jnp.minimum, [jnp.min(l) for l in l_rows])
    well_scaled = l_min >= SOFTMAX_SUM_FLOOR

    @pl.when(well_scaled)
    def _emit_bound():
        emit(lambda h: l_rows[h])

    @pl.when(jnp.logical_not(well_scaled))
    def _running_max():
        m2_s[...] = jnp.full(m2_s.shape, NEG_INF, F32)
        l2_s[...] = jnp.zeros(l2_s.shape, F32)
        acc_s[...] = jnp.zeros(acc_s.shape, F32)

        def online_tile(kt, carry):
            k0 = pl.multiple_of(kt * tk, tk)
            tile_bias(kt, True)
            for h in range(N_HEADS):
                rows = halves[h % 2]
                logits = lax.dot_general(qm_s[h // 2, rows], k_ref[0, pl.ds(k0, tk), pair_of(h)], nt,
                                         preferred_element_type=F32) + bias_s[...]
                m_old = m2_s[h]
                m_new = jnp.maximum(m_old, jnp.max(logits, axis=1, keepdims=True))
                m_safe = jnp.where(m_new == NEG_INF, 0.0, m_new)
                alpha = jnp.exp2(m_old - m_safe)
                p = jnp.exp2(logits - m_safe)
                l2_s[h] = alpha * l2_s[h] + jnp.sum(p, axis=1, keepdims=True)
                acc_s[h // 2, rows] = alpha * acc_s[h // 2, rows] + jnp.dot(
                    p.astype(BF16), v_ref[0, pl.ds(k0, tk), pair_of(h)], preferred_element_type=F32)
                m2_s[h] = m_new
            return carry

        lax.fori_loop(0, nk, online_tile, 0)
        emit(lambda h: l2_s[h])


def _pick_key_tile(l_pad, k_keep):
    for tk in (256, 384, 512, 640, 768, 896, 1024):
        if l_pad % tk == 0 and tk >= k_keep:
            return tk
    return l_pad


def _attention(q, qi, kw, kmax, kiki, k_all, v_all, *, n_keys, past, k_keep, tq):
    b, t, _ = q.shape
    l_pad = k_all.shape[1]
    tk = _pick_key_tile(l_pad, k_keep)
    assert l_pad % tk == 0 and tk >= k_keep and t % tq == 0
    assert l_pad // LANES <= 256
    group = max(g for g in (4, 2, 1) if l_pad % (g * tk) == 0)
    qblk = lambda w: pl.BlockSpec((1, tq, w), lambda bi, i: (bi, i, 0))
    kvblk = lambda w: _resident((1, l_pad, w), lambda bi, i: (bi, 0, 0))
    per_head = lambda w, dt: pltpu.VMEM((N_HEADS, tq, w), dt)
    per_pair = lambda dt: pltpu.VMEM((N_HEADS // 2, 2 * tq, LANES), dt)
    return pl.pallas_call(
        functools.partial(_attn_kernel, tq=tq, tk=tk, group=group, n_keys=n_keys, past=past, k_keep=k_keep),
        grid=(b, t // tq),
        in_specs=[qblk(D_ATT), qblk(D_ATT), qblk(LANES), pl.BlockSpec((1, 1, LANES), lambda bi, i: (bi, 0, 0)),
                  kvblk(LANES), kvblk(D_ATT), kvblk(D_ATT)],
        out_specs=qblk(D_ATT),
        out_shape=jax.ShapeDtypeStruct((b, t, D_ATT), BF16),
        scratch_shapes=[
            pltpu.VMEM((tq, l_pad), I32),
            pltpu.VMEM((tq, l_pad), BF16),
            per_pair(BF16),
            per_pair(BF16),
            per_head(LANES, F32),
            per_head(LANES, F32),
            pltpu.VMEM((tq, tk), F32),
            per_head(LANES, F32),
            per_pair(F32),
            per_head(1, F32),
            per_head(1, F32),
            pltpu.VMEM((N_HEADS // 2, 2 * tq, tk), F32),
            pltpu.VMEM((N_HEADS // 2, 2 * tq, tk), F32),
        ],
        compiler_params=_params("arbitrary", "arbitrary"),
    )(q, qi, kw, kmax, kiki, k_all, v_all)


def _cmul(ar, ai, br, bi):
    return ar * br - ai * bi, ar * bi + ai * br


def _s5_kernel(u_ref, h0re_ref, h0im_ref, are_ref, aim_ref, ldt_ref, bre_ref, bim_ref, cre_ref, cim_ref,
               dskip_ref, wglu_ref, bglu_ref, y_ref, hre_ref, him_ref,
               bbar_s, lvl_s, pw_s, h_s, *, ts, lane_chunk):
    n = N_STATE
    t_idx = pl.program_id(1)
    first = jnp.logical_and(pl.program_id(0) == 0, t_idx == 0)

    @pl.when(first)
    def _discretise():
        dt = jnp.exp(ldt_ref[...])
        ar, ai = are_ref[...], aim_ref[...]
        mag = jnp.exp(dt * ar)
        abr, abi = mag * jnp.cos(dt * ai), mag * jnp.sin(dt * ai)
        den = ar * ar + ai * ai
        nr, ni = abr - 1.0, abi
        f_re, f_im = (nr * ar + ni * ai) / den, (ni * ar - nr * ai) / den
        reps = D_SSM // SSM_GROUP
        bre = jnp.concatenate([bre_ref[...]] * reps, axis=0)
        bim = jnp.concatenate([bim_ref[...]] * reps, axis=0)
        r_grp = lax.broadcasted_iota(I32, (D_SSM, n), 0) // SSM_GROUP
        c_grp = lax.broadcasted_iota(I32, (D_SSM, n), 1) // SSM_STATE
        same = r_grp == c_grp
        bbar_s[:, :n] = jnp.where(same, f_re * bre - f_im * bim, 0.0).astype(BF16)
        bbar_s[:, n:] = jnp.where(same, f_re * bim + f_im * bre, 0.0).astype(BF16)
        a1 = (abr, abi)
        a2 = _cmul(*a1, *a1)
        a3 = _cmul(*a2, *a1)
        a4 = _cmul(*a2, *a2)
        a5 = _cmul(*a4, *a1)
        a6 = _cmul(*a4, *a2)
        a7 = _cmul(*a4, *a3)
        a8 = _cmul(*a4, *a4)
        rows = lax.broadcasted_iota(I32, (SUBLANES, n), 0)
        for part in range(2):
            pw = jnp.zeros((SUBLANES, n), F32)
            for r, a in enumerate((a1, a2, a3, a4, a5, a6, a7, a8)):
                pw = jnp.where(rows == r, a[part], pw)
            pw_s[part] = pw
            for lv, (a, dist) in enumerate(((a1, 1), (a2, 2), (a4, 4))):
                lvl_s[lv, part] = jnp.where(rows >= dist, a[part], 0.0)

    @pl.when(t_idx == 0)
    def _load_state():
        h_s[0:SUBLANES, :] = jnp.zeros((SUBLANES, 2 * n), F32)
        h_s[SUBLANES - 1:SUBLANES, :n] = h0re_ref[0]
        h_s[SUBLANES - 1:SUBLANES, n:] = h0im_ref[0]

    u = u_ref[0]
    u_bf = u.astype(BF16)
    n_slabs = D_SSM // LANES
    slab_w = n // n_slabs
    for part in range(2):
        for j in range(n_slabs):
            cols = slice(part * n + j * slab_w, part * n + (j + 1) * slab_w)
            h_s[SUBLANES:, cols] = jnp.dot(u_bf[:, j * LANES:(j + 1) * LANES], bbar_s[j * LANES:(j + 1) * LANES, cols],
                                           preferred_element_type=F32)

    def tile_step(j, carry):
        r0 = pl.multiple_of(j * SUBLANES, SUBLANES)
        for c in range(n // lane_chunk):
            re_sl = slice(c * lane_chunk, (c + 1) * lane_chunk)
            im_sl = slice(n + c * lane_chunk, n + (c + 1) * lane_chunk)
            xr = h_s[pl.ds(r0 + SUBLANES, SUBLANES), re_sl]
            xi = h_s[pl.ds(r0 + SUBLANES, SUBLANES), im_sl]
            for lv, dist in enumerate((1, 2, 4)):
                sr, si = pltpu.roll(xr, dist, 0), pltpu.roll(xi, dist, 0)
                dr, di = _cmul(lvl_s[lv, 0, :, re_sl], lvl_s[lv, 1, :, re_sl], sr, si)
                xr, xi = xr + dr, xi + di
            prev_r = h_s[pl.ds(r0, SUBLANES), re_sl][SUBLANES - 1:SUBLANES, :]
            prev_i = h_s[pl.ds(r0, SUBLANES), im_sl][SUBLANES - 1:SUBLANES, :]
            cr, ci = _cmul(pw_s[0, :, re_sl], pw_s[1, :, re_sl], prev_r, prev_i)
            h_s[pl.ds(r0 + SUBLANES, SUBLANES), re_sl] = xr + cr
            h_s[pl.ds(r0 + SUBLANES, SUBLANES), im_sl] = xi + ci
        return carry

    lax.fori_loop(0, ts // SUBLANES, tile_step, 0)

    nt = (((1,), (1,)), ((), ()))
    y_slabs = []
    for j in range(n_slabs):
        rows, cols = slice(j * LANES, (j + 1) * LANES), slice(j * slab_w, (j + 1) * slab_w)
        h_re = h_s[SUBLANES:, cols].astype(BF16)
        h_im = h_s[SUBLANES:, n + j * slab_w:n + (j + 1) * slab_w].astype(BF16)
        y_slabs.append(lax.dot_general(h_re, cre_ref[rows, cols], nt, preferred_element_type=F32)
                       - lax.dot_general(h_im, cim_ref[rows, cols], nt, preferred_element_type=F32))
    y = jnp.concatenate(y_slabs, axis=1)
    y = y + dskip_ref[...] * u
    y = jax.nn.gelu(y)
    gate = jax.nn.sigmoid(jnp.dot(y.astype(BF16), wglu_ref[...], preferred_element_type=F32) + bglu_ref[...])
    y_ref[0] = (y * gate).astype(y_ref.dtype)

    last = h_s[ts:ts + SUBLANES, :]
    h_s[0:SUBLANES, :] = last

    @pl.when(t_idx == pl.num_programs(1) - 1)
    def _emit_state():
        hre_ref[0] = last[SUBLANES - 1:SUBLANES, :n]
        him_ref[0] = last[SUBLANES - 1:SUBLANES, n:]


def _s5(u, h0_re, h0_im, prm, ts):
    b, t, _ = u.shape
    n = N_STATE
    a_re, a_im, ldt, bre_d, bim_d, cre_t, cim_t, d_skip, w_glu, b_glu = prm
    const = lambda a: _resident(a.shape, lambda bi, i: (0,) * a.ndim)
    st_blk = pl.BlockSpec((1, 1, n), lambda bi, i: (bi, 0, 0))
    y, hre, him = pl.pallas_call(
        functools.partial(_s5_kernel, ts=ts, lane_chunk=512),
        grid=(b, t // ts),
        in_specs=[pl.BlockSpec((1, ts, D_SSM), lambda bi, i: (bi, i, 0)), st_blk, st_blk,
                  const(a_re), const(a_im), const(ldt), const(bre_d), const(bim_d), const(cre_t), const(cim_t),
                  const(d_skip), const(w_glu), const(b_glu)],
        out_specs=[pl.BlockSpec((1, ts, D_SSM), lambda bi, i: (bi, i, 0)), st_blk, st_blk],
        out_shape=[jax.ShapeDtypeStruct((b, t, D_SSM), BF16),
                   jax.ShapeDtypeStruct((b, 1, n), F32), jax.ShapeDtypeStruct((b, 1, n), F32)],
        scratch_shapes=[
            pltpu.VMEM((D_SSM, 2 * n), BF16),
            pltpu.VMEM((3, 2, SUBLANES, n), F32),
            pltpu.VMEM((2, SUBLANES, n), F32),
            pltpu.VMEM((SUBLANES + ts, 2 * n), F32),
        ],
        compiler_params=_params("arbitrary", "arbitrary"),
    )(u, h0_re, h0_im, a_re, a_im, ldt, bre_d, bim_d, cre_t, cim_t, d_skip, w_glu, b_glu)
    return y, hre, him


def _layer_norm(z, g, b):
    mu = jnp.mean(z, axis=-1, keepdims=True)
    zc = z - mu
    var = jnp.mean(zc * zc, axis=-1, keepdims=True)
    return zc * lax.rsqrt(var + LN_EPS) * g + b


def _merge_kernel(x_ref, att_ref, ssm_ref, wg_ref, bg_ref, wpa_ref, wpb_ref, wo_ref, g_ref, b_ref, o_ref, *, alpha):
    x = x_ref[...]
    d = x.shape[1]
    gates = jax.nn.sigmoid(jnp.dot(x.astype(BF16), wg_ref[...], preferred_element_type=F32) + bg_ref[...])
    mix = (gates[:, :d] * jnp.dot(att_ref[...], wpa_ref[...], preferred_element_type=F32)
           + gates[:, d:] * jnp.dot(ssm_ref[...], wpb_ref[...], preferred_element_type=F32))
    z = alpha * x + jnp.dot(mix.astype(BF16), wo_ref[...], preferred_element_type=F32)
    o_ref[...] = _layer_norm(z, g_ref[...], b_ref[...])


def _merge(x2d, att2d, ssm2d, prm, alpha, tm):
    rows, d = x2d.shape
    w_g, b_g, w_pa, w_pb, w_o, ln_g, ln_b = prm
    row_blk = lambda w: pl.BlockSpec((tm, w), lambda i: (i, 0))
    const = lambda a: _resident(a.shape, lambda i: (0, 0))
    return pl.pallas_call(
        functools.partial(_merge_kernel, alpha=alpha),
        grid=(rows // tm,),
        in_specs=[row_blk(d), row_blk(D_ATT), row_blk(D_SSM)] + [const(a) for a in prm],
        out_specs=row_blk(d),
        out_shape=jax.ShapeDtypeStruct((rows, d), F32),
        compiler_params=_params("parallel"),
    )(x2d, att2d, ssm2d, *prm)


def _ffn_kernel(x_ref, cbuf_ref, wup_ref, cw_ref, cb_ref, wdn_ref, g_ref, b_ref, o_ref, nconv_ref,
                h_s, acc_s, *, tm, d_ff, fc, alpha):
    t_idx = pl.program_id(1)
    hist = CONV_W - 1

    @pl.when(t_idx == 0)
    def _load_history():
        h_s[0:SUBLANES, :] = jnp.zeros((SUBLANES, h_s.shape[1]), F32)
        h_s[SUBLANES - hist:SUBLANES, :] = cbuf_ref[0]

    x = x_ref[0]
    x_bf = x.astype(BF16)
    n_chunks = d_ff // fc
    column_pair = lambda c: (slice(c * fc, (c + 1) * fc), slice(d_ff + c * fc, d_ff + (c + 1) * fc))

    def up_project(c):
        for cols in column_pair(c):
            h_s[SUBLANES:, cols] = jnp.dot(x_bf, wup_ref[:, cols], preferred_element_type=F32)

    def conv_act_down(c):
        conv = [cb_ref[:, cols] + sum(cw_ref[j:j + 1, cols] * h_s[SUBLANES - hist + j:SUBLANES - hist + j + tm, cols]
                                      for j in range(CONV_W)) for cols in column_pair(c)]
        act = (jax.nn.gelu(conv[0]) * conv[1]).astype(BF16)
        part = jnp.dot(act, wdn_ref[c * fc:(c + 1) * fc, :], preferred_element_type=F32)
        if c == 0:
            acc_s[...] = part
        else:
            acc_s[...] += part

    up_project(0)
    for c in range(n_chunks):
        if c + 1 < n_chunks:
            up_project(c + 1)
        conv_act_down(c)
    o_ref[0] = _layer_norm(alpha * x + acc_s[...], g_ref[...], b_ref[...])

    tail = h_s[tm:tm + SUBLANES, :]
    h_s[0:SUBLANES, :] = tail

    @pl.when(t_idx == pl.num_programs(1) - 1)
    def _emit_history():
        nconv_ref[0] = tail[SUBLANES - hist:, :]


def _ffn(x, conv_buf, prm, alpha, tm):
    b, t, d = x.shape
    w_up, conv_w, conv_b, w_dn, ln_g, ln_b = prm
    d_ff = w_dn.shape[0]
    fc = 256
    assert d_ff % fc == 0 and t % tm == 0 and tm % SUBLANES == 0
    const = lambda a: _resident(a.shape, lambda bi, i: (0, 0))
    hist_blk = pl.BlockSpec((1, CONV_W - 1, 2 * d_ff), lambda bi, i: (bi, 0, 0))
    return pl.pallas_call(
        functools.partial(_ffn_kernel, tm=tm, d_ff=d_ff, fc=fc, alpha=alpha),
        grid=(b, t // tm),
        in_specs=[pl.BlockSpec((1, tm, d), lambda bi, i: (bi, i, 0)), hist_blk] + [const(a) for a in prm],
        out_specs=[pl.BlockSpec((1, tm, d), lambda bi, i: (bi, i, 0)), hist_blk],
        out_shape=[jax.ShapeDtypeStruct((b, t, d), F32), jax.ShapeDtypeStruct((b, CONV_W - 1, 2 * d_ff), F32)],
        scratch_shapes=[pltpu.VMEM((SUBLANES + tm, 2 * d_ff), F32),
                        pltpu.VMEM((tm, d), F32)],
        compiler_params=_params("arbitrary", "arbitrary"),
    )(x, conv_buf, *prm)


def _rope_tables(pos):
    half = HEAD_DIM // 2
    inv = ROPE_THETA ** (-jnp.arange(half, dtype=F32) / half)
    ang = pos.astype(F32)[:, None] * inv[None, :]
    cos, sin = jnp.cos(ang), jnp.sin(ang)
    cos_tab = jnp.tile(cos, (1, LANES // half))
    sin_tab = jnp.tile(jnp.concatenate([-sin, sin], axis=1), (1, LANES // HEAD_DIM))
    return cos_tab, sin_tab


def _layer_weights(w_in, b_gate, w_pa, w_pb, w_o, a_re, a_im, log_dt, b_re, b_im, c_re, c_im, d_skip, w_glu,
                   b_glu, ln1_g, ln1_b, w_up, conv_w, conv_b, w_down, ln2_g, ln2_b):
    d = w_in.shape[0]
    o = 0
    pieces = []
    for w in (D_ATT, D_ATT, D_ATT, N_IDX_HEADS * IDX_DIM, IDX_DIM + N_IDX_HEADS, D_SSM, 2 * d):
        pieces.append(w_in[:, o:o + w])
        o += w
    wq, wk, wv, wqi, wkw, wu, wg = pieces
    wkw = jnp.pad(wkw, ((0, 0), (0, LANES - wkw.shape[1])))
    inproj = tuple(w.astype(BF16) for w in (wq, wk, wv, wqi, wkw, wu))
    n = N_STATE
    row = lambda a: a.reshape(1, -1).astype(F32)
    same_group = jnp.eye(N_GROUPS, dtype=bool)[:, None, :, None]
    blockdiag_t = lambda c: jnp.where(same_group, c[:, :, None, :], 0.0).reshape(D_SSM, n).astype(BF16)
    dense = lambda bm: jnp.transpose(bm, (2, 0, 1)).reshape(SSM_GROUP, n).astype(F32)
    s5 = (row(a_re), row(a_im), row(jnp.repeat(log_dt, SSM_STATE)), dense(b_re), dense(b_im),
          blockdiag_t(c_re), blockdiag_t(c_im), row(d_skip), w_glu.astype(BF16), row(b_glu))
    merge = (wg.astype(BF16), row(b_gate), w_pa.astype(BF16), w_pb.astype(BF16), w_o.astype(BF16),
             row(ln1_g), row(ln1_b))
    ffn = (w_up.astype(BF16), conv_w.astype(F32), row(conv_b), w_down.astype(BF16), row(ln2_g), row(ln2_b))
    return inproj, s5, merge, ffn


def _largest_tile(n, cap):
    t = min(n, cap)
    while n % t:
        t -= SUBLANES
    return t


def _trunk_layer(x, past_k, past_v, past_ki, h0_re, h0_im, conv_buf, weights, alpha):
    bn, t, d = x.shape
    p = 0 if past_k is None else past_k.shape[1]
    n_keys = p + t
    k_keep = min(TOPK_MAX, n_keys // 4)
    inproj_w, s5_w, merge_w, ffn_w = weights
    rows = bn * t
    tm = _largest_tile(rows, 512)

    cos_tab, sin_tab = _rope_tables(p + jnp.arange(t))
    if t % tm == 0:
        n_tab_blocks = t // tm
    else:
        cos_tab, sin_tab = jnp.tile(cos_tab, (bn, 1)), jnp.tile(sin_tab, (bn, 1))
        n_tab_blocks = rows // tm
    q, k_f, k_b, v_f, v_b, qi, kw, u = _inproj(x.reshape(rows, d), cos_tab, sin_tab, inproj_w, tm, n_tab_blocks)

    r3 = lambda a: a.reshape(bn, t, a.shape[-1])
    ki_b = r3(kw)[:, :, :IDX_DIM].astype(BF16)
    k_all, v_all = r3(k_b), r3(v_b)
    if p:
        k_all = jnp.concatenate([past_k.reshape(bn, p, D_ATT).astype(BF16), k_all], axis=1)
        v_all = jnp.concatenate([past_v.reshape(bn, p, D_ATT).astype(BF16), v_all], axis=1)
        ki_b = jnp.concatenate([past_ki.astype(BF16), ki_b], axis=1)
    l_pad = -(-n_keys // LANES) * LANES
    l_pad = max(l_pad, 2 * LANES)
    pad = lambda a: jnp.pad(a, ((0, 0), (0, l_pad - n_keys), (0, 0)))
    kiki = jnp.concatenate([ki_b, ki_b], axis=-1)
    k_heads = k_all.astype(F32).reshape(bn, n_keys, N_HEADS, HEAD_DIM)
    kmax = jnp.sqrt(jnp.max(jnp.sum(k_heads * k_heads, axis=-1), axis=1))
    kmax = jnp.pad(kmax, ((0, 0), (0, LANES - N_HEADS))).reshape(bn, 1, LANES)
    tq = _largest_tile(t, 128)
    att = _attention(r3(q), r3(qi), r3(kw), kmax, pad(kiki), pad(k_all), pad(v_all),
                     n_keys=n_keys, past=p, k_keep=k_keep, tq=tq)

    ts = _largest_tile(t, 256)
    ssm, h_re, h_im = _s5(r3(u), h0_re.reshape(bn, 1, N_STATE), h0_im.reshape(bn, 1, N_STATE), s5_w, ts)

    x1 = _merge(x.reshape(rows, d), att.reshape(rows, D_ATT), ssm.reshape(rows, D_SSM), merge_w, alpha, tm)
    x2, new_conv = _ffn(x1.reshape(bn, t, d), conv_buf, ffn_w, alpha, _largest_tile(t, 256))

    k_out = r3(k_f).reshape(bn, t, N_HEADS, HEAD_DIM)
    v_out = r3(v_f).reshape(bn, t, N_HEADS, HEAD_DIM)
    ki_out = r3(kw)[:, :, :IDX_DIM]
    return (x2, k_out, v_out, ki_out, h_re.reshape(bn, N_GROUPS, SSM_STATE), h_im.reshape(bn, N_GROUPS, SSM_STATE),
            new_conv)


def kernel(x_prompt, x_sample, cache_k, cache_v, cache_idx_k, state_ssm_re, state_ssm_im, state_conv, w_in, b_gate, w_pa, w_pb, w_o, a_re, a_im, log_dt, b_re, b_im, c_re, c_im, d_skip, w_glu, b_glu, ln1_g, ln1_b, w_up, conv_w, conv_b, w_down, ln2_g, ln2_b):
    depth = w_in.shape[0]
    alpha = (2 * depth) ** 0.25
    bp = x_prompt.shape[0]
    d_ff2 = w_up.shape[2]
    zero_h = jnp.zeros((bp, N_GROUPS, SSM_STATE), F32)
    zero_conv = jnp.zeros((bp, CONV_W - 1, d_ff2), F32)
    layer_params = (w_in, b_gate, w_pa, w_pb, w_o, a_re, a_im, log_dt, b_re, b_im, c_re, c_im, d_skip, w_glu, b_glu,
                    ln1_g, ln1_b, w_up, conv_w, conv_b, w_down, ln2_g, ln2_b)
    xp, xs = x_prompt, x_sample
    outs_p, outs_s = [], []
    for l in range(depth):
        weights = _layer_weights(*(a[l] for a in layer_params))
        rp = _trunk_layer(xp, None, None, None, zero_h, zero_h, zero_conv, weights, alpha)
        rs = _trunk_layer(xs, cache_k[l], cache_v[l], cache_idx_k[l], state_ssm_re[l].astype(F32),
                          state_ssm_im[l].astype(F32), state_conv[l], weights, alpha)
        xp, xs = rp[0], rs[0]
        outs_p.append(rp[1:])
        outs_s.append(rs[1:])
    stack = lambda outs: tuple(jnp.stack([o[j] for o in outs], axis=0) for j in range(6))
    return (xp, xs) + stack(outs_p) + stack(outs_s)
```

```python
import functools

import jax
import jax.numpy as jnp
from jax import lax
from jax.experimental import pallas as pl
from jax.experimental.pallas import tpu as pltpu

CHUNK = 64
N_HEADS = 8
HEAD_DIM = 64
D_ATT = N_HEADS * HEAD_DIM
N_IDX_HEADS = 8
IDX_DIM = 64
TOPK_MAX = 256
D_SSM = 512
SSM_GROUP = 16
N_GROUPS = D_SSM // SSM_GROUP
SSM_STATE = 64
N_STATE = N_GROUPS * SSM_STATE
CONV_W = 3
ROPE_THETA = 10000.0
LN_EPS = 1e-5
LOG2_E = 1.4426950408889634

LANES = 128
SUBLANES = 8
VMEM_LIMIT = 60 * 1024 * 1024

F32 = jnp.float32
BF16 = jnp.bfloat16
I32 = jnp.int32
NEG_INF = float("-inf")
INT_MIN = -(2 ** 31)
HIGH16 = -(2 ** 16)
MIN_NORMAL_BITS = 0x00800000


def _f32_key_of_neg_inf():
    bits = 0xFF800000
    signed = bits - (1 << 32)
    return signed ^ ((signed >> 31) & 0x7FFFFFFF)


NEG_INF_KEY = _f32_key_of_neg_inf()
UP_AHEAD = 5
TILES_PER_TRIP = 4
SEARCH_UNROLL = 4
LOW_VALUE_BITS = 18
LOW_STEP_BITS = 2
LANE_KEEP = 3
SOFTMAX_SUM_FLOOR = 1e-30


def _resident(block_shape, index_map):
    return pl.BlockSpec(block_shape, index_map, pipeline_mode=pl.Buffered(1))


def _params(*sem):
    return pltpu.CompilerParams(dimension_semantics=sem, vmem_limit_bytes=VMEM_LIMIT)


def _inproj_kernel(x_ref, cos_ref, sin_ref, wq_ref, wk_ref, wv_ref, wqi_ref, wkw_ref, wu_ref,
                   q_ref, kf_ref, kb_ref, vf_ref, vb_ref, qi_ref, kw_ref, u_ref, *, wi_scale):
    x = x_ref[...].astype(BF16)
    cos = cos_ref[...]
    sin = sin_ref[...]
    tm = x.shape[0]
    lane = lax.broadcasted_iota(I32, (tm, LANES), 1)
    first_half = (lane % HEAD_DIM) < (HEAD_DIM // 2)

    def proj(w_ref):
        return jnp.dot(x, w_ref[...], preferred_element_type=F32)

    def rope_group(y):
        rot = jnp.where(first_half, pltpu.roll(y, LANES - HEAD_DIM // 2, 1), pltpu.roll(y, HEAD_DIM // 2, 1))
        return y * cos + rot * sin

    def rope(y):
        return jnp.concatenate([rope_group(y[:, c * LANES:(c + 1) * LANES]) for c in range(y.shape[1] // LANES)],
                               axis=1)

    q = rope(proj(wq_ref))
    q_ref[...] = (q * (HEAD_DIM ** -0.5 * LOG2_E)).astype(BF16)
    k = rope(proj(wk_ref))
    kf_ref[...] = k
    kb_ref[...] = k.astype(BF16)
    v = proj(wv_ref)
    vf_ref[...] = v
    vb_ref[...] = v.astype(BF16)
    qi_ref[...] = rope(proj(wqi_ref)).astype(BF16)
    kw = proj(wkw_ref)
    kw_ref[...] = jnp.where(lane < IDX_DIM, rope_group(kw), kw * wi_scale)
    u_ref[...] = proj(wu_ref)


def _inproj(x2d, cos_tab, sin_tab, ws, tm, n_tab_blocks):
    rows, d = x2d.shape
    wq, wk, wv, wqi, wkw, wu = ws
    row_blk = lambda w: pl.BlockSpec((tm, w), lambda i: (i, 0))
    tab_blk = pl.BlockSpec((tm, LANES), lambda i: (i % n_tab_blocks, 0))
    w_blk = lambda w: _resident(w.shape, lambda i: (0, 0))
    out_shapes = [
        jax.ShapeDtypeStruct((rows, D_ATT), BF16),
        jax.ShapeDtypeStruct((rows, D_ATT), F32),
        jax.ShapeDtypeStruct((rows, D_ATT), BF16),
        jax.ShapeDtypeStruct((rows, D_ATT), F32),
        jax.ShapeDtypeStruct((rows, D_ATT), BF16),
        jax.ShapeDtypeStruct((rows, D_ATT), BF16),
        jax.ShapeDtypeStruct((rows, LANES), F32),
        jax.ShapeDtypeStruct((rows, D_SSM), F32),
    ]
    wi_scale = (N_IDX_HEADS ** -0.5) * (IDX_DIM ** -0.5)
    return pl.pallas_call(
        functools.partial(_inproj_kernel, wi_scale=wi_scale),
        grid=(rows // tm,),
        in_specs=[row_blk(d), tab_blk, tab_blk, w_blk(wq), w_blk(wk), w_blk(wv), w_blk(wqi), w_blk(wkw), w_blk(wu)],
        out_specs=[row_blk(s.shape[1]) for s in out_shapes],
        out_shape=out_shapes,
        compiler_params=_params("parallel"),
    )(x2d, cos_tab, sin_tab, wq, wk, wv, wqi, wkw, wu)


def _attn_kernel(q_ref, qi_ref, kw_ref, kmax_ref, kiki_ref, k_ref, v_ref, o_ref,
                 key_s, keyhi_s, qm_s, qim_s, wi_s, mb_s, bias_s, l_s, acc_s, m2_s, l2_s, lga_s, lgb_s,
                 *, tq, tk, group, n_keys, past, k_keep):
    q0 = pl.program_id(1) * tq
    n_pairs = N_HEADS // 2
    n_chunks = tk // LANES
    row = lax.broadcasted_iota(I32, (tq, 1), 0)
    q_lim = jnp.minimum(((past + q0 + row) // CHUNK + 1) * CHUNK, n_keys)
    q_lim_b = jnp.broadcast_to(q_lim, (tq, LANES))
    blk_lim = jnp.minimum(((past + q0 + tq - 1) // CHUNK + 1) * CHUNK, n_keys)
    nk = (blk_lim + tk - 1) // tk
    lane_q = lax.broadcasted_iota(I32, (tq, LANES), 1)
    lo_half = lane_q < HEAD_DIM
    nt = (((1,), (1,)), ((), ()))
    pair_of = lambda h: slice((h // 2) * LANES, (h // 2 + 1) * LANES)
    chunk_of = lambda c: slice(c * LANES, (c + 1) * LANES)

    kw = kw_ref[0]
    kmax = kmax_ref[0]
    halves = (slice(0, tq), slice(tq, 2 * tq))
    for pr in range(n_pairs):
        qi_pair = qi_ref[0, :, pair_of(2 * pr)]
        q_pair = q_ref[0, :, pair_of(2 * pr)]
        for half, keep in enumerate((lo_half, jnp.logical_not(lo_half))):
            h = 2 * pr + half
            qim_s[pr, halves[half]] = jnp.where(keep, qi_pair, jnp.zeros_like(qi_pair))
            qm = jnp.where(keep, q_pair, jnp.zeros_like(q_pair))
            qm_s[pr, halves[half]] = qm
            wi_s[h] = jnp.broadcast_to(kw[:, IDX_DIM + h:IDX_DIM + h + 1], (tq, LANES))
            qf = qm.astype(F32)
            q_norm = jnp.sqrt(jnp.sum(qf * qf, axis=1, keepdims=True))
            mb_s[h] = jnp.broadcast_to(q_norm * kmax[:, h:h + 1], (tq, LANES))

    def key_base(kt, c):
        return pl.multiple_of(kt * tk + c * LANES, LANES)

    def tile_start(kt):
        return pl.multiple_of(jnp.minimum(kt, nk - 1) * tk, tk)

    def two_stage_loop(produce, consume):
        produce(0, lga_s)

        def body(j, carry):
            for i in range(TILES_PER_TRIP):
                now, nxt = (lga_s, lgb_s) if i % 2 == 0 else (lgb_s, lga_s)
                produce(TILES_PER_TRIP * j + i + 1, nxt)
                consume(TILES_PER_TRIP * j + i, now)
            return carry

        lax.fori_loop(0, (nk + TILES_PER_TRIP - 1) // TILES_PER_TRIP, body, 0)

    def produce_scores(kt, buf):
        kiki = kiki_ref[0, pl.ds(tile_start(kt), tk), :]
        for pr in range(n_pairs):
            buf[pr] = lax.dot_general(qim_s[pr], kiki, nt, preferred_element_type=F32)

    def consume_scores(kt, buf):
        ktc = jnp.minimum(kt, nk - 1)
        score = [jnp.zeros((tq, LANES), F32) for _ in range(n_chunks)]
        for pr in range(n_pairs):
            for half in range(2):
                w = wi_s[2 * pr + half]
                for c in range(n_chunks):
                    score[c] = score[c] + w * jnp.maximum(buf[pr, halves[half], chunk_of(c)], 0.0)
        for c in range(n_chunks):
            kb = key_base(ktc, c)
            bits = pltpu.bitcast(jnp.where(kb + lane_q < q_lim_b, score[c], NEG_INF), I32)
            key_s[:, pl.ds(kb, LANES)] = bits ^ ((bits >> 31) & 0x7FFFFFFF)
            keyhi_s[:, pl.ds(kb, LANES)] = pltpu.bitcast(bits & HIGH16, F32).astype(BF16)

    two_stage_loop(produce_scores, consume_scores)

    n_groups = (nk + group - 1) // group

    def fill_tile(kt, carry):
        for c in range(n_chunks):
            kb = key_base(kt, c)
            key_s[:, pl.ds(kb, LANES)] = jnp.full((tq, LANES), NEG_INF_KEY, I32)
            keyhi_s[:, pl.ds(kb, LANES)] = jnp.full((tq, LANES), NEG_INF, BF16)
        return carry

    lax.fori_loop(nk, n_groups * group, fill_tile, 0)

    def group_chunks(g_idx):
        return [pl.multiple_of(g_idx * (group * tk) + c * LANES, LANES) for c in range(group * n_chunks)]

    def count(pred):
        def body(g_idx, cnt):
            for kb in group_chunks(g_idx):
                cnt = cnt + pred(key_s[:, pl.ds(kb, LANES)], kb).astype(I32)
            return cnt
        cnt = lax.fori_loop(0, n_groups, body, jnp.zeros((tq, LANES), I32))
        return jnp.sum(cnt, axis=1, keepdims=True)

    def count_ge(c):
        cb = jnp.broadcast_to(c, (tq, LANES))
        return count(lambda key, kb: key >= cb)

    def count_ge_high(cands):
        def lanes_of(c):
            c_bits = (c ^ ((c >> 31) & 0x7FFFFFFF)) & HIGH16
            c_bits = jnp.where(c_bits > 0, jnp.maximum(c_bits, MIN_NORMAL_BITS), c_bits)
            return jnp.broadcast_to(pltpu.bitcast(c_bits, F32).astype(BF16), (tq, LANES))

        cbs = [lanes_of(c) for c in cands]
        one, nil = jnp.ones((tq, LANES), BF16), jnp.zeros((tq, LANES), BF16)

        def body(g_idx, cnts):
            cnts = list(cnts)
            for kb in group_chunks(g_idx):
                x = keyhi_s[:, pl.ds(kb, LANES)]
                for i, cb in enumerate(cbs):
                    cnts[i] = cnts[i] + jnp.where(x >= cb, one, nil)
            return tuple(cnts)
        cnts = lax.fori_loop(0, n_groups, body, (nil,) * len(cbs))
        return [jnp.sum(c.astype(F32), axis=1, keepdims=True).astype(I32) for c in cnts]

    def visit(cand, cnt, st):
        t, thr, done, at_t = st
        t = jnp.where(cnt >= k_keep, cand, t)
        at_t = jnp.where(cnt >= k_keep, cnt, at_t)
        exact = cnt == k_keep
        thr = jnp.where(done == 0, jnp.where(exact, jnp.maximum(cand - 1, NEG_INF_KEY), thr), thr)
        done = jnp.where(exact, 1, done)
        return t, thr, done, at_t

    def search_loop(n_trips, trip, state):
        def cond(st):
            return jnp.logical_and(st[0] < n_trips, jnp.min(st[3]) == 0)
        return lax.while_loop(cond, lambda st: (st[0] + 1,) + trip(st[0], st[1:]), (jnp.int32(0),) + state)[1:]

    def search(counter, first_bit, n_bits, per_trip, state):
        def trip(step, st):
            for i in range(per_trip):
                cand = st[0] + jnp.left_shift(jnp.int32(1), first_bit - i - step * per_trip)
                st = visit(cand, counter(cand), st)
            return st
        return search_loop(n_bits // per_trip, trip, state)

    zero = jnp.zeros((tq, 1), I32)
    high_state = search(lambda c: count_ge_high([c])[0], 31, 16, 16,
                        (jnp.full((tq, 1), INT_MIN, I32), zero, zero, zero))
    t_high = high_state[0]
    lane_sum = lambda a: jnp.sum(a.astype(F32), axis=1, keepdims=True).astype(I32)

    def low_half(_):
        above, from_bucket = count_ge_high([t_high + 2 ** 16, t_high])
        members = from_bucket - above
        bucket = jnp.broadcast_to(t_high, (tq, LANES))

        def keep_largest(g_idx, kept):
            kept = list(kept)
            for kb in group_chunks(g_idx):
                key = key_s[:, pl.ds(kb, LANES)]
                v = jnp.where((key & HIGH16) == bucket, (key & 0xFFFF) + 1, 0)
                for i in range(LANE_KEEP):
                    kept[i], v = jnp.maximum(kept[i], v), jnp.minimum(kept[i], v)
            return tuple(kept)

        nil = jnp.zeros((tq, LANES), I32)
        kept = lax.fori_loop(0, n_groups, keep_largest, (nil,) * LANE_KEEP)
        count_low = lambda pred: lane_sum(functools.reduce(jnp.add, [pred(m).astype(I32) for m in kept]))
        caught_all = jnp.min((count_low(lambda m: m > 0) == members).astype(I32)) == 1

        def from_lanes(_):
            want_low = k_keep - above
            low = zero
            for step in range(LOW_VALUE_BITS // LOW_STEP_BITS):
                unit = 1 << (LOW_VALUE_BITS - LOW_STEP_BITS * (step + 1))
                taken = zero
                low_lanes = jnp.broadcast_to(low, (tq, LANES))
                for mult in range(1, 2 ** LOW_STEP_BITS):
                    cand = low_lanes + mult * unit
                    taken = taken + (count_low(lambda m: m >= cand) >= want_low).astype(I32)
                low = low + taken * unit
            low = jnp.maximum(low, 1)
            low_b = jnp.broadcast_to(low, (tq, LANES))
            return t_high + low - 1, high_state[1], high_state[2], above + count_low(lambda m: m >= low_b)

        return lax.cond(caught_all, from_lanes, lambda _: search(count_ge, 15, 16, SEARCH_UNROLL, high_state), 0)

    t_fin, thr_early, done, at_t = lax.cond(jnp.min(high_state[2]) == 0, low_half, lambda _: high_state, 0)
    t_fin = jnp.maximum(t_fin, NEG_INF_KEY)

    def resolve_ties(_):
        tb = jnp.broadcast_to(t_fin, (tq, LANES))
        need = jnp.logical_and(jnp.logical_and(done == 0, at_t > k_keep), t_fin > NEG_INF_KEY)

        def cut_search(_):
            want = (k_keep - count_ge(t_fin + 1)).astype(F32)
            upper = jnp.where(lax.broadcasted_iota(I32, (tk, tk), 0) <= lax.broadcasted_iota(I32, (tk, tk), 1),
                              1.0, 0.0).astype(BF16)

            def body(g_idx, st):
                before, last = st
                bases = group_chunks(g_idx)
                for i in range(group):
                    tile_bases = bases[i * n_chunks:(i + 1) * n_chunks]
                    hits = [key_s[:, pl.ds(kb, LANES)] == tb for kb in tile_bases]
                    ones = jnp.concatenate([jnp.where(h, 1.0, 0.0) for h in hits], axis=1)
                    rank = before + jnp.dot(ones.astype(BF16), upper, preferred_element_type=F32)
                    for c, kb in enumerate(tile_bases):
                        take = jnp.logical_and(hits[c], rank[:, chunk_of(c)] <= want)
                        last = jnp.maximum(last, jnp.where(take, kb + lane_q, -1))
                    before = before + jnp.sum(ones, axis=1, keepdims=True)
                return before, last

            _, last = lax.fori_loop(0, n_groups, body, (jnp.zeros((tq, 1), F32), jnp.full((tq, LANES), -1, I32)))
            return jnp.max(last, axis=1, keepdims=True)

        cut = lax.cond(jnp.max(need.astype(I32)) > 0, cut_search, lambda _: zero, 0)
        all_ties = jnp.where(t_fin > NEG_INF_KEY, jnp.int32(2 ** 30), jnp.int32(-1))
        return jnp.where(need, cut, all_ties)

    cut_full = lax.cond(jnp.min(done) == 0, resolve_ties, lambda _: jnp.full((tq, 1), -1, I32), 0)
    thr = jnp.where(done == 1, thr_early, t_fin)
    cut = jnp.where(done == 1, -1, cut_full)
    thr_b = jnp.broadcast_to(thr, (tq, LANES))
    cut_b = jnp.broadcast_to(cut, (tq, LANES))

    def tile_bias(kt, valid):
        thr_t = jnp.where(valid, thr_b, jnp.int32(2 ** 31 - 1))
        cut_t = jnp.where(valid, cut_b, jnp.int32(-1))
        for c in range(n_chunks):
            kb = key_base(kt, c)
            key = key_s[:, pl.ds(kb, LANES)]
            bound = jnp.where(kb + lane_q <= cut_t, thr_t - 1, thr_t)
            bias_s[:, chunk_of(c)] = jnp.where(key > bound, 0.0, NEG_INF)

    def emit(l_of_head):
        for pr in range(n_pairs):
            even = acc_s[pr, halves[0]] / l_of_head(2 * pr)
            odd = acc_s[pr, halves[1]] / l_of_head(2 * pr + 1)
            o_ref[0, :, pair_of(2 * pr)] = jnp.where(lo_half, even, odd).astype(o_ref.dtype)

    l_s[...] = jnp.zeros(l_s.shape, F32)
    acc_s[...] = jnp.zeros(acc_s.shape, F32)

    def produce_logits(kt, buf):
        k0 = tile_start(kt)
        for pr in range(n_pairs):
            buf[pr] = lax.dot_general(qm_s[pr], k_ref[0, pl.ds(k0, tk), pair_of(2 * pr)], nt,
                                      preferred_element_type=F32)

    def consume_logits(kt, buf):
        k0 = tile_start(kt)
        tile_bias(jnp.minimum(kt, nk - 1), kt < nk)
        for pr in range(n_pairs):
            p = []
            for half in range(2):
                h = 2 * pr + half
                shift = mb_s[h]
                l_part = l_s[h]
                p_row = []
                for c in range(n_chunks):
                    e = jnp.exp2(buf[pr, halves[half], chunk_of(c)] + bias_s[:, chunk_of(c)] - shift)
                    l_part = l_part + e
                    p_row.append(e.astype(BF16))
                l_s[h] = l_part
                p.append(jnp.concatenate(p_row, axis=1))
            acc_s[pr] += jnp.dot(jnp.concatenate(p, axis=0), v_ref[0, pl.ds(k0, tk), pair_of(2 * pr)],
                                 preferred_element_type=F32)

    two_stage_loop(produce_logits, consume_logits)
    l_rows = [jnp.sum(l_s[h], axis=1, keepdims=True) for h in range(N_HEADS)]
    l_min = functools.reduce(jnp.minimum, [jnp.min(l) for l in l_rows])
    well_scaled = l_min >= SOFTMAX_SUM_FLOOR

    @pl.when(well_scaled)
    def _emit_bound():
        emit(lambda h: l_rows[h])

    @pl.when(jnp.logical_not(well_scaled))
    def _running_max():
        m2_s[...] = jnp.full(m2_s.shape, NEG_INF, F32)
        l2_s[...] = jnp.zeros(l2_s.shape, F32)
        acc_s[...] = jnp.zeros(acc_s.shape, F32)

        def online_tile(kt, carry):
            k0 = pl.multiple_of(kt * tk, tk)
            tile_bias(kt, True)
            for h in range(N_HEADS):
                rows = halves[h % 2]
                logits = lax.dot_general(qm_s[h // 2, rows], k_ref[0, pl.ds(k0, tk), pair_of(h)], nt,
                                         preferred_element_type=F32) + bias_s[...]
                m_old = m2_s[h]
                m_new = jnp.maximum(m_old, jnp.max(logits, axis=1, keepdims=True))
                m_safe = jnp.where(m_new == NEG_INF, 0.0, m_new)
                alpha = jnp.exp2(m_old - m_safe)
                p = jnp.exp2(logits - m_safe)
                l2_s[h] = alpha * l2_s[h] + jnp.sum(p, axis=1, keepdims=True)
                acc_s[h // 2, rows] = alpha * acc_s[h // 2, rows] + jnp.dot(
                    p.astype(BF16), v_ref[0, pl.ds(k0, tk), pair_of(h)], preferred_element_type=F32)
                m2_s[h] = m_new
            return carry

        lax.fori_loop(0, nk, online_tile, 0)
        emit(lambda h: l2_s[h])


def _pick_key_tile(l_pad, k_keep):
    for tk in (256, 384, 512, 640, 768, 896, 1024):
        if l_pad % tk == 0 and tk >= k_keep:
            return tk
    return l_pad


def _attention(q, qi, kw, kmax, kiki, k_all, v_all, *, n_keys, past, k_keep, tq):
    b, t, _ = q.shape
    l_pad = k_all.shape[1]
    tk = _pick_key_tile(l_pad, k_keep)
    assert l_pad % tk == 0 and tk >= k_keep and t % tq == 0
    assert l_pad // LANES <= 256
    group = max(g for g in (4, 2, 1) if l_pad % (g * tk) == 0)
    qblk = lambda w: pl.BlockSpec((1, tq, w), lambda bi, i: (bi, i, 0))
    kvblk = lambda w: _resident((1, l_pad, w), lambda bi, i: (bi, 0, 0))
    per_head = lambda w, dt: pltpu.VMEM((N_HEADS, tq, w), dt)
    per_pair = lambda dt: pltpu.VMEM((N_HEADS // 2, 2 * tq, LANES), dt)
    return pl.pallas_call(
        functools.partial(_attn_kernel, tq=tq, tk=tk, group=group, n_keys=n_keys, past=past, k_keep=k_keep),
        grid=(b, t // tq),
        in_specs=[qblk(D_ATT), qblk(D_ATT), qblk(LANES), pl.BlockSpec((1, 1, LANES), lambda bi, i: (bi, 0, 0)),
                  kvblk(LANES), kvblk(D_ATT), kvblk(D_ATT)],
        out_specs=qblk(D_ATT),
        out_shape=jax.ShapeDtypeStruct((b, t, D_ATT), BF16),
        scratch_shapes=[
            pltpu.VMEM((tq, l_pad), I32),
            pltpu.VMEM((tq, l_pad), BF16),
            per_pair(BF16),
            per_pair(BF16),
            per_head(LANES, F32),
            per_head(LANES, F32),
            pltpu.VMEM((tq, tk), F32),
            per_head(LANES, F32),
            per_pair(F32),
            per_head(1, F32),
            per_head(1, F32),
            pltpu.VMEM((N_HEADS // 2, 2 * tq, tk), F32),
            pltpu.VMEM((N_HEADS // 2, 2 * tq, tk), F32),
        ],
        compiler_params=_params("arbitrary", "arbitrary"),
    )(q, qi, kw, kmax, kiki, k_all, v_all)


def _cmul(ar, ai, br, bi):
    return ar * br - ai * bi, ar * bi + ai * br


def _s5_kernel(u_ref, h0re_ref, h0im_ref, are_ref, aim_ref, ldt_ref, bre_ref, bim_ref, cre_ref, cim_ref,
               dskip_ref, wglu_ref, bglu_ref, y_ref, hre_ref, him_ref,
               bbar_s, lvl_s, pw_s, h_s, *, ts, lane_chunk):
    n = N_STATE
    t_idx = pl.program_id(1)
    first = jnp.logical_and(pl.program_id(0) == 0, t_idx == 0)

    @pl.when(first)
    def _discretise():
        dt = jnp.exp(ldt_ref[...])
        ar, ai = are_ref[...], aim_ref[...]
        mag = jnp.exp(dt * ar)
        abr, abi = mag * jnp.cos(dt * ai), mag * jnp.sin(dt * ai)
        den = ar * ar + ai * ai
        nr, ni = abr - 1.0, abi
        f_re, f_im = (nr * ar + ni * ai) / den, (ni * ar - nr * ai) / den
        reps = D_SSM // SSM_GROUP
        bre = jnp.concatenate([bre_ref[...]] * reps, axis=0)
        bim = jnp.concatenate([bim_ref[...]] * reps, axis=0)
        r_grp = lax.broadcasted_iota(I32, (D_SSM, n), 0) // SSM_GROUP
        c_grp = lax.broadcasted_iota(I32, (D_SSM, n), 1) // SSM_STATE
        same = r_grp == c_grp
        bbar_s[:, :n] = jnp.where(same, f_re * bre - f_im * bim, 0.0).astype(BF16)
        bbar_s[:, n:] = jnp.where(same, f_re * bim + f_im * bre, 0.0).astype(BF16)
        a1 = (abr, abi)
        a2 = _cmul(*a1, *a1)
        a3 = _cmul(*a2, *a1)
        a4 = _cmul(*a2, *a2)
        a5 = _cmul(*a4, *a1)
        a6 = _cmul(*a4, *a2)
        a7 = _cmul(*a4, *a3)
        a8 = _cmul(*a4, *a4)
        rows = lax.broadcasted_iota(I32, (SUBLANES, n), 0)
        for part in range(2):
            pw = jnp.zeros((SUBLANES, n), F32)
            for r, a in enumerate((a1, a2, a3, a4, a5, a6, a7, a8)):
                pw = jnp.where(rows == r, a[part], pw)
            pw_s[part] = pw
            for lv, (a, dist) in enumerate(((a1, 1), (a2, 2), (a4, 4))):
                lvl_s[lv, part] = jnp.where(rows >= dist, a[part], 0.0)

    @pl.when(t_idx == 0)
    def _load_state():
        h_s[0:SUBLANES, :] = jnp.zeros((SUBLANES, 2 * n), F32)
        h_s[SUBLANES - 1:SUBLANES, :n] = h0re_ref[0]
        h_s[SUBLANES - 1:SUBLANES, n:] = h0im_ref[0]

    u = u_ref[0]
    u_bf = u.astype(BF16)
    n_slabs = D_SSM // LANES
    slab_w = n // n_slabs
    for part in range(2):
        for j in range(n_slabs):
            cols = slice(part * n + j * slab_w, part * n + (j + 1) * slab_w)
            h_s[SUBLANES:, cols] = jnp.dot(u_bf[:, j * LANES:(j + 1) * LANES], bbar_s[j * LANES:(j + 1) * LANES, cols],
                                           preferred_element_type=F32)

    def tile_step(j, carry):
        r0 = pl.multiple_of(j * SUBLANES, SUBLANES)
        for c in range(n // lane_chunk):
            re_sl = slice(c * lane_chunk, (c + 1) * lane_chunk)
            im_sl = slice(n + c * lane_chunk, n + (c + 1) * lane_chunk)
            xr = h_s[pl.ds(r0 + SUBLANES, SUBLANES), re_sl]
            xi = h_s[pl.ds(r0 + SUBLANES, SUBLANES), im_sl]
            for lv, dist in enumerate((1, 2, 4)):
                sr, si = pltpu.roll(xr, dist, 0), pltpu.roll(xi, dist, 0)
                dr, di = _cmul(lvl_s[lv, 0, :, re_sl], lvl_s[lv, 1, :, re_sl], sr, si)
                xr, xi = xr + dr, xi + di
            prev_r = h_s[pl.ds(r0, SUBLANES), re_sl][SUBLANES - 1:SUBLANES, :]
            prev_i = h_s[pl.ds(r0, SUBLANES), im_sl][SUBLANES - 1:SUBLANES, :]
            cr, ci = _cmul(pw_s[0, :, re_sl], pw_s[1, :, re_sl], prev_r, prev_i)
            h_s[pl.ds(r0 + SUBLANES, SUBLANES), re_sl] = xr + cr
            h_s[pl.ds(r0 + SUBLANES, SUBLANES), im_sl] = xi + ci
        return carry

    lax.fori_loop(0, ts // SUBLANES, tile_step, 0)

    nt = (((1,), (1,)), ((), ()))
    y_slabs = []
    for j in range(n_slabs):
        rows, cols = slice(j * LANES, (j + 1) * LANES), slice(j * slab_w, (j + 1) * slab_w)
        h_re = h_s[SUBLANES:, cols].astype(BF16)
        h_im = h_s[SUBLANES:, n + j * slab_w:n + (j + 1) * slab_w].astype(BF16)
        y_slabs.append(lax.dot_general(h_re, cre_ref[rows, cols], nt, preferred_element_type=F32)
                       - lax.dot_general(h_im, cim_ref[rows, cols], nt, preferred_element_type=F32))
    y = jnp.concatenate(y_slabs, axis=1)
    y = y + dskip_ref[...] * u
    y = jax.nn.gelu(y)
    gate = jax.nn.sigmoid(jnp.dot(y.astype(BF16), wglu_ref[...], preferred_element_type=F32) + bglu_ref[...])
    y_ref[0] = (y * gate).astype(y_ref.dtype)

    last = h_s[ts:ts + SUBLANES, :]
    h_s[0:SUBLANES, :] = last

    @pl.when(t_idx == pl.num_programs(1) - 1)
    def _emit_state():
        hre_ref[0] = last[SUBLANES - 1:SUBLANES, :n]
        him_ref[0] = last[SUBLANES - 1:SUBLANES, n:]


def _s5(u, h0_re, h0_im, prm, ts):
    b, t, _ = u.shape
    n = N_STATE
    a_re, a_im, ldt, bre_d, bim_d, cre_t, cim_t, d_skip, w_glu, b_glu = prm
    const = lambda a: _resident(a.shape, lambda bi, i: (0,) * a.ndim)
    st_blk = pl.BlockSpec((1, 1, n), lambda bi, i: (bi, 0, 0))
    y, hre, him = pl.pallas_call(
        functools.partial(_s5_kernel, ts=ts, lane_chunk=512),
        grid=(b, t // ts),
        in_specs=[pl.BlockSpec((1, ts, D_SSM), lambda bi, i: (bi, i, 0)), st_blk, st_blk,
                  const(a_re), const(a_im), const(ldt), const(bre_d), const(bim_d), const(cre_t), const(cim_t),
                  const(d_skip), const(w_glu), const(b_glu)],
        out_specs=[pl.BlockSpec((1, ts, D_SSM), lambda bi, i: (bi, i, 0)), st_blk, st_blk],
        out_shape=[jax.ShapeDtypeStruct((b, t, D_SSM), BF16),
                   jax.ShapeDtypeStruct((b, 1, n), F32), jax.ShapeDtypeStruct((b, 1, n), F32)],
        scratch_shapes=[
            pltpu.VMEM((D_SSM, 2 * n), BF16),
            pltpu.VMEM((3, 2, SUBLANES, n), F32),
            pltpu.VMEM((2, SUBLANES, n), F32),
            pltpu.VMEM((SUBLANES + ts, 2 * n), F32),
        ],
        compiler_params=_params("arbitrary", "arbitrary"),
    )(u, h0_re, h0_im, a_re, a_im, ldt, bre_d, bim_d, cre_t, cim_t, d_skip, w_glu, b_glu)
    return y, hre, him


def _layer_norm(z, g, b):
    mu = jnp.mean(z, axis=-1, keepdims=True)
    zc = z - mu
    var = jnp.mean(zc * zc, axis=-1, keepdims=True)
    return zc * lax.rsqrt(var + LN_EPS) * g + b


def _merge_kernel(x_ref, att_ref, ssm_ref, wg_ref, bg_ref, wpa_ref, wpb_ref, wo_ref, g_ref, b_ref, o_ref, *, alpha):
    x = x_ref[...]
    d = x.shape[1]
    gates = jax.nn.sigmoid(jnp.dot(x.astype(BF16), wg_ref[...], preferred_element_type=F32) + bg_ref[...])
    mix = (gates[:, :d] * jnp.dot(att_ref[...], wpa_ref[...], preferred_element_type=F32)
           + gates[:, d:] * jnp.dot(ssm_ref[...], wpb_ref[...], preferred_element_type=F32))
    z = alpha * x + jnp.dot(mix.astype(BF16), wo_ref[...], preferred_element_type=F32)
    o_ref[...] = _layer_norm(z, g_ref[...], b_ref[...])


def _merge(x2d, att2d, ssm2d, prm, alpha, tm):
    rows, d = x2d.shape
    w_g, b_g, w_pa, w_pb, w_o, ln_g, ln_b = prm
    row_blk = lambda w: pl.BlockSpec((tm, w), lambda i: (i, 0))
    const = lambda a: _resident(a.shape, lambda i: (0, 0))
    return pl.pallas_call(
        functools.partial(_merge_kernel, alpha=alpha),
        grid=(rows // tm,),
        in_specs=[row_blk(d), row_blk(D_ATT), row_blk(D_SSM)] + [const(a) for a in prm],
        out_specs=row_blk(d),
        out_shape=jax.ShapeDtypeStruct((rows, d), F32),
        compiler_params=_params("parallel"),
    )(x2d, att2d, ssm2d, *prm)


def _ffn_kernel(x_ref, cbuf_ref, wup_ref, cw_ref, cb_ref, wdn_ref, g_ref, b_ref, o_ref, nconv_ref,
                h_s, acc_s, *, tm, d_ff, fc, alpha):
    t_idx = pl.program_id(1)
    hist = CONV_W - 1

    @pl.when(t_idx == 0)
    def _load_history():
        h_s[0:SUBLANES, :] = jnp.zeros((SUBLANES, h_s.shape[1]), F32)
        h_s[SUBLANES - hist:SUBLANES, :] = cbuf_ref[0]

    x = x_ref[0]
    x_bf = x.astype(BF16)
    n_chunks = d_ff // fc
    column_pair = lambda c: (slice(c * fc, (c + 1) * fc), slice(d_ff + c * fc, d_ff + (c + 1) * fc))

    def up_project(c):
        for cols in column_pair(c):
            h_s[SUBLANES:, cols] = jnp.dot(x_bf, wup_ref[:, cols], preferred_element_type=F32)

    def conv_act_down(c):
        conv = [cb_ref[:, cols] + sum(cw_ref[j:j + 1, cols] * h_s[SUBLANES - hist + j:SUBLANES - hist + j + tm, cols]
                                      for j in range(CONV_W)) for cols in column_pair(c)]
        act = (jax.nn.gelu(conv[0]) * conv[1]).astype(BF16)
        part = jnp.dot(act, wdn_ref[c * fc:(c + 1) * fc, :], preferred_element_type=F32)
        if c == 0:
            acc_s[...] = part
        else:
            acc_s[...] += part

    for c in range(min(UP_AHEAD, n_chunks)):
        up_project(c)
    for c in range(n_chunks):
        if c + UP_AHEAD < n_chunks:
            up_project(c + UP_AHEAD)
        conv_act_down(c)
    o_ref[0] = _layer_norm(alpha * x + acc_s[...], g_ref[...], b_ref[...])

    tail = h_s[tm:tm + SUBLANES, :]
    h_s[0:SUBLANES, :] = tail

    @pl.when(t_idx == pl.num_programs(1) - 1)
    def _emit_history():
        nconv_ref[0] = tail[SUBLANES - hist:, :]


def _ffn(x, conv_buf, prm, alpha, tm):
    b, t, d = x.shape
    w_up, conv_w, conv_b, w_dn, ln_g, ln_b = prm
    d_ff = w_dn.shape[0]
    fc = 256
    assert d_ff % fc == 0 and t % tm == 0 and tm % SUBLANES == 0
    const = lambda a: _resident(a.shape, lambda bi, i: (0, 0))
    hist_blk = pl.BlockSpec((1, CONV_W - 1, 2 * d_ff), lambda bi, i: (bi, 0, 0))
    return pl.pallas_call(
        functools.partial(_ffn_kernel, tm=tm, d_ff=d_ff, fc=fc, alpha=alpha),
        grid=(b, t // tm),
        in_specs=[pl.BlockSpec((1, tm, d), lambda bi, i: (bi, i, 0)), hist_blk] + [const(a) for a in prm],
        out_specs=[pl.BlockSpec((1, tm, d), lambda bi, i: (bi, i, 0)), hist_blk],
        out_shape=[jax.ShapeDtypeStruct((b, t, d), F32), jax.ShapeDtypeStruct((b, CONV_W - 1, 2 * d_ff), F32)],
        scratch_shapes=[pltpu.VMEM((SUBLANES + tm, 2 * d_ff), F32),
                        pltpu.VMEM((tm, d), F32)],
        compiler_params=_params("arbitrary", "arbitrary"),
    )(x, conv_buf, *prm)


def _rope_tables(pos):
    half = HEAD_DIM // 2
    inv = ROPE_THETA ** (-jnp.arange(half, dtype=F32) / half)
    ang = pos.astype(F32)[:, None] * inv[None, :]
    cos, sin = jnp.cos(ang), jnp.sin(ang)
    cos_tab = jnp.tile(cos, (1, LANES // half))
    sin_tab = jnp.tile(jnp.concatenate([-sin, sin], axis=1), (1, LANES // HEAD_DIM))
    return cos_tab, sin_tab


def _layer_weights(w_in, b_gate, w_pa, w_pb, w_o, a_re, a_im, log_dt, b_re, b_im, c_re, c_im, d_skip, w_glu,
                   b_glu, ln1_g, ln1_b, w_up, conv_w, conv_b, w_down, ln2_g, ln2_b):
    d = w_in.shape[0]
    o = 0
    pieces = []
    for w in (D_ATT, D_ATT, D_ATT, N_IDX_HEADS * IDX_DIM, IDX_DIM + N_IDX_HEADS, D_SSM, 2 * d):
        pieces.append(w_in[:, o:o + w])
        o += w
    wq, wk, wv, wqi, wkw, wu, wg = pieces
    wkw = jnp.pad(wkw, ((0, 0), (0, LANES - wkw.shape[1])))
    inproj = tuple(w.astype(BF16) for w in (wq, wk, wv, wqi, wkw, wu))
    n = N_STATE
    row = lambda a: a.reshape(1, -1).astype(F32)
    same_group = jnp.eye(N_GROUPS, dtype=bool)[:, None, :, None]
    blockdiag_t = lambda c: jnp.where(same_group, c[:, :, None, :], 0.0).reshape(D_SSM, n).astype(BF16)
    dense = lambda bm: jnp.transpose(bm, (2, 0, 1)).reshape(SSM_GROUP, n).astype(F32)
    s5 = (row(a_re), row(a_im), row(jnp.repeat(log_dt, SSM_STATE)), dense(b_re), dense(b_im),
          blockdiag_t(c_re), blockdiag_t(c_im), row(d_skip), w_glu.astype(BF16), row(b_glu))
    merge = (wg.astype(BF16), row(b_gate), w_pa.astype(BF16), w_pb.astype(BF16), w_o.astype(BF16),
             row(ln1_g), row(ln1_b))
    ffn = (w_up.astype(BF16), conv_w.astype(F32), row(conv_b), w_down.astype(BF16), row(ln2_g), row(ln2_b))
    return inproj, s5, merge, ffn


def _largest_tile(n, cap):
    t = min(n, cap)
    while n % t:
        t -= SUBLANES
    return t


def _trunk_layer(x, past_k, past_v, past_ki, h0_re, h0_im, conv_buf, weights, alpha):
    bn, t, d = x.shape
    p = 0 if past_k is None else past_k.shape[1]
    n_keys = p + t
    k_keep = min(TOPK_MAX, n_keys // 4)
    inproj_w, s5_w, merge_w, ffn_w = weights
    rows = bn * t
    tm = _largest_tile(rows, 512)

    cos_tab, sin_tab = _rope_tables(p + jnp.arange(t))
    if t % tm == 0:
        n_tab_blocks = t // tm
    else:
        cos_tab, sin_tab = jnp.tile(cos_tab, (bn, 1)), jnp.tile(sin_tab, (bn, 1))
        n_tab_blocks = rows // tm
    q, k_f, k_b, v_f, v_b, qi, kw, u = _inproj(x.reshape(rows, d), cos_tab, sin_tab, inproj_w, tm, n_tab_blocks)

    r3 = lambda a: a.reshape(bn, t, a.shape[-1])
    ki_b = r3(kw)[:, :, :IDX_DIM].astype(BF16)
    k_all, v_all = r3(k_b), r3(v_b)
    if p:
        k_all = jnp.concatenate([past_k.reshape(bn, p, D_ATT).astype(BF16), k_all], axis=1)
        v_all = jnp.concatenate([past_v.reshape(bn, p, D_ATT).astype(BF16), v_all], axis=1)
        ki_b = jnp.concatenate([past_ki.astype(BF16), ki_b], axis=1)
    l_pad = -(-n_keys // LANES) * LANES
    l_pad = max(l_pad, 2 * LANES)
    pad = lambda a: jnp.pad(a, ((0, 0), (0, l_pad - n_keys), (0, 0)))
    kiki = jnp.concatenate([ki_b, ki_b], axis=-1)
    k_heads = k_all.astype(F32).reshape(bn, n_keys, N_HEADS, HEAD_DIM)
    kmax = jnp.sqrt(jnp.max(jnp.sum(k_heads * k_heads, axis=-1), axis=1))
    kmax = jnp.pad(kmax, ((0, 0), (0, LANES - N_HEADS))).reshape(bn, 1, LANES)
    tq = _largest_tile(t, 128)
    att = _attention(r3(q), r3(qi), r3(kw), kmax, pad(kiki), pad(k_all), pad(v_all),
                     n_keys=n_keys, past=p, k_keep=k_keep, tq=tq)

    ts = _largest_tile(t, 256)
    ssm, h_re, h_im = _s5(r3(u), h0_re.reshape(bn, 1, N_STATE), h0_im.reshape(bn, 1, N_STATE), s5_w, ts)

    x1 = _merge(x.reshape(rows, d), att.reshape(rows, D_ATT), ssm.reshape(rows, D_SSM), merge_w, alpha, tm)
    x2, new_conv = _ffn(x1.reshape(bn, t, d), conv_buf, ffn_w, alpha, _largest_tile(t, 256))

    k_out = r3(k_f).reshape(bn, t, N_HEADS, HEAD_DIM)
    v_out = r3(v_f).reshape(bn, t, N_HEADS, HEAD_DIM)
    ki_out = r3(kw)[:, :, :IDX_DIM]
    return (x2, k_out, v_out, ki_out, h_re.reshape(bn, N_GROUPS, SSM_STATE), h_im.reshape(bn, N_GROUPS, SSM_STATE),
            new_conv)


def kernel(x_prompt, x_sample, cache_k, cache_v, cache_idx_k, state_ssm_re, state_ssm_im, state_conv, w_in, b_gate, w_pa, w_pb, w_o, a_re, a_im, log_dt, b_re, b_im, c_re, c_im, d_skip, w_glu, b_glu, ln1_g, ln1_b, w_up, conv_w, conv_b, w_down, ln2_g, ln2_b):
    depth = w_in.shape[0]
    alpha = (2 * depth) ** 0.25
    bp = x_prompt.shape[0]
    d_ff2 = w_up.shape[2]
    zero_h = jnp.zeros((bp, N_GROUPS, SSM_STATE), F32)
    zero_conv = jnp.zeros((bp, CONV_W - 1, d_ff2), F32)
    layer_params = (w_in, b_gate, w_pa, w_pb, w_o, a_re, a_im, log_dt, b_re, b_im, c_re, c_im, d_skip, w_glu, b_glu,
                    ln1_g, ln1_b, w_up, conv_w, conv_b, w_down, ln2_g, ln2_b)
    xp, xs = x_prompt, x_sample
    outs_p, outs_s = [], []
    for l in range(depth):
        weights = _layer_weights(*(a[l] for a in layer_params))
        rp = _trunk_layer(xp, None, None, None, zero_h, zero_h, zero_conv, weights, alpha)
        rs = _trunk_layer(xs, cache_k[l], cache_v[l], cache_idx_k[l], state_ssm_re[l].astype(F32),
                          state_ssm_im[l].astype(F32), state_conv[l], weights, alpha)
        xp, xs = rp[0], rs[0]
        outs_p.append(rp[1:])
        outs_s.append(rs[1:])
    stack = lambda outs: tuple(jnp.stack([o[j] for o in outs], axis=0) for j in range(6))
    return (xp, xs) + stack(outs_p) + stack(outs_s)
```

```python
import functools

import jax
import jax.numpy as jnp
from jax import lax
from jax.experimental import pallas as pl
from jax.experimental.pallas import tpu as pltpu

CHUNK = 64
N_HEADS = 8
HEAD_DIM = 64
D_ATT = N_HEADS * HEAD_DIM
N_IDX_HEADS = 8
IDX_DIM = 64
TOPK_MAX = 256
D_SSM = 512
SSM_GROUP = 16
N_GROUPS = D_SSM // SSM_GROUP
SSM_STATE = 64
N_STATE = N_GROUPS * SSM_STATE
CONV_W = 3
ROPE_THETA = 10000.0
LN_EPS = 1e-5
LOG2_E = 1.4426950408889634

LANES = 128
SUBLANES = 8
VMEM_LIMIT = 60 * 1024 * 1024

F32 = jnp.float32
BF16 = jnp.bfloat16
I32 = jnp.int32
NEG_INF = float("-inf")
INT_MIN = -(2 ** 31)
HIGH16 = -(2 ** 16)
MIN_NORMAL_BITS = 0x00800000


def _f32_key_of_neg_inf():
    bits = 0xFF800000
    signed = bits - (1 << 32)
    return signed ^ ((signed >> 31) & 0x7FFFFFFF)


NEG_INF_KEY = _f32_key_of_neg_inf()
UP_AHEAD = 5
TILES_PER_TRIP = 4
SEARCH_UNROLL = 4
LOW_VALUE_BITS = 18
LOW_STEP_BITS = 2
LANE_KEEP = 3
SOFTMAX_SUM_FLOOR = 1e-30


def _resident(block_shape, index_map):
    return pl.BlockSpec(block_shape, index_map, pipeline_mode=pl.Buffered(1))


def _params(*sem):
    return pltpu.CompilerParams(dimension_semantics=sem, vmem_limit_bytes=VMEM_LIMIT)


def _inproj_kernel(x_ref, cos_ref, sin_ref, wq_ref, wk_ref, wv_ref, wqi_ref, wkw_ref, wu_ref,
                   q_ref, kf_ref, kb_ref, vf_ref, vb_ref, qi_ref, kw_ref, u_ref, *, wi_scale):
    x = x_ref[...].astype(BF16)
    cos = cos_ref[...]
    sin = sin_ref[...]
    tm = x.shape[0]
    lane = lax.broadcasted_iota(I32, (tm, LANES), 1)
    first_half = (lane % HEAD_DIM) < (HEAD_DIM // 2)

    def proj(w_ref):
        return jnp.dot(x, w_ref[...], preferred_element_type=F32)

    def rope_group(y):
        rot = jnp.where(first_half, pltpu.roll(y, LANES - HEAD_DIM // 2, 1), pltpu.roll(y, HEAD_DIM // 2, 1))
        return y * cos + rot * sin

    def rope(y):
        return jnp.concatenate([rope_group(y[:, c * LANES:(c + 1) * LANES]) for c in range(y.shape[1] // LANES)],
                               axis=1)

    q = rope(proj(wq_ref))
    q_ref[...] = (q * (HEAD_DIM ** -0.5 * LOG2_E)).astype(BF16)
    k = rope(proj(wk_ref))
    kf_ref[...] = k
    kb_ref[...] = k.astype(BF16)
    v = proj(wv_ref)
    vf_ref[...] = v
    vb_ref[...] = v.astype(BF16)
    qi_ref[...] = rope(proj(wqi_ref)).astype(BF16)
    kw = proj(wkw_ref)
    kw_ref[...] = jnp.where(lane < IDX_DIM, rope_group(kw), kw * wi_scale)
    u_ref[...] = proj(wu_ref)


def _inproj(x2d, cos_tab, sin_tab, ws, tm, n_tab_blocks):
    rows, d = x2d.shape
    wq, wk, wv, wqi, wkw, wu = ws
    row_blk = lambda w: pl.BlockSpec((tm, w), lambda i: (i, 0))
    tab_blk = pl.BlockSpec((tm, LANES), lambda i: (i % n_tab_blocks, 0))
    w_blk = lambda w: _resident(w.shape, lambda i: (0, 0))
    out_shapes = [
        jax.ShapeDtypeStruct((rows, D_ATT), BF16),
        jax.ShapeDtypeStruct((rows, D_ATT), F32),
        jax.ShapeDtypeStruct((rows, D_ATT), BF16),
        jax.ShapeDtypeStruct((rows, D_ATT), F32),
        jax.ShapeDtypeStruct((rows, D_ATT), BF16),
        jax.ShapeDtypeStruct((rows, D_ATT), BF16),
        jax.ShapeDtypeStruct((rows, LANES), F32),
        jax.ShapeDtypeStruct((rows, D_SSM), F32),
    ]
    wi_scale = (N_IDX_HEADS ** -0.5) * (IDX_DIM ** -0.5)
    return pl.pallas_call(
        functools.partial(_inproj_kernel, wi_scale=wi_scale),
        grid=(rows // tm,),
        in_specs=[row_blk(d), tab_blk, tab_blk, w_blk(wq), w_blk(wk), w_blk(wv), w_blk(wqi), w_blk(wkw), w_blk(wu)],
        out_specs=[row_blk(s.shape[1]) for s in out_shapes],
        out_shape=out_shapes,
        compiler_params=_params("parallel"),
    )(x2d, cos_tab, sin_tab, wq, wk, wv, wqi, wkw, wu)


def _attn_kernel(q_ref, qi_ref, kw_ref, kmax_ref, kiki_ref, k_ref, v_ref, o_ref,
                 key_s, keyhi_s, qm_s, qim_s, wi_s, mb_s, bias_s, l_s, acc_s, m2_s, l2_s, lga_s, lgb_s,
                 *, tq, tk, group, n_keys, past, k_keep):
    q0 = pl.program_id(1) * tq
    n_pairs = N_HEADS // 2
    n_chunks = tk // LANES
    row = lax.broadcasted_iota(I32, (tq, 1), 0)
    q_lim = jnp.minimum(((past + q0 + row) // CHUNK + 1) * CHUNK, n_keys)
    q_lim_b = jnp.broadcast_to(q_lim, (tq, LANES))
    blk_lim = jnp.minimum(((past + q0 + tq - 1) // CHUNK + 1) * CHUNK, n_keys)
    nk = (blk_lim + tk - 1) // tk
    lane_q = lax.broadcasted_iota(I32, (tq, LANES), 1)
    lo_half = lane_q < HEAD_DIM
    nt = (((1,), (1,)), ((), ()))
    pair_of = lambda h: slice((h // 2) * LANES, (h // 2 + 1) * LANES)
    chunk_of = lambda c: slice(c * LANES, (c + 1) * LANES)

    kw = kw_ref[0]
    kmax = kmax_ref[0]
    halves = (slice(0, tq), slice(tq, 2 * tq))
    for pr in range(n_pairs):
        qi_pair = qi_ref[0, :, pair_of(2 * pr)]
        q_pair = q_ref[0, :, pair_of(2 * pr)]
        for half, keep in enumerate((lo_half, jnp.logical_not(lo_half))):
            h = 2 * pr + half
            qim_s[pr, halves[half]] = jnp.where(keep, qi_pair, jnp.zeros_like(qi_pair))
            qm = jnp.where(keep, q_pair, jnp.zeros_like(q_pair))
            qm_s[pr, halves[half]] = qm
            wi_s[h] = jnp.broadcast_to(kw[:, IDX_DIM + h:IDX_DIM + h + 1], (tq, LANES))
            qf = qm.astype(F32)
            norm_sq = jnp.dot((qf * qf).astype(BF16), jnp.ones((LANES, LANES), BF16), preferred_element_type=F32)
            mb_s[h] = jnp.sqrt(norm_sq) * kmax[:, h:h + 1]

    def key_base(kt, c):
        return pl.multiple_of(kt * tk + c * LANES, LANES)

    def tile_start(kt):
        return pl.multiple_of(jnp.minimum(kt, nk - 1) * tk, tk)

    def two_stage_loop(produce, consume):
        produce(0, lga_s)

        def body(j, carry):
            for i in range(TILES_PER_TRIP):
                now, nxt = (lga_s, lgb_s) if i % 2 == 0 else (lgb_s, lga_s)
                produce(TILES_PER_TRIP * j + i + 1, nxt)
                consume(TILES_PER_TRIP * j + i, now)
            return carry

        lax.fori_loop(0, (nk + TILES_PER_TRIP - 1) // TILES_PER_TRIP, body, 0)

    def produce_scores(kt, buf):
        kiki = kiki_ref[0, pl.ds(tile_start(kt), tk), :]
        for pr in range(n_pairs):
            buf[pr] = lax.dot_general(qim_s[pr], kiki, nt, preferred_element_type=F32)

    def consume_scores(kt, buf):
        ktc = jnp.minimum(kt, nk - 1)
        score = [jnp.zeros((tq, LANES), F32) for _ in range(n_chunks)]
        for pr in range(n_pairs):
            for half in range(2):
                w = wi_s[2 * pr + half]
                for c in range(n_chunks):
                    score[c] = score[c] + w * jnp.maximum(buf[pr, halves[half], chunk_of(c)], 0.0)
        for c in range(n_chunks):
            kb = key_base(ktc, c)
            bits = pltpu.bitcast(jnp.where(kb + lane_q < q_lim_b, score[c], NEG_INF), I32)
            key_s[:, pl.ds(kb, LANES)] = bits ^ ((bits >> 31) & 0x7FFFFFFF)
            keyhi_s[:, pl.ds(kb, LANES)] = pltpu.bitcast(bits & HIGH16, F32).astype(BF16)

    two_stage_loop(produce_scores, consume_scores)

    n_groups = (nk + group - 1) // group

    def fill_tile(kt, carry):
        for c in range(n_chunks):
            kb = key_base(kt, c)
            key_s[:, pl.ds(kb, LANES)] = jnp.full((tq, LANES), NEG_INF_KEY, I32)
            keyhi_s[:, pl.ds(kb, LANES)] = jnp.full((tq, LANES), NEG_INF, BF16)
        return carry

    lax.fori_loop(nk, n_groups * group, fill_tile, 0)

    def group_chunks(g_idx):
        return [pl.multiple_of(g_idx * (group * tk) + c * LANES, LANES) for c in range(group * n_chunks)]

    def count(pred):
        def body(g_idx, cnt):
            for kb in group_chunks(g_idx):
                cnt = cnt + pred(key_s[:, pl.ds(kb, LANES)], kb).astype(I32)
            return cnt
        cnt = lax.fori_loop(0, n_groups, body, jnp.zeros((tq, LANES), I32))
        return jnp.sum(cnt, axis=1, keepdims=True)

    def count_ge(c):
        cb = jnp.broadcast_to(c, (tq, LANES))
        return count(lambda key, kb: key >= cb)

    def count_ge_high(cands):
        def lanes_of(c):
            c_bits = (c ^ ((c >> 31) & 0x7FFFFFFF)) & HIGH16
            c_bits = jnp.where(c_bits > 0, jnp.maximum(c_bits, MIN_NORMAL_BITS), c_bits)
            return jnp.broadcast_to(pltpu.bitcast(c_bits, F32).astype(BF16), (tq, LANES))

        cbs = [lanes_of(c) for c in cands]
        one, nil = jnp.ones((tq, LANES), BF16), jnp.zeros((tq, LANES), BF16)

        def body(g_idx, cnts):
            cnts = list(cnts)
            for kb in group_chunks(g_idx):
                x = keyhi_s[:, pl.ds(kb, LANES)]
                for i, cb in enumerate(cbs):
                    cnts[i] = cnts[i] + jnp.where(x >= cb, one, nil)
            return tuple(cnts)
        cnts = lax.fori_loop(0, n_groups, body, (nil,) * len(cbs))
        return [jnp.sum(c.astype(F32), axis=1, keepdims=True).astype(I32) for c in cnts]

    def visit(cand, cnt, st):
        t, thr, done, at_t = st
        t = jnp.where(cnt >= k_keep, cand, t)
        at_t = jnp.where(cnt >= k_keep, cnt, at_t)
        exact = cnt == k_keep
        thr = jnp.where(done == 0, jnp.where(exact, jnp.maximum(cand - 1, NEG_INF_KEY), thr), thr)
        done = jnp.where(exact, 1, done)
        return t, thr, done, at_t

    def search_loop(n_trips, trip, state):
        def cond(st):
            return jnp.logical_and(st[0] < n_trips, jnp.min(st[3]) == 0)
        return lax.while_loop(cond, lambda st: (st[0] + 1,) + trip(st[0], st[1:]), (jnp.int32(0),) + state)[1:]

    def search(counter, first_bit, n_bits, per_trip, state):
        def trip(step, st):
            for i in range(per_trip):
                cand = st[0] + jnp.left_shift(jnp.int32(1), first_bit - i - step * per_trip)
                st = visit(cand, counter(cand), st)
            return st
        return search_loop(n_bits // per_trip, trip, state)

    zero = jnp.zeros((tq, 1), I32)
    high_state = search(lambda c: count_ge_high([c])[0], 31, 16, 16,
                        (jnp.full((tq, 1), INT_MIN, I32), zero, zero, zero))
    t_high = high_state[0]
    lane_sum = lambda a: jnp.sum(a.astype(F32), axis=1, keepdims=True).astype(I32)

    def low_half(_):
        above, from_bucket = count_ge_high([t_high + 2 ** 16, t_high])
        members = from_bucket - above
        bucket = jnp.broadcast_to(t_high, (tq, LANES))

        def keep_largest(g_idx, kept):
            kept = list(kept)
            for kb in group_chunks(g_idx):
                key = key_s[:, pl.ds(kb, LANES)]
                v = jnp.where((key & HIGH16) == bucket, (key & 0xFFFF) + 1, 0)
                for i in range(LANE_KEEP):
                    kept[i], v = jnp.maximum(kept[i], v), jnp.minimum(kept[i], v)
            return tuple(kept)

        nil = jnp.zeros((tq, LANES), I32)
        kept = lax.fori_loop(0, n_groups, keep_largest, (nil,) * LANE_KEEP)
        count_low = lambda pred: lane_sum(functools.reduce(jnp.add, [pred(m).astype(I32) for m in kept]))
        caught_all = jnp.min((count_low(lambda m: m > 0) == members).astype(I32)) == 1

        def from_lanes(_):
            want_low = k_keep - above
            low = zero
            for step in range(LOW_VALUE_BITS // LOW_STEP_BITS):
                unit = 1 << (LOW_VALUE_BITS - LOW_STEP_BITS * (step + 1))
                taken = zero
                low_lanes = jnp.broadcast_to(low, (tq, LANES))
                for mult in range(1, 2 ** LOW_STEP_BITS):
                    cand = low_lanes + mult * unit
                    taken = taken + (count_low(lambda m: m >= cand) >= want_low).astype(I32)
                low = low + taken * unit
            low = jnp.maximum(low, 1)
            low_b = jnp.broadcast_to(low, (tq, LANES))
            return t_high + low - 1, high_state[1], high_state[2], above + count_low(lambda m: m >= low_b)

        return lax.cond(caught_all, from_lanes, lambda _: search(count_ge, 15, 16, SEARCH_UNROLL, high_state), 0)

    t_fin, thr_early, done, at_t = lax.cond(jnp.min(high_state[2]) == 0, low_half, lambda _: high_state, 0)
    t_fin = jnp.maximum(t_fin, NEG_INF_KEY)

    def resolve_ties(_):
        tb = jnp.broadcast_to(t_fin, (tq, LANES))
        need = jnp.logical_and(jnp.logical_and(done == 0, at_t > k_keep), t_fin > NEG_INF_KEY)

        def cut_search(_):
            want = (k_keep - count_ge(t_fin + 1)).astype(F32)
            upper = jnp.where(lax.broadcasted_iota(I32, (tk, tk), 0) <= lax.broadcasted_iota(I32, (tk, tk), 1),
                              1.0, 0.0).astype(BF16)

            def body(g_idx, st):
                before, last = st
                bases = group_chunks(g_idx)
                for i in range(group):
                    tile_bases = bases[i * n_chunks:(i + 1) * n_chunks]
                    hits = [key_s[:, pl.ds(kb, LANES)] == tb for kb in tile_bases]
                    ones = jnp.concatenate([jnp.where(h, 1.0, 0.0) for h in hits], axis=1)
                    rank = before + jnp.dot(ones.astype(BF16), upper, preferred_element_type=F32)
                    for c, kb in enumerate(tile_bases):
                        take = jnp.logical_and(hits[c], rank[:, chunk_of(c)] <= want)
                        last = jnp.maximum(last, jnp.where(take, kb + lane_q, -1))
                    before = before + jnp.sum(ones, axis=1, keepdims=True)
                return before, last

            _, last = lax.fori_loop(0, n_groups, body, (jnp.zeros((tq, 1), F32), jnp.full((tq, LANES), -1, I32)))
            return jnp.max(last, axis=1, keepdims=True)

        cut = lax.cond(jnp.max(need.astype(I32)) > 0, cut_search, lambda _: zero, 0)
        all_ties = jnp.where(t_fin > NEG_INF_KEY, jnp.int32(2 ** 30), jnp.int32(-1))
        return jnp.where(need, cut, all_ties)

    cut_full = lax.cond(jnp.min(done) == 0, resolve_ties, lambda _: jnp.full((tq, 1), -1, I32), 0)
    thr = jnp.where(done == 1, thr_early, t_fin)
    cut = jnp.where(done == 1, -1, cut_full)
    thr_b = jnp.broadcast_to(thr, (tq, LANES))
    cut_b = jnp.broadcast_to(cut, (tq, LANES))

    def tile_bias(kt, valid):
        thr_t = jnp.where(valid, thr_b, jnp.int32(2 ** 31 - 1))
        cut_t = jnp.where(valid, cut_b, jnp.int32(-1))
        for c in range(n_chunks):
            kb = key_base(kt, c)
            key = key_s[:, pl.ds(kb, LANES)]
            bound = jnp.where(kb + lane_q <= cut_t, thr_t - 1, thr_t)
            bias_s[:, chunk_of(c)] = jnp.where(key > bound, 0.0, NEG_INF)

    def emit(l_of_head):
        for pr in range(n_pairs):
            even = acc_s[pr, halves[0]] / l_of_head(2 * pr)
            odd = acc_s[pr, halves[1]] / l_of_head(2 * pr + 1)
            o_ref[0, :, pair_of(2 * pr)] = jnp.where(lo_half, even, odd).astype(o_ref.dtype)

    l_s[...] = jnp.zeros(l_s.shape, F32)
    acc_s[...] = jnp.zeros(acc_s.shape, F32)

    def produce_logits(kt, buf):
        k0 = tile_start(kt)
        for pr in range(n_pairs):
            buf[pr] = lax.dot_general(qm_s[pr], k_ref[0, pl.ds(k0, tk), pair_of(2 * pr)], nt,
                                      preferred_element_type=F32)

    def consume_logits(kt, buf):
        k0 = tile_start(kt)
        tile_bias(jnp.minimum(kt, nk - 1), kt < nk)
        for pr in range(n_pairs):
            p = []
            for half in range(2):
                h = 2 * pr + half
                shift = mb_s[h]
                l_part = l_s[h]
                p_row = []
                for c in range(n_chunks):
                    e = jnp.exp2(buf[pr, halves[half], chunk_of(c)] + bias_s[:, chunk_of(c)] - shift)
                    l_part = l_part + e
                    p_row.append(e.astype(BF16))
                l_s[h] = l_part
                p.append(jnp.concatenate(p_row, axis=1))
            acc_s[pr] += jnp.dot(jnp.concatenate(p, axis=0), v_ref[0, pl.ds(k0, tk), pair_of(2 * pr)],
                                 preferred_element_type=F32)

    two_stage_loop(produce_logits, consume_logits)
    l_rows = [jnp.sum(l_s[h], axis=1, keepdims=True) for h in range(N_HEADS)]
    l_min = functools.reduce(jnp.minimum, [jnp.min(l) for l in l_rows])
    well_scaled = l_min >= SOFTMAX_SUM_FLOOR

    @pl.when(well_scaled)
    def _emit_bound():
        emit(lambda h: l_rows[h])

    @pl.when(jnp.logical_not(well_scaled))
    def _running_max():
        m2_s[...] = jnp.full(m2_s.shape, NEG_INF, F32)
        l2_s[...] = jnp.zeros(l2_s.shape, F32)
        acc_s[...] = jnp.zeros(acc_s.shape, F32)

        def online_tile(kt, carry):
            k0 = pl.multiple_of(kt * tk, tk)
            tile_bias(kt, True)
            for h in range(N_HEADS):
                rows = halves[h % 2]
                logits = lax.dot_general(qm_s[h // 2, rows], k_ref[0, pl.ds(k0, tk), pair_of(h)], nt,
                                         preferred_element_type=F32) + bias_s[...]
                m_old = m2_s[h]
                m_new = jnp.maximum(m_old, jnp.max(logits, axis=1, keepdims=True))
                m_safe = jnp.where(m_new == NEG_INF, 0.0, m_new)
                alpha = jnp.exp2(m_old - m_safe)
                p = jnp.exp2(logits - m_safe)
                l2_s[h] = alpha * l2_s[h] + jnp.sum(p, axis=1, keepdims=True)
                acc_s[h // 2, rows] = alpha * acc_s[h // 2, rows] + jnp.dot(
                    p.astype(BF16), v_ref[0, pl.ds(k0, tk), pair_of(h)], preferred_element_type=F32)
                m2_s[h] = m_new
            return carry

        lax.fori_loop(0, nk, online_tile, 0)
        emit(lambda h: l2_s[h])


def _pick_key_tile(l_pad, k_keep):
    for tk in (256, 384, 512, 640, 768, 896, 1024):
        if l_pad % tk == 0 and tk >= k_keep:
            return tk
    return l_pad


def _attention(q, qi, kw, kmax, kiki, k_all, v_all, *, n_keys, past, k_keep, tq):
    b, t, _ = q.shape
    l_pad = k_all.shape[1]
    tk = _pick_key_tile(l_pad, k_keep)
    assert l_pad % tk == 0 and tk >= k_keep and t % tq == 0
    assert l_pad // LANES <= 256
    group = max(g for g in (4, 2, 1) if l_pad % (g * tk) == 0)
    qblk = lambda w: pl.BlockSpec((1, tq, w), lambda bi, i: (bi, i, 0))
    kvblk = lambda w: _resident((1, l_pad, w), lambda bi, i: (bi, 0, 0))
    per_head = lambda w, dt: pltpu.VMEM((N_HEADS, tq, w), dt)
    per_pair = lambda dt: pltpu.VMEM((N_HEADS // 2, 2 * tq, LANES), dt)
    return pl.pallas_call(
        functools.partial(_attn_kernel, tq=tq, tk=tk, group=group, n_keys=n_keys, past=past, k_keep=k_keep),
        grid=(b, t // tq),
        in_specs=[qblk(D_ATT), qblk(D_ATT), qblk(LANES), pl.BlockSpec((1, 1, LANES), lambda bi, i: (bi, 0, 0)),
                  kvblk(LANES), kvblk(D_ATT), kvblk(D_ATT)],
        out_specs=qblk(D_ATT),
        out_shape=jax.ShapeDtypeStruct((b, t, D_ATT), BF16),
        scratch_shapes=[
            pltpu.VMEM((tq, l_pad), I32),
            pltpu.VMEM((tq, l_pad), BF16),
            per_pair(BF16),
            per_pair(BF16),
            per_head(LANES, F32),
            per_head(LANES, F32),
            pltpu.VMEM((tq, tk), F32),
            per_head(LANES, F32),
            per_pair(F32),
            per_head(1, F32),
            per_head(1, F32),
            pltpu.VMEM((N_HEADS // 2, 2 * tq, tk), F32),
            pltpu.VMEM((N_HEADS // 2, 2 * tq, tk), F32),
        ],
        compiler_params=_params("arbitrary", "arbitrary"),
    )(q, qi, kw, kmax, kiki, k_all, v_all)


def _cmul(ar, ai, br, bi):
    return ar * br - ai * bi, ar * bi + ai * br


def _s5_kernel(u_ref, h0re_ref, h0im_ref, are_ref, aim_ref, ldt_ref, bre_ref, bim_ref, cre_ref, cim_ref,
               dskip_ref, wglu_ref, bglu_ref, y_ref, hre_ref, him_ref,
               bbar_s, lvl_s, pw_s, h_s, *, ts, lane_chunk):
    n = N_STATE
    t_idx = pl.program_id(1)
    first = jnp.logical_and(pl.program_id(0) == 0, t_idx == 0)

    @pl.when(first)
    def _discretise():
        dt = jnp.exp(ldt_ref[...])
        ar, ai = are_ref[...], aim_ref[...]
        mag = jnp.exp(dt * ar)
        abr, abi = mag * jnp.cos(dt * ai), mag * jnp.sin(dt * ai)
        den = ar * ar + ai * ai
        nr, ni = abr - 1.0, abi
        f_re, f_im = (nr * ar + ni * ai) / den, (ni * ar - nr * ai) / den
        reps = D_SSM // SSM_GROUP
        bre = jnp.concatenate([bre_ref[...]] * reps, axis=0)
        bim = jnp.concatenate([bim_ref[...]] * reps, axis=0)
        r_grp = lax.broadcasted_iota(I32, (D_SSM, n), 0) // SSM_GROUP
        c_grp = lax.broadcasted_iota(I32, (D_SSM, n), 1) // SSM_STATE
        same = r_grp == c_grp
        bbar_s[:, :n] = jnp.where(same, f_re * bre - f_im * bim, 0.0).astype(BF16)
        bbar_s[:, n:] = jnp.where(same, f_re * bim + f_im * bre, 0.0).astype(BF16)
        a1 = (abr, abi)
        a2 = _cmul(*a1, *a1)
        a3 = _cmul(*a2, *a1)
        a4 = _cmul(*a2, *a2)
        a5 = _cmul(*a4, *a1)
        a6 = _cmul(*a4, *a2)
        a7 = _cmul(*a4, *a3)
        a8 = _cmul(*a4, *a4)
        rows = lax.broadcasted_iota(I32, (SUBLANES, n), 0)
        for part in range(2):
            pw = jnp.zeros((SUBLANES, n), F32)
            for r, a in enumerate((a1, a2, a3, a4, a5, a6, a7, a8)):
                pw = jnp.where(rows == r, a[part], pw)
            pw_s[part] = pw
            for lv, (a, dist) in enumerate(((a1, 1), (a2, 2), (a4, 4))):
                lvl_s[lv, part] = jnp.where(rows >= dist, a[part], 0.0)

    @pl.when(t_idx == 0)
    def _load_state():
        h_s[0:SUBLANES, :] = jnp.zeros((SUBLANES, 2 * n), F32)
        h_s[SUBLANES - 1:SUBLANES, :n] = h0re_ref[0]
        h_s[SUBLANES - 1:SUBLANES, n:] = h0im_ref[0]

    u = u_ref[0]
    u_bf = u.astype(BF16)
    n_slabs = D_SSM // LANES
    slab_w = n // n_slabs
    for part in range(2):
        for j in range(n_slabs):
            cols = slice(part * n + j * slab_w, part * n + (j + 1) * slab_w)
            h_s[SUBLANES:, cols] = jnp.dot(u_bf[:, j * LANES:(j + 1) * LANES], bbar_s[j * LANES:(j + 1) * LANES, cols],
                                           preferred_element_type=F32)

    def tile_step(j, carry):
        r0 = pl.multiple_of(j * SUBLANES, SUBLANES)
        for c in range(n // lane_chunk):
            re_sl = slice(c * lane_chunk, (c + 1) * lane_chunk)
            im_sl = slice(n + c * lane_chunk, n + (c + 1) * lane_chunk)
            xr = h_s[pl.ds(r0 + SUBLANES, SUBLANES), re_sl]
            xi = h_s[pl.ds(r0 + SUBLANES, SUBLANES), im_sl]
            for lv, dist in enumerate((1, 2, 4)):
                sr, si = pltpu.roll(xr, dist, 0), pltpu.roll(xi, dist, 0)
                dr, di = _cmul(lvl_s[lv, 0, :, re_sl], lvl_s[lv, 1, :, re_sl], sr, si)
                xr, xi = xr + dr, xi + di
            prev_r = h_s[pl.ds(r0, SUBLANES), re_sl][SUBLANES - 1:SUBLANES, :]
            prev_i = h_s[pl.ds(r0, SUBLANES), im_sl][SUBLANES - 1:SUBLANES, :]
            cr, ci = _cmul(pw_s[0, :, re_sl], pw_s[1, :, re_sl], prev_r, prev_i)
            h_s[pl.ds(r0 + SUBLANES, SUBLANES), re_sl] = xr + cr
            h_s[pl.ds(r0 + SUBLANES, SUBLANES), im_sl] = xi + ci
        return carry

    lax.fori_loop(0, ts // SUBLANES, tile_step, 0)

    nt = (((1,), (1,)), ((), ()))
    y_slabs = []
    for j in range(n_slabs):
        rows, cols = slice(j * LANES, (j + 1) * LANES), slice(j * slab_w, (j + 1) * slab_w)
        h_re = h_s[SUBLANES:, cols].astype(BF16)
        h_im = h_s[SUBLANES:, n + j * slab_w:n + (j + 1) * slab_w].astype(BF16)
        y_slabs.append(lax.dot_general(h_re, cre_ref[rows, cols], nt, preferred_element_type=F32)
                       - lax.dot_general(h_im, cim_ref[rows, cols], nt, preferred_element_type=F32))
    y = jnp.concatenate(y_slabs, axis=1)
    y = y + dskip_ref[...] * u
    y = jax.nn.gelu(y)
    gate = jax.nn.sigmoid(jnp.dot(y.astype(BF16), wglu_ref[...], preferred_element_type=F32) + bglu_ref[...])
    y_ref[0] = (y * gate).astype(y_ref.dtype)

    last = h_s[ts:ts + SUBLANES, :]
    h_s[0:SUBLANES, :] = last

    @pl.when(t_idx == pl.num_programs(1) - 1)
    def _emit_state():
        hre_ref[0] = last[SUBLANES - 1:SUBLANES, :n]
        him_ref[0] = last[SUBLANES - 1:SUBLANES, n:]


def _s5(u, h0_re, h0_im, prm, ts):
    b, t, _ = u.shape
    n = N_STATE
    a_re, a_im, ldt, bre_d, bim_d, cre_t, cim_t, d_skip, w_glu, b_glu = prm
    const = lambda a: _resident(a.shape, lambda bi, i: (0,) * a.ndim)
    st_blk = pl.BlockSpec((1, 1, n), lambda bi, i: (bi, 0, 0))
    y, hre, him = pl.pallas_call(
        functools.partial(_s5_kernel, ts=ts, lane_chunk=512),
        grid=(b, t // ts),
        in_specs=[pl.BlockSpec((1, ts, D_SSM), lambda bi, i: (bi, i, 0)), st_blk, st_blk,
                  const(a_re), const(a_im), const(ldt), const(bre_d), const(bim_d), const(cre_t), const(cim_t),
                  const(d_skip), const(w_glu), const(b_glu)],
        out_specs=[pl.BlockSpec((1, ts, D_SSM), lambda bi, i: (bi, i, 0)), st_blk, st_blk],
        out_shape=[jax.ShapeDtypeStruct((b, t, D_SSM), BF16),
                   jax.ShapeDtypeStruct((b, 1, n), F32), jax.ShapeDtypeStruct((b, 1, n), F32)],
        scratch_shapes=[
            pltpu.VMEM((D_SSM, 2 * n), BF16),
            pltpu.VMEM((3, 2, SUBLANES, n), F32),
            pltpu.VMEM((2, SUBLANES, n), F32),
            pltpu.VMEM((SUBLANES + ts, 2 * n), F32),
        ],
        compiler_params=_params("arbitrary", "arbitrary"),
    )(u, h0_re, h0_im, a_re, a_im, ldt, bre_d, bim_d, cre_t, cim_t, d_skip, w_glu, b_glu)
    return y, hre, him


def _layer_norm(z, g, b):
    mu = jnp.mean(z, axis=-1, keepdims=True)
    zc = z - mu
    var = jnp.mean(zc * zc, axis=-1, keepdims=True)
    return zc * lax.rsqrt(var + LN_EPS) * g + b


def _merge_kernel(x_ref, att_ref, ssm_ref, wg_ref, bg_ref, wpa_ref, wpb_ref, wo_ref, g_ref, b_ref, o_ref, *, alpha):
    x = x_ref[...]
    d = x.shape[1]
    gates = jax.nn.sigmoid(jnp.dot(x.astype(BF16), wg_ref[...], preferred_element_type=F32) + bg_ref[...])
    mix = (gates[:, :d] * jnp.dot(att_ref[...], wpa_ref[...], preferred_element_type=F32)
           + gates[:, d:] * jnp.dot(ssm_ref[...], wpb_ref[...], preferred_element_type=F32))
    z = alpha * x + jnp.dot(mix.astype(BF16), wo_ref[...], preferred_element_type=F32)
    o_ref[...] = _layer_norm(z, g_ref[...], b_ref[...])


def _merge(x2d, att2d, ssm2d, prm, alpha, tm):
    rows, d = x2d.shape
    w_g, b_g, w_pa, w_pb, w_o, ln_g, ln_b = prm
    row_blk = lambda w: pl.BlockSpec((tm, w), lambda i: (i, 0))
    const = lambda a: _resident(a.shape, lambda i: (0, 0))
    return pl.pallas_call(
        functools.partial(_merge_kernel, alpha=alpha),
        grid=(rows // tm,),
        in_specs=[row_blk(d), row_blk(D_ATT), row_blk(D_SSM)] + [const(a) for a in prm],
        out_specs=row_blk(d),
        out_shape=jax.ShapeDtypeStruct((rows, d), F32),
        compiler_params=_params("parallel"),
    )(x2d, att2d, ssm2d, *prm)


def _ffn_kernel(x_ref, cbuf_ref, wup_ref, cw_ref, cb_ref, wdn_ref, g_ref, b_ref, o_ref, nconv_ref,
                h_s, acc_s, *, tm, d_ff, fc, alpha):
    t_idx = pl.program_id(1)
    hist = CONV_W - 1

    @pl.when(t_idx == 0)
    def _load_history():
        h_s[0:SUBLANES, :] = jnp.zeros((SUBLANES, h_s.shape[1]), F32)
        h_s[SUBLANES - hist:SUBLANES, :] = cbuf_ref[0]

    x = x_ref[0]
    x_bf = x.astype(BF16)
    n_chunks = d_ff // fc
    column_pair = lambda c: (slice(c * fc, (c + 1) * fc), slice(d_ff + c * fc, d_ff + (c + 1) * fc))

    def up_project(c):
        for cols in column_pair(c):
            h_s[SUBLANES:, cols] = jnp.dot(x_bf, wup_ref[:, cols], preferred_element_type=F32)

    def conv_act_down(c):
        conv = [cb_ref[:, cols] + sum(cw_ref[j:j + 1, cols] * h_s[SUBLANES - hist + j:SUBLANES - hist + j + tm, cols]
                                      for j in range(CONV_W)) for cols in column_pair(c)]
        act = (jax.nn.gelu(conv[0]) * conv[1]).astype(BF16)
        part = jnp.dot(act, wdn_ref[c * fc:(c + 1) * fc, :], preferred_element_type=F32)
        if c == 0:
            acc_s[...] = part
        else:
            acc_s[...] += part

    for c in range(min(UP_AHEAD, n_chunks)):
        up_project(c)
    for c in range(n_chunks):
        if c + UP_AHEAD < n_chunks:
            up_project(c + UP_AHEAD)
        conv_act_down(c)
    o_ref[0] = _layer_norm(alpha * x + acc_s[...], g_ref[...], b_ref[...])

    tail = h_s[tm:tm + SUBLANES, :]
    h_s[0:SUBLANES, :] = tail

    @pl.when(t_idx == pl.num_programs(1) - 1)
    def _emit_history():
        nconv_ref[0] = tail[SUBLANES - hist:, :]


def _ffn(x, conv_buf, prm, alpha, tm):
    b, t, d = x.shape
    w_up, conv_w, conv_b, w_dn, ln_g, ln_b = prm
    d_ff = w_dn.shape[0]
    fc = 256
    assert d_ff % fc == 0 and t % tm == 0 and tm % SUBLANES == 0
    const = lambda a: _resident(a.shape, lambda bi, i: (0, 0))
    hist_blk = pl.BlockSpec((1, CONV_W - 1, 2 * d_ff), lambda bi, i: (bi, 0, 0))
    return pl.pallas_call(
        functools.partial(_ffn_kernel, tm=tm, d_ff=d_ff, fc=fc, alpha=alpha),
        grid=(b, t // tm),
        in_specs=[pl.BlockSpec((1, tm, d), lambda bi, i: (bi, i, 0)), hist_blk] + [const(a) for a in prm],
        out_specs=[pl.BlockSpec((1, tm, d), lambda bi, i: (bi, i, 0)), hist_blk],
        out_shape=[jax.ShapeDtypeStruct((b, t, d), F32), jax.ShapeDtypeStruct((b, CONV_W - 1, 2 * d_ff), F32)],
        scratch_shapes=[pltpu.VMEM((SUBLANES + tm, 2 * d_ff), F32),
                        pltpu.VMEM((tm, d), F32)],
        compiler_params=_params("arbitrary", "arbitrary"),
    )(x, conv_buf, *prm)


def _rope_tables(pos):
    half = HEAD_DIM // 2
    inv = ROPE_THETA ** (-jnp.arange(half, dtype=F32) / half)
    ang = pos.astype(F32)[:, None] * inv[None, :]
    cos, sin = jnp.cos(ang), jnp.sin(ang)
    cos_tab = jnp.tile(cos, (1, LANES // half))
    sin_tab = jnp.tile(jnp.concatenate([-sin, sin], axis=1), (1, LANES // HEAD_DIM))
    return cos_tab, sin_tab


def _layer_weights(w_in, b_gate, w_pa, w_pb, w_o, a_re, a_im, log_dt, b_re, b_im, c_re, c_im, d_skip, w_glu,
                   b_glu, ln1_g, ln1_b, w_up, conv_w, conv_b, w_down, ln2_g, ln2_b):
    d = w_in.shape[0]
    o = 0
    pieces = []
    for w in (D_ATT, D_ATT, D_ATT, N_IDX_HEADS * IDX_DIM, IDX_DIM + N_IDX_HEADS, D_SSM, 2 * d):
        pieces.append(w_in[:, o:o + w])
        o += w
    wq, wk, wv, wqi, wkw, wu, wg = pieces
    wkw = jnp.pad(wkw, ((0, 0), (0, LANES - wkw.shape[1])))
    inproj = tuple(w.astype(BF16) for w in (wq, wk, wv, wqi, wkw, wu))
    n = N_STATE
    row = lambda a: a.reshape(1, -1).astype(F32)
    same_group = jnp.eye(N_GROUPS, dtype=bool)[:, None, :, None]
    blockdiag_t = lambda c: jnp.where(same_group, c[:, :, None, :], 0.0).reshape(D_SSM, n).astype(BF16)
    dense = lambda bm: jnp.transpose(bm, (2, 0, 1)).reshape(SSM_GROUP, n).astype(F32)
    s5 = (row(a_re), row(a_im), row(jnp.repeat(log_dt, SSM_STATE)), dense(b_re), dense(b_im),
          blockdiag_t(c_re), blockdiag_t(c_im), row(d_skip), w_glu.astype(BF16), row(b_glu))
    merge = (wg.astype(BF16), row(b_gate), w_pa.astype(BF16), w_pb.astype(BF16), w_o.astype(BF16),
             row(ln1_g), row(ln1_b))
    ffn = (w_up.astype(BF16), conv_w.astype(F32), row(conv_b), w_down.astype(BF16), row(ln2_g), row(ln2_b))
    return inproj, s5, merge, ffn


def _largest_tile(n, cap):
    t = min(n, cap)
    while n % t:
        t -= SUBLANES
    return t


def _trunk_layer(x, past_k, past_v, past_ki, h0_re, h0_im, conv_buf, weights, alpha):
    bn, t, d = x.shape
    p = 0 if past_k is None else past_k.shape[1]
    n_keys = p + t
    k_keep = min(TOPK_MAX, n_keys // 4)
    inproj_w, s5_w, merge_w, ffn_w = weights
    rows = bn * t
    tm = _largest_tile(rows, 512)

    cos_tab, sin_tab = _rope_tables(p + jnp.arange(t))
    if t % tm == 0:
        n_tab_blocks = t // tm
    else:
        cos_tab, sin_tab = jnp.tile(cos_tab, (bn, 1)), jnp.tile(sin_tab, (bn, 1))
        n_tab_blocks = rows // tm
    q, k_f, k_b, v_f, v_b, qi, kw, u = _inproj(x.reshape(rows, d), cos_tab, sin_tab, inproj_w, tm, n_tab_blocks)

    r3 = lambda a: a.reshape(bn, t, a.shape[-1])
    ki_b = r3(kw)[:, :, :IDX_DIM].astype(BF16)
    k_all, v_all = r3(k_b), r3(v_b)
    if p:
        k_all = jnp.concatenate([past_k.reshape(bn, p, D_ATT).astype(BF16), k_all], axis=1)
        v_all = jnp.concatenate([past_v.reshape(bn, p, D_ATT).astype(BF16), v_all], axis=1)
        ki_b = jnp.concatenate([past_ki.astype(BF16), ki_b], axis=1)
    l_pad = -(-n_keys // LANES) * LANES
    l_pad = max(l_pad, 2 * LANES)
    pad = lambda a: jnp.pad(a, ((0, 0), (0, l_pad - n_keys), (0, 0)))
    kiki = jnp.concatenate([ki_b, ki_b], axis=-1)
    k_heads = k_all.astype(F32).reshape(bn, n_keys, N_HEADS, HEAD_DIM)
    kmax = jnp.sqrt(jnp.max(jnp.sum(k_heads * k_heads, axis=-1), axis=1))
    kmax = jnp.pad(kmax, ((0, 0), (0, LANES - N_HEADS))).reshape(bn, 1, LANES)
    tq = _largest_tile(t, 128)
    att = _attention(r3(q), r3(qi), r3(kw), kmax, pad(kiki), pad(k_all), pad(v_all),
                     n_keys=n_keys, past=p, k_keep=k_keep, tq=tq)

    ts = _largest_tile(t, 256)
    ssm, h_re, h_im = _s5(r3(u), h0_re.reshape(bn, 1, N_STATE), h0_im.reshape(bn, 1, N_STATE), s5_w, ts)

    x1 = _merge(x.reshape(rows, d), att.reshape(rows, D_ATT), ssm.reshape(rows, D_SSM), merge_w, alpha, tm)
    x2, new_conv = _ffn(x1.reshape(bn, t, d), conv_buf, ffn_w, alpha, _largest_tile(t, 256))

    k_out = r3(k_f).reshape(bn, t, N_HEADS, HEAD_DIM)
    v_out = r3(v_f).reshape(bn, t, N_HEADS, HEAD_DIM)
    ki_out = r3(kw)[:, :, :IDX_DIM]
    return (x2, k_out, v_out, ki_out, h_re.reshape(bn, N_GROUPS, SSM_STATE), h_im.reshape(bn, N_GROUPS, SSM_STATE),
            new_conv)


def kernel(x_prompt, x_sample, cache_k, cache_v, cache_idx_k, state_ssm_re, state_ssm_im, state_conv, w_in, b_gate, w_pa, w_pb, w_o, a_re, a_im, log_dt, b_re, b_im, c_re, c_im, d_skip, w_glu, b_glu, ln1_g, ln1_b, w_up, conv_w, conv_b, w_down, ln2_g, ln2_b):
    depth = w_in.shape[0]
    alpha = (2 * depth) ** 0.25
    bp = x_prompt.shape[0]
    d_ff2 = w_up.shape[2]
    zero_h = jnp.zeros((bp, N_GROUPS, SSM_STATE), F32)
    zero_conv = jnp.zeros((bp, CONV_W - 1, d_ff2), F32)
    layer_params = (w_in, b_gate, w_pa, w_pb, w_o, a_re, a_im, log_dt, b_re, b_im, c_re, c_im, d_skip, w_glu, b_glu,
                    ln1_g, ln1_b, w_up, conv_w, conv_b, w_down, ln2_g, ln2_b)
    xp, xs = x_prompt, x_sample
    outs_p, outs_s = [], []
    for l in range(depth):
        weights = _layer_weights(*(a[l] for a in layer_params))
        rp = _trunk_layer(xp, None, None, None, zero_h, zero_h, zero_conv, weights, alpha)
        rs = _trunk_layer(xs, cache_k[l], cache_v[l], cache_idx_k[l], state_ssm_re[l].astype(F32),
                          state_ssm_im[l].astype(F32), state_conv[l], weights, alpha)
        xp, xs = rp[0], rs[0]
        outs_p.append(rp[1:])
        outs_s.append(rs[1:])
    stack = lambda outs: tuple(jnp.stack([o[j] for o in outs], axis=0) for j in range(6))
    return (xp, xs) + stack(outs_p) + stack(outs_s)
```
